```python
import math
import jax, jax.numpy as jnp
from jax import lax
import numpy as np

D_MODEL = 1024
BATCH = 4
SEQ = 8192
DEPTH = 4

N_MIXERS = 2
N_CONV_LAYERS = (DEPTH + 1) // 2
N_ATTN_LAYERS = DEPTH // 2
WIDTH = D_MODEL
N_PROJ = 4
CONV_K = 3
N_DIFF_HEADS = 8
HEAD_DIM = 64
V_DIM = 2 * HEAD_DIM
N_MAPS = 2 * N_DIFF_HEADS
NUM_BUCKETS = 32
MAX_EXACT = NUM_BUCKETS // 2
REL_MAX_DIST = 128
Q_BLOCK = 128
RMS_EPS = 1e-6

kernel_name = "hybrid_shortconv_diffattn_trunk"


def rmsnorm(x, g):
    xf = x.astype(jnp.float32)
    y = xf * lax.rsqrt(jnp.mean(xf * xf, axis=-1, keepdims=True) + RMS_EPS)
    return (y * g.astype(jnp.float32)).astype(x.dtype)


def t5_causal_bucket(dist):
    d_safe = jnp.maximum(dist, 1).astype(jnp.float32)
    large = MAX_EXACT + (jnp.log(d_safe / MAX_EXACT) / math.log(REL_MAX_DIST / MAX_EXACT)
                         * (NUM_BUCKETS - MAX_EXACT)).astype(jnp.int32)
    large = jnp.minimum(large, NUM_BUCKETS - 1)
    return jnp.where(dist < MAX_EXACT, dist, large)


def lambda_init_for(layer_idx):
    return 0.8 - 0.6 * math.exp(-0.3 * layer_idx)


def short_conv_mixer(h, w_in, w_out, conv_w):
    s = h.shape[1]
    proj = h @ w_in
    b_gate, c_gate, u, z = jnp.split(proj, N_PROJ, axis=-1)
    v = c_gate * u
    vp = jnp.pad(v, ((0, 0), (CONV_K - 1, 0), (0, 0)))
    conv = (vp[:, 0:s] * conv_w[:, 0] + vp[:, 1:s + 1] * conv_w[:, 1]
            + vp[:, 2:s + 2] * conv_w[:, 2])
    y = b_gate * conv * jax.nn.silu(z)
    return y @ w_out


def diff_attn_mixer(h, w_in, w_out, lq1, lk1, lq2, lk2, subln_g, bias_dist, lambda_init):
    bsz, s = h.shape[0], h.shape[1]
    proj = h @ w_in
    q, k, v, z = jnp.split(proj, N_PROJ, axis=-1)
    q = q.reshape(bsz, s, N_MAPS, HEAD_DIM)
    k = k.reshape(bsz, s, N_MAPS, HEAD_DIM)
    v = v.reshape(bsz, s, N_DIFF_HEADS, V_DIM)
    lam = (jnp.exp(jnp.sum(lq1.astype(jnp.float32) * lk1.astype(jnp.float32)))
           - jnp.exp(jnp.sum(lq2.astype(jnp.float32) * lk2.astype(jnp.float32)))
           + lambda_init)
    scale = HEAD_DIM ** -0.5
    n_blk = s // Q_BLOCK
    q_blocks = q.reshape(bsz, n_blk, Q_BLOCK, N_MAPS, HEAD_DIM).transpose(1, 0, 2, 3, 4)
    k_pos = jnp.arange(s, dtype=jnp.int32)

    def block(args):
        q_blk, blk_idx = args
        q_pos = blk_idx * Q_BLOCK + jnp.arange(Q_BLOCK, dtype=jnp.int32)
        logits = jnp.einsum('bqhd,bkhd->bhqk', q_blk, k,
                            preferred_element_type=jnp.float32) * scale
        dist = q_pos[:, None] - k_pos[None, :]
        bias = bias_dist[:, jnp.maximum(dist, 0)].astype(jnp.float32)
        logits = jnp.where(dist[None, None] >= 0, logits + bias[None], -jnp.inf)
        p = jax.nn.softmax(logits, axis=-1).reshape(bsz, N_DIFF_HEADS, 2, Q_BLOCK, s)
        a = p[:, :, 0] - lam * p[:, :, 1]
        o = jnp.einsum('bhqk,bkhe->bqhe', a.astype(v.dtype), v)
        return rmsnorm(o, subln_g) * (1.0 - lambda_init)

    o = lax.map(block, (q_blocks, jnp.arange(n_blk, dtype=jnp.int32)))
    o = o.transpose(1, 0, 2, 3, 4).reshape(bsz, s, N_DIFF_HEADS * V_DIM)
    return (o * jax.nn.silu(z)) @ w_out


def setup_inputs(seed: int = 0) -> dict:
    key = jax.random.key(seed)
    ks = jax.random.split(key, 13)
    f32 = jnp.float32
    x = jax.random.normal(ks[0], (BATCH, SEQ, D_MODEL), f32)
    norm_g = 1.0 + 0.02 * jax.random.normal(ks[1], (DEPTH, D_MODEL), f32)
    w_in = jax.random.normal(ks[2], (DEPTH, D_MODEL, N_PROJ * WIDTH), f32) * D_MODEL ** -0.5
    w_out = jax.random.normal(ks[3], (DEPTH, WIDTH, D_MODEL), f32) * WIDTH ** -0.5
    conv_w = jax.random.normal(ks[4], (N_CONV_LAYERS, WIDTH, CONV_K), f32) * CONV_K ** -0.5
    lambda_q1 = 0.1 * jax.random.normal(ks[5], (N_ATTN_LAYERS, HEAD_DIM), f32)
    lambda_k1 = 0.1 * jax.random.normal(ks[6], (N_ATTN_LAYERS, HEAD_DIM), f32)
    lambda_q2 = 0.1 * jax.random.normal(ks[7], (N_ATTN_LAYERS, HEAD_DIM), f32)
    lambda_k2 = 0.1 * jax.random.normal(ks[8], (N_ATTN_LAYERS, HEAD_DIM), f32)
    subln_g = 1.0 + 0.02 * jax.random.normal(ks[9], (N_ATTN_LAYERS, V_DIM), f32)
    rel_bias = 0.5 * jax.random.normal(ks[10], (NUM_BUCKETS, N_MAPS), f32)
    final_g = 1.0 + 0.02 * jax.random.normal(ks[11], (D_MODEL,), f32)
    return {"x": x, "norm_g": norm_g, "w_in": w_in, "w_out": w_out, "conv_w": conv_w,
            "lambda_q1": lambda_q1, "lambda_k1": lambda_k1, "lambda_q2": lambda_q2,
            "lambda_k2": lambda_k2, "subln_g": subln_g, "rel_bias": rel_bias,
            "final_g": final_g}


def reference(x, norm_g, w_in, w_out, conv_w, lambda_q1, lambda_k1, lambda_q2, lambda_k2,
              subln_g, rel_bias, final_g):
    s = x.shape[1]
    bias_dist = rel_bias[t5_causal_bucket(jnp.arange(s, dtype=jnp.int32))].T
    for i in range(DEPTH):
        h = rmsnorm(x, norm_g[i])
        j = i // N_MIXERS
        if i % N_MIXERS == 0:
            out = short_conv_mixer(h, w_in[i], w_out[i], conv_w[j])
        else:
            out = diff_attn_mixer(h, w_in[i], w_out[i], lambda_q1[j], lambda_k1[j],
                                  lambda_q2[j], lambda_k2[j], subln_g[j], bias_dist,
                                  lambda_init_for(i))
        x = x + out
    return rmsnorm(x, final_g)
```

```python
import functools
import math

import jax
import jax.numpy as jnp
from jax import lax
from jax.experimental import pallas as pl
from jax.experimental.pallas import tpu as pltpu

N_MIXERS = 2
CONV_K = 3
N_DIFF_HEADS = 8
HEAD_DIM = 64
V_DIM = 2 * HEAD_DIM
N_MAPS = 2 * N_DIFF_HEADS
NUM_BUCKETS = 32
MAX_EXACT = NUM_BUCKETS // 2
REL_MAX_DIST = 128
RMS_EPS = 1e-6

SUBLANES = 8
ROW_TILE = 256
ATTN_TILE = 256
VMEM_LIMIT_BYTES = 56 * 1024 * 1024

assert ATTN_TILE >= REL_MAX_DIST

_F32 = jnp.float32
_BF16 = jnp.bfloat16


def _dot(a, b):
    return jnp.dot(a, b, preferred_element_type=_F32)


def _rmsnorm_rows(x, g):
    return x * lax.rsqrt(jnp.mean(x * x, axis=-1, keepdims=True) + RMS_EPS) * g


def _const_spec(shape):
    return pl.BlockSpec(shape, lambda *_: (0,) * len(shape), pipeline_mode=pl.Buffered(1))


def _params(semantics):
    return pltpu.CompilerParams(dimension_semantics=semantics,
                                vmem_limit_bytes=VMEM_LIMIT_BYTES)


def _shift_rows(v, prev, s):
    r = pltpu.roll(v, s, 0)
    p = pltpu.roll(prev, s, 0)
    row = lax.broadcasted_iota(jnp.int32, prev.shape, 0)
    head = jnp.where(row < s, p, r[:SUBLANES])
    return jnp.concatenate([head, r[SUBLANES:]], axis=0)


def _conv_layer_kernel(x_ref, g_ref, win_ref, wout_ref, cw_ref, o_ref, carry_ref, *,
                       tiles_per_seq, width):
    w = width
    x = x_ref[...]
    h = _rmsnorm_rows(x, g_ref[...]).astype(_BF16)
    b_gate = _dot(h, win_ref[:, 0:w])
    c_gate = _dot(h, win_ref[:, w:2 * w])
    u = _dot(h, win_ref[:, 2 * w:3 * w])
    z = _dot(h, win_ref[:, 3 * w:4 * w])
    v = c_gate * u

    @pl.when(pl.program_id(0) % tiles_per_seq == 0)
    def _():
        carry_ref[...] = jnp.zeros_like(carry_ref)

    prev = carry_ref[...]
    conv = (_shift_rows(v, prev, 2) * cw_ref[0:1, :] + _shift_rows(v, prev, 1) * cw_ref[1:2, :]
            + v * cw_ref[2:3, :])
    carry_ref[...] = v[v.shape[0] - SUBLANES:]
    y = (b_gate * conv * jax.nn.silu(z)).astype(_BF16)
    o_ref[...] = x + _dot(y, wout_ref[...])


def _conv_layer(x2, g, w_in, w_out, conv_w, seq):
    n, d = x2.shape
    w = w_out.shape[0]
    tm = ROW_TILE
    return pl.pallas_call(
        functools.partial(_conv_layer_kernel, tiles_per_seq=seq // tm, width=w),
        grid=(n // tm,),
        in_specs=[
            pl.BlockSpec((tm, d), lambda i: (i, 0)),
            _const_spec((1, d)),
            _const_spec((d, 4 * w)),
            _const_spec((w, d)),
            _const_spec((CONV_K, w)),
        ],
        out_specs=pl.BlockSpec((tm, d), lambda i: (i, 0)),
        out_shape=jax.ShapeDtypeStruct((n, d), _F32),
        scratch_shapes=[pltpu.VMEM((SUBLANES, w), _F32)],
        compiler_params=_params(("arbitrary",)),
        name="conv_layer",
    )(x2, g.reshape(1, d), w_in.astype(_BF16), w_out.astype(_BF16), conv_w.T)


def _qkv_kernel(x_ref, g_ref, w_ref, q_ref, k_ref, v_ref, *, width):
    w = width
    h = _rmsnorm_rows(x_ref[...], g_ref[...]).astype(_BF16)
    q_ref[...] = (_dot(h, w_ref[:, 0:w]) * (HEAD_DIM ** -0.5)).astype(_BF16)
    k_ref[...] = _dot(h, w_ref[:, w:2 * w]).astype(_BF16)
    v_ref[...] = _dot(h, w_ref[:, 2 * w:3 * w]).astype(_BF16)


def _qkv_proj(x2, g, w_qkv):
    n, d = x2.shape
    w = w_qkv.shape[1] // 3
    tm = ROW_TILE
    out = jax.ShapeDtypeStruct((n, w), _BF16)
    return pl.pallas_call(
        functools.partial(_qkv_kernel, width=w),
        grid=(n // tm,),
        in_specs=[
            pl.BlockSpec((tm, d), lambda i: (i, 0)),
            _const_spec((1, d)),
            _const_spec((d, 3 * w)),
        ],
        out_specs=[pl.BlockSpec((tm, w), lambda i: (i, 0))] * 3,
        out_shape=[out, out, out],
        compiler_params=_params(("arbitrary",)),
        name="qkv_proj",
    )(x2, g.reshape(1, d), w_qkv)


def _attn_kernel(q_ref, k_ref, vt_ref, bias_ref, lam_ref, sg_ref, o_ref, m_sc, l_sc, acc_sc, *,
                 lambda_init):
    t = q_ref.shape[0]
    qi = pl.program_id(2)
    q = q_ref[...]
    lane = lax.broadcasted_iota(jnp.int32, q.shape, 1)
    zero = jnp.zeros_like(q)
    qq = jnp.concatenate([jnp.where(lane < HEAD_DIM, q, zero),
                          jnp.where(lane >= HEAD_DIM, q, zero)], axis=0)

    m_sc[...] = jnp.full_like(m_sc, -jnp.inf)
    l_sc[...] = jnp.zeros_like(l_sc)
    acc_sc[...] = jnp.zeros_like(acc_sc)

    def tile(j, bias):
        kj = k_ref[pl.ds(pl.multiple_of(j * t, t), t), :]
        s = lax.dot_general(kj, qq, (((1,), (1,)), ((), ())), preferred_element_type=_F32)
        if bias is not None:
            s = s + bias
        m_prev = m_sc[...]
        m_new = jnp.maximum(m_prev, jnp.max(s, axis=0, keepdims=True))
        alpha = jnp.exp(m_prev - m_new)
        p = jnp.exp(s - m_new)
        l_sc[...] = alpha * l_sc[...] + jnp.sum(p, axis=0, keepdims=True)
        acc_sc[...] = acc_sc[...] * alpha + _dot(vt_ref[j], p.astype(_BF16))
        m_sc[...] = m_new

    def plain(j, carry):
        tile(j, None)
        return carry

    lax.fori_loop(0, jnp.maximum(qi - 1, 0), plain, 0)

    @pl.when(qi > 0)
    def _():
        tile(qi - 1, bias_ref[1])

    tile(qi, bias_ref[0])

    lam = (jnp.exp(jnp.sum(lam_ref[0:1, :] * lam_ref[1:2, :], axis=-1, keepdims=True))
           - jnp.exp(jnp.sum(lam_ref[2:3, :] * lam_ref[3:4, :], axis=-1, keepdims=True))
           + lambda_init)
    o_maps = acc_sc[...] / l_sc[...]
    o = (o_maps[:, :t] - lam * o_maps[:, t:]).T
    o_ref[...] = _rmsnorm_rows(o, sg_ref[...]) * (1.0 - lambda_init)


def _attention(q, k, vt, bias_tiles, lam_params, subln_g, lambda_init):
    bsz, s, w = q.shape
    t = ATTN_TILE
    n_heads = w // V_DIM
    return pl.pallas_call(
        functools.partial(_attn_kernel, lambda_init=lambda_init),
        grid=(bsz, n_heads, s // t),
        in_specs=[
            pl.BlockSpec((None, t, V_DIM), lambda b, h, i: (b, i, h)),
            pl.BlockSpec((None, s, V_DIM), lambda b, h, i: (b, 0, h)),
            pl.BlockSpec((None, None, s // t, V_DIM, t), lambda b, h, i: (b, h, 0, 0, 0)),
            pl.BlockSpec((None, 2, t, 2 * t), lambda b, h, i: (h, 0, 0, 0)),
            _const_spec((4, HEAD_DIM)),
            _const_spec((1, V_DIM)),
        ],
        out_specs=pl.BlockSpec((None, t, V_DIM), lambda b, h, i: (b, i, h)),
        out_shape=jax.ShapeDtypeStruct((bsz, s, w), _F32),
        scratch_shapes=[pltpu.VMEM((1, 2 * t), _F32), pltpu.VMEM((1, 2 * t), _F32),
                        pltpu.VMEM((V_DIM, 2 * t), _F32)],
        compiler_params=_params(("arbitrary", "arbitrary", "arbitrary")),
        name="diff_attention",
    )(q, k, vt, bias_tiles, lam_params, subln_g.reshape(1, V_DIM))


def _attn_out_kernel(x_ref, o_ref, g_ref, wz_ref, wout_ref, fg_ref, y_ref, *, final_norm):
    x = x_ref[...]
    h = _rmsnorm_rows(x, g_ref[...]).astype(_BF16)
    z = _dot(h, wz_ref[...])
    y = (o_ref[...] * jax.nn.silu(z)).astype(_BF16)
    x_new = x + _dot(y, wout_ref[...])
    if final_norm:
        x_new = _rmsnorm_rows(x_new, fg_ref[...])
    y_ref[...] = x_new


def _attn_out(x2, o2, g, w_z, w_out, final_g, final_norm):
    n, d = x2.shape
    w = w_out.shape[0]
    tm = ROW_TILE
    return pl.pallas_call(
        functools.partial(_attn_out_kernel, final_norm=final_norm),
        grid=(n // tm,),
        in_specs=[
            pl.BlockSpec((tm, d), lambda i: (i, 0)),
            pl.BlockSpec((tm, w), lambda i: (i, 0)),
            _const_spec((1, d)),
            _const_spec((d, w)),
            _const_spec((w, d)),
            _const_spec((1, d)),
        ],
        out_specs=pl.BlockSpec((tm, d), lambda i: (i, 0)),
        out_shape=jax.ShapeDtypeStruct((n, d), _F32),
        compiler_params=_params(("arbitrary",)),
        name="attn_out",
    )(x2, o2, g.reshape(1, d), w_z, w_out, final_g.reshape(1, d))


def _t5_causal_bucket(dist):
    d_safe = jnp.maximum(dist, 1).astype(_F32)
    large = MAX_EXACT + (jnp.log(d_safe / MAX_EXACT) / math.log(REL_MAX_DIST / MAX_EXACT)
                         * (NUM_BUCKETS - MAX_EXACT)).astype(jnp.int32)
    large = jnp.minimum(large, NUM_BUCKETS - 1)
    return jnp.where(dist < MAX_EXACT, dist, large)


def _bias_tiles(rel_bias, t):
    dist = jnp.arange(2 * t, dtype=jnp.int32)
    rel = (rel_bias[_t5_causal_bucket(dist)] - rel_bias[NUM_BUCKETS - 1]).T
    delta = dist[None, :t] - dist[:t, None]
    diag = jnp.where(delta >= 0, rel[:, jnp.maximum(delta, 0)], -jnp.inf)
    prev = rel[:, delta + t]
    tiles = jnp.stack([diag, prev], axis=1)
    tiles = tiles.reshape(N_DIFF_HEADS, 2, 2, t, t).transpose(0, 2, 3, 1, 4)
    return tiles.reshape(N_DIFF_HEADS, 2, t, 2 * t)


def _lambda_init_for(layer_idx):
    return 0.8 - 0.6 * math.exp(-0.3 * layer_idx)


def kernel(x, norm_g, w_in, w_out, conv_w, lambda_q1, lambda_k1, lambda_q2, lambda_k2, subln_g,
           rel_bias, final_g):
    bsz, seq, d = x.shape
    depth = norm_g.shape[0]
    w = w_out.shape[1]
    t = ATTN_TILE
    assert seq % t == 0 and seq % ROW_TILE == 0 and w == N_DIFF_HEADS * V_DIM

    bias_tiles = _bias_tiles(rel_bias, t)
    x2 = x.reshape(bsz * seq, d)
    for i in range(depth):
        j = i // N_MIXERS
        if i % N_MIXERS == 0:
            x2 = _conv_layer(x2, norm_g[i], w_in[i], w_out[i], conv_w[j], seq)
        else:
            w_in_bf = w_in[i].astype(_BF16)
            q, k, v = _qkv_proj(x2, norm_g[i], w_in_bf[:, :3 * w])
            vt = (v.reshape(bsz, seq // t, t, N_DIFF_HEADS, V_DIM).transpose(0, 3, 1, 4, 2))
            lam_params = jnp.stack([lambda_q1[j], lambda_k1[j], lambda_q2[j], lambda_k2[j]])
            o = _attention(q.reshape(bsz, seq, w), k.reshape(bsz, seq, w), vt, bias_tiles,
                           lam_params, subln_g[j], _lambda_init_for(i))
            x2 = _attn_out(x2, o.reshape(bsz * seq, w), norm_g[i], w_in_bf[:, 3 * w:],
                           w_out[i].astype(_BF16), final_g, final_norm=(i == depth - 1))
    if depth % N_MIXERS == 1:
        raise NotImplementedError("final norm is fused into the last attention layer")
    return x2.reshape(bsz, seq, d)
```

```python
import functools
import math

import jax
import jax.numpy as jnp
from jax import lax
from jax.experimental import pallas as pl
from jax.experimental.pallas import tpu as pltpu

N_MIXERS = 2
CONV_K = 3
N_DIFF_HEADS = 8
HEAD_DIM = 64
V_DIM = 2 * HEAD_DIM
N_MAPS = 2 * N_DIFF_HEADS
NUM_BUCKETS = 32
MAX_EXACT = NUM_BUCKETS // 2
REL_MAX_DIST = 128
RMS_EPS = 1e-6

SUBLANES = 8
ROW_TILE = 256
ATTN_Q_TILE = 512
ATTN_K_TILE = 256
VMEM_LIMIT_BYTES = 56 * 1024 * 1024

assert ATTN_Q_TILE == 2 * ATTN_K_TILE and ATTN_K_TILE >= REL_MAX_DIST

_F32 = jnp.float32
_BF16 = jnp.bfloat16
_LOG2E = math.log2(math.e)


def _dot(a, b):
    return jnp.dot(a, b, preferred_element_type=_F32)


def _rmsnorm_rows(x, g):
    return x * lax.rsqrt(jnp.mean(x * x, axis=-1, keepdims=True) + RMS_EPS) * g


def _const_spec(shape):
    return pl.BlockSpec(shape, lambda *_: (0,) * len(shape), pipeline_mode=pl.Buffered(1))


def _params(semantics):
    return pltpu.CompilerParams(dimension_semantics=semantics,
                                vmem_limit_bytes=VMEM_LIMIT_BYTES)


def _shift_rows(v, prev, s):
    r = pltpu.roll(v, s, 0)
    p = pltpu.roll(prev, s, 0)
    row = lax.broadcasted_iota(jnp.int32, prev.shape, 0)
    head = jnp.where(row < s, p, r[:SUBLANES])
    return jnp.concatenate([head, r[SUBLANES:]], axis=0)


def _conv_layer_kernel(x_ref, g_ref, win_ref, wout_ref, cw_ref, o_ref, carry_ref, *,
                       tiles_per_seq, width):
    w = width
    x = x_ref[...]
    h = _rmsnorm_rows(x, g_ref[...]).astype(_BF16)
    b_gate = _dot(h, win_ref[:, 0:w])
    c_gate = _dot(h, win_ref[:, w:2 * w])
    u = _dot(h, win_ref[:, 2 * w:3 * w])
    z = _dot(h, win_ref[:, 3 * w:4 * w])
    v = c_gate * u

    @pl.when(pl.program_id(0) % tiles_per_seq == 0)
    def _():
        carry_ref[...] = jnp.zeros_like(carry_ref)

    prev = carry_ref[...]
    conv = (_shift_rows(v, prev, 2) * cw_ref[0:1, :] + _shift_rows(v, prev, 1) * cw_ref[1:2, :]
            + v * cw_ref[2:3, :])
    carry_ref[...] = v[v.shape[0] - SUBLANES:]
    y = (b_gate * conv * jax.nn.silu(z)).astype(_BF16)
    o_ref[...] = x + _dot(y, wout_ref[...])


def _conv_layer(x2, g, w_in, w_out, conv_w, seq):
    n, d = x2.shape
    w = w_out.shape[0]
    tm = ROW_TILE
    return pl.pallas_call(
        functools.partial(_conv_layer_kernel, tiles_per_seq=seq // tm, width=w),
        grid=(n // tm,),
        in_specs=[
            pl.BlockSpec((tm, d), lambda i: (i, 0)),
            _const_spec((1, d)),
            _const_spec((d, 4 * w)),
            _const_spec((w, d)),
            _const_spec((CONV_K, w)),
        ],
        out_specs=pl.BlockSpec((tm, d), lambda i: (i, 0)),
        out_shape=jax.ShapeDtypeStruct((n, d), _F32),
        scratch_shapes=[pltpu.VMEM((SUBLANES, w), _F32)],
        compiler_params=_params(("arbitrary",)),
        name="conv_layer",
    )(x2, g.reshape(1, d), w_in.astype(_BF16), w_out.astype(_BF16), conv_w.T)


def _qkv_kernel(x_ref, g_ref, w_ref, q_ref, k_ref, v_ref, *, width):
    w = width
    h = _rmsnorm_rows(x_ref[...], g_ref[...]).astype(_BF16)
    q_ref[...] = (_dot(h, w_ref[:, 0:w]) * (HEAD_DIM ** -0.5 * _LOG2E)).astype(_BF16)
    k_ref[...] = _dot(h, w_ref[:, w:2 * w]).astype(_BF16)
    v_ref[...] = _dot(h, w_ref[:, 2 * w:3 * w]).astype(_BF16)


def _qkv_proj(x2, g, w_qkv):
    n, d = x2.shape
    w = w_qkv.shape[1] // 3
    tm = ROW_TILE
    out = jax.ShapeDtypeStruct((n, w), _BF16)
    return pl.pallas_call(
        functools.partial(_qkv_kernel, width=w),
        grid=(n // tm,),
        in_specs=[
            pl.BlockSpec((tm, d), lambda i: (i, 0)),
            _const_spec((1, d)),
            _const_spec((d, 3 * w)),
        ],
        out_specs=[pl.BlockSpec((tm, w), lambda i: (i, 0))] * 3,
        out_shape=[out, out, out],
        compiler_params=_params(("arbitrary",)),
        name="qkv_proj",
    )(x2, g.reshape(1, d), w_qkv)


def _attn_kernel(q_ref, k_ref, vt_ref, bias_ref, lam_ref, sg_ref, o_ref,
                 qq_sc, m_sc, l_sc, acc_sc, s0, s1, t0, t1, p0, p1, a0, a1, *, lambda_init):
    tq = q_ref.shape[0]
    tk = s0.shape[0]
    qi = pl.program_id(2)
    n_tiles = 2 * qi + 2
    slots = ((s0, t0, p0, a0), (s1, t1, p1, a1))

    q = q_ref[...]
    lane = lax.broadcasted_iota(jnp.int32, q.shape, 1)
    zero = jnp.zeros_like(q)
    qq_sc[0:tq, :] = jnp.where(lane < HEAD_DIM, q, zero)
    qq_sc[tq:2 * tq, :] = jnp.where(lane >= HEAD_DIM, q, zero)

    m_sc[...] = jnp.full_like(m_sc, -jnp.inf)
    l_sc[...] = jnp.zeros_like(l_sc)
    acc_sc[...] = jnp.zeros_like(acc_sc)

    def stage_a(j, slot, bias):
        s_ref, t_ref, _, _ = slots[slot]
        kj = k_ref[pl.ds(pl.multiple_of(j * tk, tk), tk), :]
        s = lax.dot_general(kj, qq_sc[...], (((1,), (1,)), ((), ())),
                            preferred_element_type=_F32)
        if bias is not None:
            s = s + bias
        s_ref[...] = s
        t_ref[...] = jnp.max(s, axis=0, keepdims=True)

    def stage_b(slot):
        s_ref, t_ref, p_ref, a_ref = slots[slot]
        m_prev = m_sc[...]
        m_new = jnp.maximum(m_prev, t_ref[...])
        alpha = jnp.exp2(m_prev - m_new)
        p = jnp.exp2(s_ref[...] - m_new)
        l_sc[...] = alpha * l_sc[...] + jnp.sum(p, axis=0, keepdims=True)
        p_ref[...] = p.astype(_BF16)
        a_ref[...] = alpha
        m_sc[...] = m_new

    def stage_c(j, slot):
        _, _, p_ref, a_ref = slots[slot]
        acc_sc[...] = acc_sc[...] * a_ref[...] + _dot(vt_ref[j], p_ref[...])

    def step_pair(u, bias_odd, bias_even):
        i = 2 * u
        stage_a(i + 1, 1, bias_odd)
        stage_b(0)
        stage_c(jnp.maximum(i - 1, 0), 1)
        stage_a(i + 2, 0, bias_even)
        stage_b(1)
        stage_c(i, 0)

    @pl.when(qi == 0)
    def _():
        stage_a(0, 0, bias_ref[1])
        stage_b(0)
        stage_c(0, 0)
        stage_a(1, 1, bias_ref[2])
        stage_b(1)
        stage_c(1, 1)

    @pl.when(qi > 0)
    def _():
        p1[...] = jnp.zeros_like(p1)
        a1[...] = jnp.ones_like(a1)
        stage_a(0, 0, None)

        def plain(u, carry):
            step_pair(u, None, None)
            return carry

        lax.fori_loop(0, qi - 1, plain, 0)
        step_pair(qi - 1, bias_ref[0], bias_ref[1])
        stage_a(n_tiles - 1, 1, bias_ref[2])
        stage_b(0)
        stage_c(n_tiles - 3, 1)
        stage_b(1)
        stage_c(n_tiles - 2, 0)
        stage_c(n_tiles - 1, 1)

    lam = (jnp.exp(jnp.sum(lam_ref[0:1, :] * lam_ref[1:2, :], axis=-1, keepdims=True))
           - jnp.exp(jnp.sum(lam_ref[2:3, :] * lam_ref[3:4, :], axis=-1, keepdims=True))
           + lambda_init)
    o_maps = acc_sc[...] / l_sc[...]
    o = (o_maps[:, :tq] - lam * o_maps[:, tq:]).T
    o_ref[...] = _rmsnorm_rows(o, sg_ref[...]) * (1.0 - lambda_init)


def _attention(q, k, vt, bias_tiles, lam_params, subln_g, lambda_init):
    bsz, s, w = q.shape
    tq, tk = ATTN_Q_TILE, ATTN_K_TILE
    n_heads = w // V_DIM

    def col(dtype):
        return pltpu.VMEM((1, 2 * tq), dtype)

    def tile(dtype):
        return pltpu.VMEM((tk, 2 * tq), dtype)

    return pl.pallas_call(
        functools.partial(_attn_kernel, lambda_init=lambda_init),
        grid=(bsz, n_heads, s // tq),
        in_specs=[
            pl.BlockSpec((None, tq, V_DIM), lambda b, h, i: (b, i, h)),
            pl.BlockSpec((None, s, V_DIM), lambda b, h, i: (b, 0, h)),
            pl.BlockSpec((None, None, s // tk, V_DIM, tk), lambda b, h, i: (b, h, 0, 0, 0)),
            pl.BlockSpec((None, 3, tk, 2 * tq), lambda b, h, i: (h, 0, 0, 0)),
            _const_spec((4, HEAD_DIM)),
            _const_spec((1, V_DIM)),
        ],
        out_specs=pl.BlockSpec((None, tq, V_DIM), lambda b, h, i: (b, i, h)),
        out_shape=jax.ShapeDtypeStruct((bsz, s, w), _F32),
        scratch_shapes=[
            pltpu.VMEM((2 * tq, V_DIM), _BF16),
            col(_F32), col(_F32),
            pltpu.VMEM((V_DIM, 2 * tq), _F32),
            tile(_F32), tile(_F32),
            col(_F32), col(_F32),
            tile(_BF16), tile(_BF16),
            col(_F32), col(_F32),
        ],
        compiler_params=_params(("arbitrary", "arbitrary", "arbitrary")),
        name="diff_attention",
    )(q, k, vt, bias_tiles, lam_params, subln_g.reshape(1, V_DIM))


def _attn_out_kernel(x_ref, o_ref, g_ref, wz_ref, wout_ref, fg_ref, y_ref, *, final_norm):
    x = x_ref[...]
    h = _rmsnorm_rows(x, g_ref[...]).astype(_BF16)
    z = _dot(h, wz_ref[...])
    y = (o_ref[...] * jax.nn.silu(z)).astype(_BF16)
    x_new = x + _dot(y, wout_ref[...])
    if final_norm:
        x_new = _rmsnorm_rows(x_new, fg_ref[...])
    y_ref[...] = x_new


def _attn_out(x2, o2, g, w_z, w_out, final_g, final_norm):
    n, d = x2.shape
    w = w_out.shape[0]
    tm = ROW_TILE
    return pl.pallas_call(
        functools.partial(_attn_out_kernel, final_norm=final_norm),
        grid=(n // tm,),
        in_specs=[
            pl.BlockSpec((tm, d), lambda i: (i, 0)),
            pl.BlockSpec((tm, w), lambda i: (i, 0)),
            _const_spec((1, d)),
            _const_spec((d, w)),
            _const_spec((w, d)),
            _const_spec((1, d)),
        ],
        out_specs=pl.BlockSpec((tm, d), lambda i: (i, 0)),
        out_shape=jax.ShapeDtypeStruct((n, d), _F32),
        compiler_params=_params(("arbitrary",)),
        name="attn_out",
    )(x2, o2, g.reshape(1, d), w_z, w_out, final_g.reshape(1, d))


def _t5_causal_bucket(dist):
    d_safe = jnp.maximum(dist, 1).astype(_F32)
    large = MAX_EXACT + (jnp.log(d_safe / MAX_EXACT) / math.log(REL_MAX_DIST / MAX_EXACT)
                         * (NUM_BUCKETS - MAX_EXACT)).astype(jnp.int32)
    large = jnp.minimum(large, NUM_BUCKETS - 1)
    return jnp.where(dist < MAX_EXACT, dist, large)


def _toeplitz(g_vec, rows, cols):
    period = g_vec.shape[-1]
    flat = jnp.tile(g_vec, (1,) * (g_vec.ndim - 1) + (rows,))
    flat = flat[..., :rows * (period - 1)]
    return flat.reshape(g_vec.shape[:-1] + (rows, period - 1))[..., :cols]


def _bias_tiles(rel_bias, tq, tk):
    period = 2 * (tq + tk)
    delta = jnp.arange(period, dtype=jnp.int32)
    delta = jnp.where(delta >= period // 2, delta - period, delta)
    n_dist = tq + tk
    rel = ((rel_bias[_t5_causal_bucket(jnp.arange(n_dist, dtype=jnp.int32))]
            - rel_bias[NUM_BUCKETS - 1]) * _LOG2E).T

    def vec(offset):
        dist = delta + offset
        vals = rel[:, jnp.clip(dist, 0, n_dist - 1)]
        return jnp.where(dist >= 0, vals, -jnp.inf)

    g_vec = jnp.stack([vec(tk), vec(0), vec(-tk)], axis=1)
    tiles = _toeplitz(g_vec, tk, tq)
    tiles = tiles.reshape(N_DIFF_HEADS, 2, 3, tk, tq).transpose(0, 2, 3, 1, 4)
    return tiles.reshape(N_DIFF_HEADS, 3, tk, 2 * tq)


def _lambda_init_for(layer_idx):
    return 0.8 - 0.6 * math.exp(-0.3 * layer_idx)


def kernel(x, norm_g, w_in, w_out, conv_w, lambda_q1, lambda_k1, lambda_q2, lambda_k2, subln_g,
           rel_bias, final_g):
    bsz, seq, d = x.shape
    depth = norm_g.shape[0]
    w = w_out.shape[1]
    tq, tk = ATTN_Q_TILE, ATTN_K_TILE
    assert seq % tq == 0 and seq % ROW_TILE == 0 and w == N_DIFF_HEADS * V_DIM
    assert depth % N_MIXERS == 0

    bias_tiles = _bias_tiles(rel_bias, tq, tk)
    x2 = x.reshape(bsz * seq, d)
    for i in range(depth):
        j = i // N_MIXERS
        if i % N_MIXERS == 0:
            x2 = _conv_layer(x2, norm_g[i], w_in[i], w_out[i], conv_w[j], seq)
        else:
            w_in_bf = w_in[i].astype(_BF16)
            q, k, v = _qkv_proj(x2, norm_g[i], w_in_bf[:, :3 * w])
            vt = (v.reshape(bsz, seq // tk, tk, N_DIFF_HEADS, V_DIM).transpose(0, 3, 1, 4, 2))
            lam_params = jnp.stack([lambda_q1[j], lambda_k1[j], lambda_q2[j], lambda_k2[j]])
            o = _attention(q.reshape(bsz, seq, w), k.reshape(bsz, seq, w), vt, bias_tiles,
                           lam_params, subln_g[j], _lambda_init_for(i))
            x2 = _attn_out(x2, o.reshape(bsz * seq, w), norm_g[i], w_in_bf[:, 3 * w:],
                           w_out[i].astype(_BF16), final_g, final_norm=(i == depth - 1))
    return x2.reshape(bsz, seq, d)
```

```python
import functools
import math

import jax
import jax.numpy as jnp
from jax import lax
from jax.experimental import pallas as pl
from jax.experimental.pallas import tpu as pltpu

N_MIXERS = 2
CONV_K = 3
N_DIFF_HEADS = 8
HEAD_DIM = 64
V_DIM = 2 * HEAD_DIM
N_MAPS = 2 * N_DIFF_HEADS
NUM_BUCKETS = 32
MAX_EXACT = NUM_BUCKETS // 2
REL_MAX_DIST = 128
RMS_EPS = 1e-6

SUBLANES = 8
ROW_TILE = 256
ATTN_Q_TILE = 512
ATTN_K_TILE = 256
VMEM_LIMIT_BYTES = 56 * 1024 * 1024

assert ATTN_Q_TILE == 2 * ATTN_K_TILE and ATTN_K_TILE >= REL_MAX_DIST

_F32 = jnp.float32
_BF16 = jnp.bfloat16
_LOG2E = math.log2(math.e)


def _dot(a, b):
    return jnp.dot(a, b, preferred_element_type=_F32)


def _rmsnorm_rows(x, g):
    return x * lax.rsqrt(jnp.mean(x * x, axis=-1, keepdims=True) + RMS_EPS) * g


def _const_spec(shape):
    return pl.BlockSpec(shape, lambda *_: (0,) * len(shape), pipeline_mode=pl.Buffered(1))


def _params(semantics, flags=None):
    return pltpu.CompilerParams(dimension_semantics=semantics,
                                vmem_limit_bytes=VMEM_LIMIT_BYTES, flags=flags)


def _shift_rows(v, prev, s):
    r = pltpu.roll(v, s, 0)
    p = pltpu.roll(prev, s, 0)
    row = lax.broadcasted_iota(jnp.int32, prev.shape, 0)
    head = jnp.where(row < s, p, r[:SUBLANES])
    return jnp.concatenate([head, r[SUBLANES:]], axis=0)


def _conv_layer_kernel(x_ref, g_ref, win_ref, wout_ref, cw_ref, o_ref, carry_ref, *,
                       tiles_per_seq, width):
    w = width
    x = x_ref[...]
    h = _rmsnorm_rows(x, g_ref[...]).astype(_BF16)
    b_gate = _dot(h, win_ref[:, 0:w])
    c_gate = _dot(h, win_ref[:, w:2 * w])
    u = _dot(h, win_ref[:, 2 * w:3 * w])
    z = _dot(h, win_ref[:, 3 * w:4 * w])
    v = c_gate * u

    @pl.when(pl.program_id(0) % tiles_per_seq == 0)
    def _():
        carry_ref[...] = jnp.zeros_like(carry_ref)

    prev = carry_ref[...]
    conv = (_shift_rows(v, prev, 2) * cw_ref[0:1, :] + _shift_rows(v, prev, 1) * cw_ref[1:2, :]
            + v * cw_ref[2:3, :])
    carry_ref[...] = v[v.shape[0] - SUBLANES:]
    y = (b_gate * conv * jax.nn.silu(z)).astype(_BF16)
    o_ref[...] = x + _dot(y, wout_ref[...])


def _conv_layer(x2, g, w_in, w_out, conv_w, seq):
    n, d = x2.shape
    w = w_out.shape[0]
    tm = ROW_TILE
    return pl.pallas_call(
        functools.partial(_conv_layer_kernel, tiles_per_seq=seq // tm, width=w),
        grid=(n // tm,),
        in_specs=[
            pl.BlockSpec((tm, d), lambda i: (i, 0)),
            _const_spec((1, d)),
            _const_spec((d, 4 * w)),
            _const_spec((w, d)),
            _const_spec((CONV_K, w)),
        ],
        out_specs=pl.BlockSpec((tm, d), lambda i: (i, 0)),
        out_shape=jax.ShapeDtypeStruct((n, d), _F32),
        scratch_shapes=[pltpu.VMEM((SUBLANES, w), _F32)],
        compiler_params=_params(("arbitrary",)),
        name="conv_layer",
    )(x2, g.reshape(1, d), w_in.astype(_BF16), w_out.astype(_BF16), conv_w.T)


def _qkv_kernel(x_ref, g_ref, w_ref, q_ref, k_ref, v_ref, *, width):
    w = width
    h = _rmsnorm_rows(x_ref[...], g_ref[...]).astype(_BF16)
    q_ref[...] = (_dot(h, w_ref[:, 0:w]) * (HEAD_DIM ** -0.5 * _LOG2E)).astype(_BF16)
    k_ref[...] = _dot(h, w_ref[:, w:2 * w]).astype(_BF16)
    v_ref[...] = _dot(h, w_ref[:, 2 * w:3 * w]).astype(_BF16)


def _qkv_proj(x2, g, w_qkv):
    n, d = x2.shape
    w = w_qkv.shape[1] // 3
    tm = ROW_TILE
    out = jax.ShapeDtypeStruct((n, w), _BF16)
    return pl.pallas_call(
        functools.partial(_qkv_kernel, width=w),
        grid=(n // tm,),
        in_specs=[
            pl.BlockSpec((tm, d), lambda i: (i, 0)),
            _const_spec((1, d)),
            _const_spec((d, 3 * w)),
        ],
        out_specs=[pl.BlockSpec((tm, w), lambda i: (i, 0))] * 3,
        out_shape=[out, out, out],
        compiler_params=_params(("arbitrary",)),
        name="qkv_proj",
    )(x2, g.reshape(1, d), w_qkv)


def _attn_kernel(q_ref, k_ref, vt_ref, bias_ref, lam_ref, sg_ref, o_ref,
                 qq_sc, m_sc, l_sc, acc_sc, s0, s1, t0, t1, p0, p1, a0, a1, *, lambda_init):
    tq = q_ref.shape[0]
    tk = s0.shape[0]
    qi = pl.program_id(2)
    n_tiles = 2 * qi + 2
    slots = ((s0, t0, p0, a0), (s1, t1, p1, a1))

    q = q_ref[...]
    lane = lax.broadcasted_iota(jnp.int32, q.shape, 1)
    zero = jnp.zeros_like(q)
    qq_sc[0:tq, :] = jnp.where(lane < HEAD_DIM, q, zero)
    qq_sc[tq:2 * tq, :] = jnp.where(lane >= HEAD_DIM, q, zero)

    m_sc[...] = jnp.full_like(m_sc, -jnp.inf)
    l_sc[...] = jnp.zeros_like(l_sc)
    acc_sc[...] = jnp.zeros_like(acc_sc)

    def stage_a(j, slot, bias):
        s_ref, t_ref, _, _ = slots[slot]
        kj = k_ref[pl.ds(pl.multiple_of(j * tk, tk), tk), :]
        s = lax.dot_general(kj, qq_sc[...], (((1,), (1,)), ((), ())),
                            preferred_element_type=_F32)
        if bias is not None:
            s = s + bias
        s_ref[...] = s
        t_ref[...] = jnp.max(s, axis=0, keepdims=True)

    def stage_b(slot):
        s_ref, t_ref, p_ref, a_ref = slots[slot]
        m_prev = m_sc[...]
        m_new = jnp.maximum(m_prev, t_ref[...])
        alpha = jnp.exp2(m_prev - m_new)
        p = jnp.exp2(s_ref[...] - m_new)
        l_sc[...] = alpha * l_sc[...] + jnp.sum(p, axis=0, keepdims=True)
        p_ref[...] = p.astype(_BF16)
        a_ref[...] = alpha
        m_sc[...] = m_new

    def stage_c(j, slot):
        _, _, p_ref, a_ref = slots[slot]
        acc_sc[...] = acc_sc[...] * a_ref[...] + _dot(vt_ref[j], p_ref[...])

    def step_pair(u, bias_even, bias_odd, issue_a=True):
        i = 2 * u
        stage_c(jnp.maximum(i - 2, 0), 0)
        stage_c(jnp.maximum(i - 1, 0), 1)
        stage_b(0)
        stage_b(1)
        if issue_a:
            stage_a(i + 2, 0, bias_even)
            stage_a(i + 3, 1, bias_odd)

    def sequential(biases):
        for j, bias in enumerate(biases):
            stage_a(j, j % 2, None if bias is None else bias_ref[bias])
            stage_b(j % 2)
            stage_c(j, j % 2)

    prev, diag0, diag1 = 0, 1, 2

    @pl.when(qi == 0)
    def _():
        sequential([diag0, diag1])

    @pl.when(qi == 1)
    def _():
        sequential([None, prev, diag0, diag1])

    @pl.when(qi > 1)
    def _():
        for _, _, p_ref, a_ref in slots:
            p_ref[...] = jnp.zeros_like(p_ref)
            a_ref[...] = jnp.ones_like(a_ref)
        stage_a(0, 0, None)
        stage_a(1, 1, None)

        def plain(u, carry):
            step_pair(u, None, None)
            return carry

        lax.fori_loop(0, qi - 2, plain, 0)
        step_pair(qi - 2, None, bias_ref[prev])
        step_pair(qi - 1, bias_ref[diag0], bias_ref[diag1])
        step_pair(qi, None, None, issue_a=False)
        stage_c(n_tiles - 2, 0)
        stage_c(n_tiles - 1, 1)

    lam = (jnp.exp(jnp.sum(lam_ref[0:1, :] * lam_ref[1:2, :], axis=-1, keepdims=True))
           - jnp.exp(jnp.sum(lam_ref[2:3, :] * lam_ref[3:4, :], axis=-1, keepdims=True))
           + lambda_init)
    o_maps = acc_sc[...] / l_sc[...]
    o = (o_maps[:, :tq] - lam * o_maps[:, tq:]).T
    o_ref[...] = _rmsnorm_rows(o, sg_ref[...]) * (1.0 - lambda_init)


def _attention(q, k, vt, bias_tiles, lam_params, subln_g, lambda_init):
    bsz, s, w = q.shape
    tq, tk = ATTN_Q_TILE, ATTN_K_TILE
    n_heads = w // V_DIM

    def col(dtype):
        return pltpu.VMEM((1, 2 * tq), dtype)

    def tile(dtype):
        return pltpu.VMEM((tk, 2 * tq), dtype)

    return pl.pallas_call(
        functools.partial(_attn_kernel, lambda_init=lambda_init),
        grid=(bsz, n_heads, s // tq),
        in_specs=[
            pl.BlockSpec((None, tq, V_DIM), lambda b, h, i: (b, i, h)),
            pl.BlockSpec((None, s, V_DIM), lambda b, h, i: (b, 0, h)),
            pl.BlockSpec((None, None, s // tk, V_DIM, tk), lambda b, h, i: (b, h, 0, 0, 0)),
            pl.BlockSpec((None, 3, tk, 2 * tq), lambda b, h, i: (h, 0, 0, 0)),
            _const_spec((4, HEAD_DIM)),
            _const_spec((1, V_DIM)),
        ],
        out_specs=pl.BlockSpec((None, tq, V_DIM), lambda b, h, i: (b, i, h)),
        out_shape=jax.ShapeDtypeStruct((bsz, s, w), _F32),
        scratch_shapes=[
            pltpu.VMEM((2 * tq, V_DIM), _BF16),
            col(_F32), col(_F32),
            pltpu.VMEM((V_DIM, 2 * tq), _F32),
            tile(_F32), tile(_F32),
            col(_F32), col(_F32),
            tile(_BF16), tile(_BF16),
            col(_F32), col(_F32),
        ],
        compiler_params=_params(("arbitrary", "arbitrary", "arbitrary")),
        name="diff_attention",
    )(q, k, vt, bias_tiles, lam_params, subln_g.reshape(1, V_DIM))


def _attn_out_kernel(x_ref, o_ref, g_ref, wz_ref, wout_ref, fg_ref, y_ref, *, final_norm):
    x = x_ref[...]
    h = _rmsnorm_rows(x, g_ref[...]).astype(_BF16)
    z = _dot(h, wz_ref[...])
    y = (o_ref[...] * jax.nn.silu(z)).astype(_BF16)
    x_new = x + _dot(y, wout_ref[...])
    if final_norm:
        x_new = _rmsnorm_rows(x_new, fg_ref[...])
    y_ref[...] = x_new


def _attn_out(x2, o2, g, w_z, w_out, final_g, final_norm):
    n, d = x2.shape
    w = w_out.shape[0]
    tm = ROW_TILE
    return pl.pallas_call(
        functools.partial(_attn_out_kernel, final_norm=final_norm),
        grid=(n // tm,),
        in_specs=[
            pl.BlockSpec((tm, d), lambda i: (i, 0)),
            pl.BlockSpec((tm, w), lambda i: (i, 0)),
            _const_spec((1, d)),
            _const_spec((d, w)),
            _const_spec((w, d)),
            _const_spec((1, d)),
        ],
        out_specs=pl.BlockSpec((tm, d), lambda i: (i, 0)),
        out_shape=jax.ShapeDtypeStruct((n, d), _F32),
        compiler_params=_params(("arbitrary",)),
        name="attn_out",
    )(x2, o2, g.reshape(1, d), w_z, w_out, final_g.reshape(1, d))


def _t5_causal_bucket(dist):
    d_safe = jnp.maximum(dist, 1).astype(_F32)
    large = MAX_EXACT + (jnp.log(d_safe / MAX_EXACT) / math.log(REL_MAX_DIST / MAX_EXACT)
                         * (NUM_BUCKETS - MAX_EXACT)).astype(jnp.int32)
    large = jnp.minimum(large, NUM_BUCKETS - 1)
    return jnp.where(dist < MAX_EXACT, dist, large)


def _toeplitz(g_vec, rows, cols):
    period = g_vec.shape[-1]
    flat = jnp.tile(g_vec, (1,) * (g_vec.ndim - 1) + (rows,))
    flat = flat[..., :rows * (period - 1)]
    return flat.reshape(g_vec.shape[:-1] + (rows, period - 1))[..., :cols]


def _bias_tiles(rel_bias, tq, tk):
    period = 2 * (tq + tk)
    delta = jnp.arange(period, dtype=jnp.int32)
    delta = jnp.where(delta >= period // 2, delta - period, delta)
    n_dist = tq + tk
    rel = ((rel_bias[_t5_causal_bucket(jnp.arange(n_dist, dtype=jnp.int32))]
            - rel_bias[NUM_BUCKETS - 1]) * _LOG2E).T

    def vec(offset):
        dist = delta + offset
        vals = rel[:, jnp.clip(dist, 0, n_dist - 1)]
        return jnp.where(dist >= 0, vals, -jnp.inf)

    g_vec = jnp.stack([vec(tk), vec(0), vec(-tk)], axis=1)
    tiles = _toeplitz(g_vec, tk, tq)
    tiles = tiles.reshape(N_DIFF_HEADS, 2, 3, tk, tq).transpose(0, 2, 3, 1, 4)
    return tiles.reshape(N_DIFF_HEADS, 3, tk, 2 * tq)


def _lambda_init_for(layer_idx):
    return 0.8 - 0.6 * math.exp(-0.3 * layer_idx)


def kernel(x, norm_g, w_in, w_out, conv_w, lambda_q1, lambda_k1, lambda_q2, lambda_k2, subln_g,
           rel_bias, final_g):
    bsz, seq, d = x.shape
    depth = norm_g.shape[0]
    w = w_out.shape[1]
    tq, tk = ATTN_Q_TILE, ATTN_K_TILE
    assert seq % tq == 0 and seq % ROW_TILE == 0 and w == N_DIFF_HEADS * V_DIM
    assert depth % N_MIXERS == 0

    bias_tiles = _bias_tiles(rel_bias, tq, tk)
    x2 = x.reshape(bsz * seq, d)
    for i in range(depth):
        j = i // N_MIXERS
        if i % N_MIXERS == 0:
            x2 = _conv_layer(x2, norm_g[i], w_in[i], w_out[i], conv_w[j], seq)
        else:
            w_in_bf = w_in[i].astype(_BF16)
            q, k, v = _qkv_proj(x2, norm_g[i], w_in_bf[:, :3 * w])
            vt = (v.reshape(bsz, seq // tk, tk, N_DIFF_HEADS, V_DIM).transpose(0, 3, 1, 4, 2))
            lam_params = jnp.stack([lambda_q1[j], lambda_k1[j], lambda_q2[j], lambda_k2[j]])
            o = _attention(q.reshape(bsz, seq, w), k.reshape(bsz, seq, w), vt, bias_tiles,
                           lam_params, subln_g[j], _lambda_init_for(i))
            x2 = _attn_out(x2, o.reshape(bsz * seq, w), norm_g[i], w_in_bf[:, 3 * w:],
                           w_out[i].astype(_BF16), final_g, final_norm=(i == depth - 1))
    return x2.reshape(bsz, seq, d)
```

```python
import functools
import math

import jax
import jax.numpy as jnp
from jax import lax
from jax.experimental import pallas as pl
from jax.experimental.pallas import tpu as pltpu

N_MIXERS = 2
CONV_K = 3
N_DIFF_HEADS = 8
HEAD_DIM = 64
V_DIM = 2 * HEAD_DIM
N_MAPS = 2 * N_DIFF_HEADS
NUM_BUCKETS = 32
MAX_EXACT = NUM_BUCKETS // 2
REL_MAX_DIST = 128
RMS_EPS = 1e-6

SUBLANES = 8
BF16_SUBLANES = 16
ROW_TILE = 256
ATTN_Q_TILE = 512
ATTN_K_TILE = 256
VT_ROWS = V_DIM + BF16_SUBLANES
VMEM_LIMIT_BYTES = 56 * 1024 * 1024

assert ATTN_Q_TILE == 2 * ATTN_K_TILE and ATTN_K_TILE >= REL_MAX_DIST
assert ROW_TILE == ATTN_K_TILE

_F32 = jnp.float32
_BF16 = jnp.bfloat16
_LOG2E = math.log2(math.e)


def _dot(a, b):
    return jnp.dot(a, b, preferred_element_type=_F32)


def _rmsnorm_rows(x, g):
    return x * lax.rsqrt(jnp.mean(x * x, axis=-1, keepdims=True) + RMS_EPS) * g


def _const_spec(shape):
    return pl.BlockSpec(shape, lambda *_: (0,) * len(shape), pipeline_mode=pl.Buffered(1))


def _params(semantics, flags=None):
    return pltpu.CompilerParams(dimension_semantics=semantics,
                                vmem_limit_bytes=VMEM_LIMIT_BYTES, flags=flags)


def _shift_rows(v, prev, s):
    r = pltpu.roll(v, s, 0)
    p = pltpu.roll(prev, s, 0)
    row = lax.broadcasted_iota(jnp.int32, prev.shape, 0)
    head = jnp.where(row < s, p, r[:SUBLANES])
    return jnp.concatenate([head, r[SUBLANES:]], axis=0)


def _conv_layer_kernel(x_ref, g_ref, win_ref, wout_ref, cw_ref, o_ref, carry_ref, *,
                       tiles_per_seq, width):
    w = width
    x = x_ref[...]
    h = _rmsnorm_rows(x, g_ref[...]).astype(_BF16)
    b_gate = _dot(h, win_ref[:, 0:w])
    c_gate = _dot(h, win_ref[:, w:2 * w])
    u = _dot(h, win_ref[:, 2 * w:3 * w])
    z = _dot(h, win_ref[:, 3 * w:4 * w])
    v = c_gate * u

    @pl.when(pl.program_id(0) % tiles_per_seq == 0)
    def _():
        carry_ref[...] = jnp.zeros_like(carry_ref)

    prev = carry_ref[...]
    conv = (_shift_rows(v, prev, 2) * cw_ref[0:1, :] + _shift_rows(v, prev, 1) * cw_ref[1:2, :]
            + v * cw_ref[2:3, :])
    carry_ref[...] = v[v.shape[0] - SUBLANES:]
    y = (b_gate * conv * jax.nn.silu(z)).astype(_BF16)
    o_ref[...] = x + _dot(y, wout_ref[...])


def _conv_layer(x2, g, w_in, w_out, conv_w, seq):
    n, d = x2.shape
    w = w_out.shape[0]
    tm = ROW_TILE
    return pl.pallas_call(
        functools.partial(_conv_layer_kernel, tiles_per_seq=seq // tm, width=w),
        grid=(n // tm,),
        in_specs=[
            pl.BlockSpec((tm, d), lambda i: (i, 0)),
            _const_spec((1, d)),
            _const_spec((d, 4 * w)),
            _const_spec((w, d)),
            _const_spec((CONV_K, w)),
        ],
        out_specs=pl.BlockSpec((tm, d), lambda i: (i, 0)),
        out_shape=jax.ShapeDtypeStruct((n, d), _F32),
        scratch_shapes=[pltpu.VMEM((SUBLANES, w), _F32)],
        compiler_params=_params(("arbitrary",)),
        name="conv_layer",
    )(x2, g.reshape(1, d), w_in.astype(_BF16), w_out.astype(_BF16), conv_w.T)


def _qkv_kernel(x_ref, g_ref, w_ref, q_ref, k_ref, vt_ref, *, width):
    w = width
    h = _rmsnorm_rows(x_ref[...], g_ref[...]).astype(_BF16)
    q = (_dot(h, w_ref[:, 0:w]) * (HEAD_DIM ** -0.5 * _LOG2E)).astype(_BF16)
    k = _dot(h, w_ref[:, w:2 * w]).astype(_BF16)
    v = _dot(h, w_ref[:, 2 * w:3 * w])
    n_pad = vt_ref.shape[1] - V_DIM
    row = lax.broadcasted_iota(jnp.int32, (n_pad, v.shape[0]), 0)
    ones_rows = jnp.where(row == 0, 1.0, 0.0).astype(_BF16)
    for hd in range(vt_ref.shape[0]):
        cols = slice(hd * V_DIM, (hd + 1) * V_DIM)
        q_ref[hd] = q[:, cols]
        k_ref[hd] = k[:, cols]
        vt_ref[hd, 0:V_DIM, :] = v[:, cols].T.astype(_BF16)
        vt_ref[hd, V_DIM:, :] = ones_rows


def _qkv_proj(x2, g, w_qkv):
    n, d = x2.shape
    w = w_qkv.shape[1] // 3
    tm = ROW_TILE
    n_heads = w // V_DIM
    out = jax.ShapeDtypeStruct((n // tm, n_heads, tm, V_DIM), _BF16)
    head_tiles = pl.BlockSpec((None, n_heads, tm, V_DIM), lambda i: (i, 0, 0, 0))
    return pl.pallas_call(
        functools.partial(_qkv_kernel, width=w),
        grid=(n // tm,),
        in_specs=[
            pl.BlockSpec((tm, d), lambda i: (i, 0)),
            _const_spec((1, d)),
            _const_spec((d, 3 * w)),
        ],
        out_specs=[head_tiles, head_tiles,
                   pl.BlockSpec((None, n_heads, VT_ROWS, tm), lambda i: (i, 0, 0, 0))],
        out_shape=[out, out, jax.ShapeDtypeStruct((n // tm, n_heads, VT_ROWS, tm), _BF16)],
        compiler_params=_params(("arbitrary",)),
        name="qkv_proj",
    )(x2, g.reshape(1, d), w_qkv)


def _attn_kernel(q_ref, k_ref, vt_ref, bias_ref, lam_ref, sg_ref, o_ref,
                 qq_sc, m_sc, acc_sc, s0, s1, t0, t1, p0, p1, a0, a1, *, lambda_init):
    tk = s0.shape[0]
    tq = q_ref.shape[0] * tk
    qi = pl.program_id(2)
    n_tiles = 2 * qi + 2
    slots = ((s0, t0, p0, a0), (s1, t1, p1, a1))

    q = q_ref[...].reshape(tq, V_DIM)
    lane = lax.broadcasted_iota(jnp.int32, q.shape, 1)
    zero = jnp.zeros_like(q)
    qq_sc[0:tq, :] = jnp.where(lane < HEAD_DIM, q, zero)
    qq_sc[tq:2 * tq, :] = jnp.where(lane >= HEAD_DIM, q, zero)

    m_sc[...] = jnp.full_like(m_sc, -jnp.inf)
    acc_sc[...] = jnp.zeros_like(acc_sc)

    def stage_a(j, slot, bias):
        s_ref, t_ref, _, _ = slots[slot]
        s = lax.dot_general(k_ref[j], qq_sc[...], (((1,), (1,)), ((), ())),
                            preferred_element_type=_F32)
        if bias is not None:
            s = s + bias
        s_ref[...] = s
        t_ref[...] = jnp.max(s, axis=0, keepdims=True)

    def stage_b(slot):
        s_ref, t_ref, p_ref, a_ref = slots[slot]
        m_prev = m_sc[...]
        m_new = jnp.maximum(m_prev, t_ref[...])
        alpha = jnp.exp2(m_prev - m_new)
        p_ref[...] = jnp.exp2(s_ref[...] - m_new).astype(_BF16)
        a_ref[...] = alpha
        m_sc[...] = m_new

    def stage_c(j, slot):
        _, _, p_ref, a_ref = slots[slot]
        acc_sc[...] = acc_sc[...] * a_ref[...] + _dot(vt_ref[j], p_ref[...])

    def step_pair(u, bias_even, bias_odd, issue_a=True):
        i = 2 * u
        stage_c(jnp.maximum(i - 2, 0), 0)
        stage_c(jnp.maximum(i - 1, 0), 1)
        stage_b(0)
        stage_b(1)
        if issue_a:
            stage_a(i + 2, 0, bias_even)
            stage_a(i + 3, 1, bias_odd)

    def sequential(biases):
        for j, bias in enumerate(biases):
            stage_a(j, j % 2, None if bias is None else bias_ref[bias])
            stage_b(j % 2)
            stage_c(j, j % 2)

    prev, diag0, diag1 = 0, 1, 2

    @pl.when(qi == 0)
    def _():
        sequential([diag0, diag1])

    @pl.when(qi == 1)
    def _():
        sequential([None, prev, diag0, diag1])

    @pl.when(qi > 1)
    def _():
        for _, _, p_ref, a_ref in slots:
            p_ref[...] = jnp.zeros_like(p_ref)
            a_ref[...] = jnp.ones_like(a_ref)
        stage_a(0, 0, None)
        stage_a(1, 1, None)

        def plain(u, carry):
            step_pair(u, None, None)
            return carry

        lax.fori_loop(0, qi - 2, plain, 0)
        step_pair(qi - 2, None, bias_ref[prev])
        step_pair(qi - 1, bias_ref[diag0], bias_ref[diag1])
        step_pair(qi, None, None, issue_a=False)
        stage_c(n_tiles - 2, 0)
        stage_c(n_tiles - 1, 1)

    lam = (jnp.exp(jnp.sum(lam_ref[0:1, :] * lam_ref[1:2, :], axis=-1, keepdims=True))
           - jnp.exp(jnp.sum(lam_ref[2:3, :] * lam_ref[3:4, :], axis=-1, keepdims=True))
           + lambda_init)
    o_maps = acc_sc[0:V_DIM, :] / acc_sc[V_DIM:V_DIM + 1, :]
    o = (o_maps[:, :tq] - lam * o_maps[:, tq:]).T
    o_ref[...] = _rmsnorm_rows(o, sg_ref[...]) * (1.0 - lambda_init)


def _attention(q, k, vt, bias_tiles, lam_params, subln_g, lambda_init, bsz):
    tq, tk = ATTN_Q_TILE, ATTN_K_TILE
    n_heads = q.shape[1]
    s = q.shape[0] * tk // bsz
    w = n_heads * V_DIM
    q_tiles = s // tq

    def col(dtype):
        return pltpu.VMEM((1, 2 * tq), dtype)

    def tile(dtype):
        return pltpu.VMEM((tk, 2 * tq), dtype)

    return pl.pallas_call(
        functools.partial(_attn_kernel, lambda_init=lambda_init),
        grid=(bsz, n_heads, s // tq),
        in_specs=[
            pl.BlockSpec((tq // tk, None, tk, V_DIM), lambda b, h, i: (b * q_tiles + i, h, 0, 0)),
            pl.BlockSpec((s // tk, None, tk, V_DIM), lambda b, h, i: (b, h, 0, 0)),
            pl.BlockSpec((s // tk, None, VT_ROWS, tk), lambda b, h, i: (b, h, 0, 0)),
            pl.BlockSpec((None, 3, tk, 2 * tq), lambda b, h, i: (h, 0, 0, 0)),
            _const_spec((4, HEAD_DIM)),
            _const_spec((1, V_DIM)),
        ],
        out_specs=pl.BlockSpec((None, tq, V_DIM), lambda b, h, i: (b, i, h)),
        out_shape=jax.ShapeDtypeStruct((bsz, s, w), _F32),
        scratch_shapes=[
            pltpu.VMEM((2 * tq, V_DIM), _BF16),
            col(_F32),
            pltpu.VMEM((VT_ROWS, 2 * tq), _F32),
            tile(_F32), tile(_F32),
            col(_F32), col(_F32),
            tile(_BF16), tile(_BF16),
            col(_F32), col(_F32),
        ],
        compiler_params=_params(("arbitrary", "arbitrary", "arbitrary")),
        name="diff_attention",
    )(q, k, vt, bias_tiles, lam_params, subln_g.reshape(1, V_DIM))


def _attn_out_kernel(x_ref, o_ref, g_ref, wz_ref, wout_ref, fg_ref, y_ref, *, final_norm):
    x = x_ref[...]
    h = _rmsnorm_rows(x, g_ref[...]).astype(_BF16)
    z = _dot(h, wz_ref[...])
    y = (o_ref[...] * jax.nn.silu(z)).astype(_BF16)
    x_new = x + _dot(y, wout_ref[...])
    if final_norm:
        x_new = _rmsnorm_rows(x_new, fg_ref[...])
    y_ref[...] = x_new


def _attn_out(x2, o2, g, w_z, w_out, final_g, final_norm):
    n, d = x2.shape
    w = w_out.shape[0]
    tm = ROW_TILE
    return pl.pallas_call(
        functools.partial(_attn_out_kernel, final_norm=final_norm),
        grid=(n // tm,),
        in_specs=[
            pl.BlockSpec((tm, d), lambda i: (i, 0)),
            pl.BlockSpec((tm, w), lambda i: (i, 0)),
            _const_spec((1, d)),
            _const_spec((d, w)),
            _const_spec((w, d)),
            _const_spec((1, d)),
        ],
        out_specs=pl.BlockSpec((tm, d), lambda i: (i, 0)),
        out_shape=jax.ShapeDtypeStruct((n, d), _F32),
        compiler_params=_params(("arbitrary",)),
        name="attn_out",
    )(x2, o2, g.reshape(1, d), w_z, w_out, final_g.reshape(1, d))


def _t5_causal_bucket(dist):
    d_safe = jnp.maximum(dist, 1).astype(_F32)
    large = MAX_EXACT + (jnp.log(d_safe / MAX_EXACT) / math.log(REL_MAX_DIST / MAX_EXACT)
                         * (NUM_BUCKETS - MAX_EXACT)).astype(jnp.int32)
    large = jnp.minimum(large, NUM_BUCKETS - 1)
    return jnp.where(dist < MAX_EXACT, dist, large)


def _toeplitz(g_vec, rows, cols):
    period = g_vec.shape[-1]
    flat = jnp.tile(g_vec, (1,) * (g_vec.ndim - 1) + (rows,))
    flat = flat[..., :rows * (period - 1)]
    return flat.reshape(g_vec.shape[:-1] + (rows, period - 1))[..., :cols]


def _bias_tiles(rel_bias, tq, tk):
    period = 2 * (tq + tk)
    delta = jnp.arange(period, dtype=jnp.int32)
    delta = jnp.where(delta >= period // 2, delta - period, delta)
    n_dist = tq + tk
    rel = ((rel_bias[_t5_causal_bucket(jnp.arange(n_dist, dtype=jnp.int32))]
            - rel_bias[NUM_BUCKETS - 1]) * _LOG2E).T

    def vec(offset):
        dist = delta + offset
        vals = rel[:, jnp.clip(dist, 0, n_dist - 1)]
        return jnp.where(dist >= 0, vals, -jnp.inf)

    g_vec = jnp.stack([vec(tk), vec(0), vec(-tk)], axis=1)
    tiles = _toeplitz(g_vec, tk, tq)
    tiles = tiles.reshape(N_DIFF_HEADS, 2, 3, tk, tq).transpose(0, 2, 3, 1, 4)
    return tiles.reshape(N_DIFF_HEADS, 3, tk, 2 * tq)


def _lambda_init_for(layer_idx):
    return 0.8 - 0.6 * math.exp(-0.3 * layer_idx)


def kernel(x, norm_g, w_in, w_out, conv_w, lambda_q1, lambda_k1, lambda_q2, lambda_k2, subln_g,
           rel_bias, final_g):
    bsz, seq, d = x.shape
    depth = norm_g.shape[0]
    w = w_out.shape[1]
    tq, tk = ATTN_Q_TILE, ATTN_K_TILE
    assert seq % tq == 0 and seq % ROW_TILE == 0 and w == N_DIFF_HEADS * V_DIM
    assert depth % N_MIXERS == 0

    bias_tiles = _bias_tiles(rel_bias, tq, tk)
    x2 = x.reshape(bsz * seq, d)
    for i in range(depth):
        j = i // N_MIXERS
        if i % N_MIXERS == 0:
            x2 = _conv_layer(x2, norm_g[i], w_in[i], w_out[i], conv_w[j], seq)
        else:
            w_in_bf = w_in[i].astype(_BF16)
            q, k, vt = _qkv_proj(x2, norm_g[i], w_in_bf[:, :3 * w])
            lam_params = jnp.stack([lambda_q1[j], lambda_k1[j], lambda_q2[j], lambda_k2[j]])
            o = _attention(q, k, vt, bias_tiles, lam_params, subln_g[j], _lambda_init_for(i),
                           bsz)
            x2 = _attn_out(x2, o.reshape(bsz * seq, w), norm_g[i], w_in_bf[:, 3 * w:],
                           w_out[i].astype(_BF16), final_g, final_norm=(i == depth - 1))
    return x2.reshape(bsz, seq, d)
```

```python
import functools
import math

import jax
import jax.numpy as jnp
from jax import lax
from jax.experimental import pallas as pl
from jax.experimental.pallas import tpu as pltpu

N_MIXERS = 2
CONV_K = 3
N_DIFF_HEADS = 8
HEAD_DIM = 64
V_DIM = 2 * HEAD_DIM
N_MAPS = 2 * N_DIFF_HEADS
NUM_BUCKETS = 32
MAX_EXACT = NUM_BUCKETS // 2
REL_MAX_DIST = 128
RMS_EPS = 1e-6

SUBLANES = 8
BF16_SUBLANES = 16
ROW_TILE = 256
ATTN_Q_TILE = 512
ATTN_K_TILE = 256
VT_ROWS = V_DIM + BF16_SUBLANES
VMEM_LIMIT_BYTES = 56 * 1024 * 1024

assert ATTN_Q_TILE == 2 * ATTN_K_TILE and ATTN_K_TILE >= REL_MAX_DIST
assert ROW_TILE == ATTN_K_TILE

_F32 = jnp.float32
_BF16 = jnp.bfloat16
_LOG2E = math.log2(math.e)


def _dot(a, b):
    return jnp.dot(a, b, preferred_element_type=_F32)


def _rmsnorm_rows(x, g):
    return x * lax.rsqrt(jnp.mean(x * x, axis=-1, keepdims=True) + RMS_EPS) * g


def _const_spec(shape):
    return pl.BlockSpec(shape, lambda *_: (0,) * len(shape), pipeline_mode=pl.Buffered(1))


def _params(semantics, flags=None):
    return pltpu.CompilerParams(dimension_semantics=semantics,
                                vmem_limit_bytes=VMEM_LIMIT_BYTES, flags=flags)


def _shift_rows(v, prev, s):
    r = pltpu.roll(v, s, 0)
    p = pltpu.roll(prev, s, 0)
    row = lax.broadcasted_iota(jnp.int32, prev.shape, 0)
    head = jnp.where(row < s, p, r[:SUBLANES])
    return jnp.concatenate([head, r[SUBLANES:]], axis=0)


def _conv_layer_kernel(x_ref, g_ref, win_ref, wout_ref, cw_ref, o_ref, carry_ref, *,
                       tiles_per_seq, width):
    w = width
    x = x_ref[...]
    h = _rmsnorm_rows(x, g_ref[...]).astype(_BF16)
    b_gate = _dot(h, win_ref[:, 0:w])
    c_gate = _dot(h, win_ref[:, w:2 * w])
    u = _dot(h, win_ref[:, 2 * w:3 * w])
    z = _dot(h, win_ref[:, 3 * w:4 * w])
    v = c_gate * u

    @pl.when(pl.program_id(0) % tiles_per_seq == 0)
    def _():
        carry_ref[...] = jnp.zeros_like(carry_ref)

    prev = carry_ref[...]
    conv = (_shift_rows(v, prev, 2) * cw_ref[0:1, :] + _shift_rows(v, prev, 1) * cw_ref[1:2, :]
            + v * cw_ref[2:3, :])
    carry_ref[...] = v[v.shape[0] - SUBLANES:]
    y = (b_gate * conv * jax.nn.silu(z)).astype(_BF16)
    o_ref[...] = x + _dot(y, wout_ref[...])


def _conv_layer(x2, g, w_in, w_out, conv_w, seq):
    n, d = x2.shape
    w = w_out.shape[0]
    tm = ROW_TILE
    return pl.pallas_call(
        functools.partial(_conv_layer_kernel, tiles_per_seq=seq // tm, width=w),
        grid=(n // tm,),
        in_specs=[
            pl.BlockSpec((tm, d), lambda i: (i, 0)),
            _const_spec((1, d)),
            _const_spec((d, 4 * w)),
            _const_spec((w, d)),
            _const_spec((CONV_K, w)),
        ],
        out_specs=pl.BlockSpec((tm, d), lambda i: (i, 0)),
        out_shape=jax.ShapeDtypeStruct((n, d), _F32),
        scratch_shapes=[pltpu.VMEM((SUBLANES, w), _F32)],
        compiler_params=_params(("arbitrary",)),
        name="conv_layer",
    )(x2, g.reshape(1, d), w_in.astype(_BF16), w_out.astype(_BF16), conv_w.T)


def _qkv_kernel(x_ref, g_ref, w_ref, q_ref, k_ref, vt_ref, *, width):
    w = width
    h = _rmsnorm_rows(x_ref[...], g_ref[...]).astype(_BF16)
    q = (_dot(h, w_ref[:, 0:w]) * (HEAD_DIM ** -0.5 * _LOG2E)).astype(_BF16)
    k = _dot(h, w_ref[:, w:2 * w]).astype(_BF16)
    v = _dot(h, w_ref[:, 2 * w:3 * w])
    n_pad = vt_ref.shape[1] - V_DIM
    row = lax.broadcasted_iota(jnp.int32, (n_pad, v.shape[0]), 0)
    ones_rows = jnp.where(row == 0, 1.0, 0.0).astype(_BF16)
    for hd in range(vt_ref.shape[0]):
        cols = slice(hd * V_DIM, (hd + 1) * V_DIM)
        q_ref[hd] = q[:, cols]
        k_ref[hd] = k[:, cols]
        vt_ref[hd, 0:V_DIM, :] = v[:, cols].T.astype(_BF16)
        vt_ref[hd, V_DIM:, :] = ones_rows


def _qkv_proj(x2, g, w_qkv):
    n, d = x2.shape
    w = w_qkv.shape[1] // 3
    tm = ROW_TILE
    n_heads = w // V_DIM
    out = jax.ShapeDtypeStruct((n // tm, n_heads, tm, V_DIM), _BF16)
    head_tiles = pl.BlockSpec((None, n_heads, tm, V_DIM), lambda i: (i, 0, 0, 0))
    return pl.pallas_call(
        functools.partial(_qkv_kernel, width=w),
        grid=(n // tm,),
        in_specs=[
            pl.BlockSpec((tm, d), lambda i: (i, 0)),
            _const_spec((1, d)),
            _const_spec((d, 3 * w)),
        ],
        out_specs=[head_tiles, head_tiles,
                   pl.BlockSpec((None, n_heads, VT_ROWS, tm), lambda i: (i, 0, 0, 0))],
        out_shape=[out, out, jax.ShapeDtypeStruct((n // tm, n_heads, VT_ROWS, tm), _BF16)],
        compiler_params=_params(("arbitrary",)),
        name="qkv_proj",
    )(x2, g.reshape(1, d), w_qkv)


def _attn_kernel(q_ref, k_ref, vt_ref, bias_ref, lam_ref, sg_ref, o_ref,
                 qq_sc, m_sc, acc_sc, s0, s1, t0, t1, p0, p1, a0, a1, *, lambda_init):
    tk = s0.shape[0]
    tq = q_ref.shape[0] * tk
    qi = pl.program_id(2)
    n_tiles = 2 * qi + 2
    slots = ((s0, t0, p0, a0), (s1, t1, p1, a1))

    q = q_ref[...].reshape(tq, V_DIM)
    lane = lax.broadcasted_iota(jnp.int32, q.shape, 1)
    zero = jnp.zeros_like(q)
    qq_sc[0:tq, :] = jnp.where(lane < HEAD_DIM, q, zero)
    qq_sc[tq:2 * tq, :] = jnp.where(lane >= HEAD_DIM, q, zero)

    m_sc[...] = jnp.full_like(m_sc, -jnp.inf)
    acc_sc[...] = jnp.zeros_like(acc_sc)

    def stage_a(j, slot, bias):
        s_ref, t_ref, _, _ = slots[slot]
        s = lax.dot_general(k_ref[j], qq_sc[...], (((1,), (1,)), ((), ())),
                            preferred_element_type=_F32)
        if bias is not None:
            s = s + bias
        s_ref[...] = s
        t_ref[...] = jnp.max(s, axis=0, keepdims=True)

    def stage_a_pair(j, biases):
        kk = k_ref[pl.ds(j, 2)].reshape(2 * tk, V_DIM)
        s = lax.dot_general(kk, qq_sc[...], (((1,), (1,)), ((), ())),
                            preferred_element_type=_F32)
        for slot, bias in enumerate(biases):
            s_ref, t_ref, _, _ = slots[slot]
            tile = s[slot * tk:(slot + 1) * tk]
            if bias is not None:
                tile = tile + bias
            s_ref[...] = tile
            t_ref[...] = jnp.max(tile, axis=0, keepdims=True)

    def stage_b(slot):
        s_ref, t_ref, p_ref, a_ref = slots[slot]
        m_prev = m_sc[...]
        m_new = jnp.maximum(m_prev, t_ref[...])
        alpha = jnp.exp2(m_prev - m_new)
        p_ref[...] = jnp.exp2(s_ref[...] - m_new).astype(_BF16)
        a_ref[...] = alpha
        m_sc[...] = m_new

    def stage_c(j, slot):
        _, _, p_ref, a_ref = slots[slot]
        acc_sc[...] = acc_sc[...] * a_ref[...] + _dot(vt_ref[j], p_ref[...])

    def step_pair(u, bias_even, bias_odd, issue_a=True):
        i = 2 * u
        stage_c(jnp.maximum(i - 2, 0), 0)
        stage_c(jnp.maximum(i - 1, 0), 1)
        stage_b(0)
        stage_b(1)
        if issue_a:
            stage_a_pair(i + 2, (bias_even, bias_odd))

    def sequential(biases):
        for j, bias in enumerate(biases):
            stage_a(j, j % 2, None if bias is None else bias_ref[bias])
            stage_b(j % 2)
            stage_c(j, j % 2)

    prev, diag0, diag1 = 0, 1, 2

    @pl.when(qi == 0)
    def _():
        sequential([diag0, diag1])

    @pl.when(qi == 1)
    def _():
        sequential([None, prev, diag0, diag1])

    @pl.when(qi > 1)
    def _():
        for _, _, p_ref, a_ref in slots:
            p_ref[...] = jnp.zeros_like(p_ref)
            a_ref[...] = jnp.ones_like(a_ref)
        stage_a_pair(0, (None, None))

        def plain(u, carry):
            step_pair(u, None, None)
            return carry

        lax.fori_loop(0, qi - 2, plain, 0)
        step_pair(qi - 2, None, bias_ref[prev])
        step_pair(qi - 1, bias_ref[diag0], bias_ref[diag1])
        step_pair(qi, None, None, issue_a=False)
        stage_c(n_tiles - 2, 0)
        stage_c(n_tiles - 1, 1)

    lam = (jnp.exp(jnp.sum(lam_ref[0:1, :] * lam_ref[1:2, :], axis=-1, keepdims=True))
           - jnp.exp(jnp.sum(lam_ref[2:3, :] * lam_ref[3:4, :], axis=-1, keepdims=True))
           + lambda_init)
    o_maps = acc_sc[0:V_DIM, :] / acc_sc[V_DIM:V_DIM + 1, :]
    o = (o_maps[:, :tq] - lam * o_maps[:, tq:]).T
    o_ref[...] = _rmsnorm_rows(o, sg_ref[...]) * (1.0 - lambda_init)


def _attention(q, k, vt, bias_tiles, lam_params, subln_g, lambda_init, bsz):
    tq, tk = ATTN_Q_TILE, ATTN_K_TILE
    n_heads = q.shape[1]
    s = q.shape[0] * tk // bsz
    w = n_heads * V_DIM
    q_tiles = s // tq

    def col(dtype):
        return pltpu.VMEM((1, 2 * tq), dtype)

    def tile(dtype):
        return pltpu.VMEM((tk, 2 * tq), dtype)

    return pl.pallas_call(
        functools.partial(_attn_kernel, lambda_init=lambda_init),
        grid=(bsz, n_heads, s // tq),
        in_specs=[
            pl.BlockSpec((tq // tk, None, tk, V_DIM), lambda b, h, i: (b * q_tiles + i, h, 0, 0)),
            pl.BlockSpec((s // tk, None, tk, V_DIM), lambda b, h, i: (b, h, 0, 0)),
            pl.BlockSpec((s // tk, None, VT_ROWS, tk), lambda b, h, i: (b, h, 0, 0)),
            pl.BlockSpec((None, 3, tk, 2 * tq), lambda b, h, i: (h, 0, 0, 0)),
            _const_spec((4, HEAD_DIM)),
            _const_spec((1, V_DIM)),
        ],
        out_specs=pl.BlockSpec((None, tq, V_DIM), lambda b, h, i: (b, i, h)),
        out_shape=jax.ShapeDtypeStruct((bsz, s, w), _F32),
        scratch_shapes=[
            pltpu.VMEM((2 * tq, V_DIM), _BF16),
            col(_F32),
            pltpu.VMEM((VT_ROWS, 2 * tq), _F32),
            tile(_F32), tile(_F32),
            col(_F32), col(_F32),
            tile(_BF16), tile(_BF16),
            col(_F32), col(_F32),
        ],
        compiler_params=_params(("arbitrary", "arbitrary", "arbitrary")),
        name="diff_attention",
    )(q, k, vt, bias_tiles, lam_params, subln_g.reshape(1, V_DIM))


def _attn_out_kernel(x_ref, o_ref, g_ref, wz_ref, wout_ref, fg_ref, y_ref, *, final_norm):
    x = x_ref[...]
    h = _rmsnorm_rows(x, g_ref[...]).astype(_BF16)
    z = _dot(h, wz_ref[...])
    y = (o_ref[...] * jax.nn.silu(z)).astype(_BF16)
    x_new = x + _dot(y, wout_ref[...])
    if final_norm:
        x_new = _rmsnorm_rows(x_new, fg_ref[...])
    y_ref[...] = x_new


def _attn_out(x2, o2, g, w_z, w_out, final_g, final_norm):
    n, d = x2.shape
    w = w_out.shape[0]
    tm = ROW_TILE
    return pl.pallas_call(
        functools.partial(_attn_out_kernel, final_norm=final_norm),
        grid=(n // tm,),
        in_specs=[
            pl.BlockSpec((tm, d), lambda i: (i, 0)),
            pl.BlockSpec((tm, w), lambda i: (i, 0)),
            _const_spec((1, d)),
            _const_spec((d, w)),
            _const_spec((w, d)),
            _const_spec((1, d)),
        ],
        out_specs=pl.BlockSpec((tm, d), lambda i: (i, 0)),
        out_shape=jax.ShapeDtypeStruct((n, d), _F32),
        compiler_params=_params(("arbitrary",)),
        name="attn_out",
    )(x2, o2, g.reshape(1, d), w_z, w_out, final_g.reshape(1, d))


def _t5_causal_bucket(dist):
    d_safe = jnp.maximum(dist, 1).astype(_F32)
    large = MAX_EXACT + (jnp.log(d_safe / MAX_EXACT) / math.log(REL_MAX_DIST / MAX_EXACT)
                         * (NUM_BUCKETS - MAX_EXACT)).astype(jnp.int32)
    large = jnp.minimum(large, NUM_BUCKETS - 1)
    return jnp.where(dist < MAX_EXACT, dist, large)


def _toeplitz(g_vec, rows, cols):
    period = g_vec.shape[-1]
    flat = jnp.tile(g_vec, (1,) * (g_vec.ndim - 1) + (rows,))
    flat = flat[..., :rows * (period - 1)]
    return flat.reshape(g_vec.shape[:-1] + (rows, period - 1))[..., :cols]


def _bias_tiles(rel_bias, tq, tk):
    period = 2 * (tq + tk)
    delta = jnp.arange(period, dtype=jnp.int32)
    delta = jnp.where(delta >= period // 2, delta - period, delta)
    n_dist = tq + tk
    rel = ((rel_bias[_t5_causal_bucket(jnp.arange(n_dist, dtype=jnp.int32))]
            - rel_bias[NUM_BUCKETS - 1]) * _LOG2E).T

    def vec(offset):
        dist = delta + offset
        vals = rel[:, jnp.clip(dist, 0, n_dist - 1)]
        return jnp.where(dist >= 0, vals, -jnp.inf)

    g_vec = jnp.stack([vec(tk), vec(0), vec(-tk)], axis=1)
    tiles = _toeplitz(g_vec, tk, tq)
    tiles = tiles.reshape(N_DIFF_HEADS, 2, 3, tk, tq).transpose(0, 2, 3, 1, 4)
    return tiles.reshape(N_DIFF_HEADS, 3, tk, 2 * tq)


def _lambda_init_for(layer_idx):
    return 0.8 - 0.6 * math.exp(-0.3 * layer_idx)


def kernel(x, norm_g, w_in, w_out, conv_w, lambda_q1, lambda_k1, lambda_q2, lambda_k2, subln_g,
           rel_bias, final_g):
    bsz, seq, d = x.shape
    depth = norm_g.shape[0]
    w = w_out.shape[1]
    tq, tk = ATTN_Q_TILE, ATTN_K_TILE
    assert seq % tq == 0 and seq % ROW_TILE == 0 and w == N_DIFF_HEADS * V_DIM
    assert depth % N_MIXERS == 0

    bias_tiles = _bias_tiles(rel_bias, tq, tk)
    x2 = x.reshape(bsz * seq, d)
    for i in range(depth):
        j = i // N_MIXERS
        if i % N_MIXERS == 0:
            x2 = _conv_layer(x2, norm_g[i], w_in[i], w_out[i], conv_w[j], seq)
        else:
            w_in_bf = w_in[i].astype(_BF16)
            q, k, vt = _qkv_proj(x2, norm_g[i], w_in_bf[:, :3 * w])
            lam_params = jnp.stack([lambda_q1[j], lambda_k1[j], lambda_q2[j], lambda_k2[j]])
            o = _attention(q, k, vt, bias_tiles, lam_params, subln_g[j], _lambda_init_for(i),
                           bsz)
            x2 = _attn_out(x2, o.reshape(bsz * seq, w), norm_g[i], w_in_bf[:, 3 * w:],
                           w_out[i].astype(_BF16), final_g, final_norm=(i == depth - 1))
    return x2.reshape(bsz, seq, d)
```

```python
import functools
import math

import jax
import jax.numpy as jnp
from jax import lax
from jax.experimental import pallas as pl
from jax.experimental.pallas import tpu as pltpu

N_MIXERS = 2
CONV_K = 3
N_DIFF_HEADS = 8
HEAD_DIM = 64
V_DIM = 2 * HEAD_DIM
N_MAPS = 2 * N_DIFF_HEADS
NUM_BUCKETS = 32
MAX_EXACT = NUM_BUCKETS // 2
REL_MAX_DIST = 128
RMS_EPS = 1e-6

SUBLANES = 8
BF16_SUBLANES = 16
ROW_TILE = 256
ATTN_Q_TILE = 512
ATTN_K_TILE = 256
VT_ROWS = V_DIM + BF16_SUBLANES
VMEM_LIMIT_BYTES = 56 * 1024 * 1024

assert ATTN_Q_TILE == 2 * ATTN_K_TILE and ATTN_K_TILE >= REL_MAX_DIST
assert ROW_TILE == ATTN_K_TILE

_F32 = jnp.float32
_BF16 = jnp.bfloat16
_LOG2E = math.log2(math.e)


def _dot(a, b):
    return jnp.dot(a, b, preferred_element_type=_F32)


def _rmsnorm_rows(x, g):
    return x * lax.rsqrt(jnp.mean(x * x, axis=-1, keepdims=True) + RMS_EPS) * g


def _const_spec(shape):
    return pl.BlockSpec(shape, lambda *_: (0,) * len(shape), pipeline_mode=pl.Buffered(1))


def _params(semantics, flags=None):
    return pltpu.CompilerParams(dimension_semantics=semantics,
                                vmem_limit_bytes=VMEM_LIMIT_BYTES, flags=flags)


def _shift_rows(v, prev, s):
    r = pltpu.roll(v, s, 0)
    p = pltpu.roll(prev, s, 0)
    row = lax.broadcasted_iota(jnp.int32, prev.shape, 0)
    head = jnp.where(row < s, p, r[:SUBLANES])
    return jnp.concatenate([head, r[SUBLANES:]], axis=0)


def _conv_layer_kernel(x_ref, g_ref, win_ref, wout_ref, cw_ref, o_ref, carry_ref, *,
                       tiles_per_seq, width):
    w = width
    x = x_ref[...]
    h = _rmsnorm_rows(x, g_ref[...]).astype(_BF16)
    b_gate = _dot(h, win_ref[:, 0:w])
    c_gate = _dot(h, win_ref[:, w:2 * w])
    u = _dot(h, win_ref[:, 2 * w:3 * w])
    z = _dot(h, win_ref[:, 3 * w:4 * w])
    v = c_gate * u

    @pl.when(pl.program_id(0) % tiles_per_seq == 0)
    def _():
        carry_ref[...] = jnp.zeros_like(carry_ref)

    prev = carry_ref[...]
    conv = (_shift_rows(v, prev, 2) * cw_ref[0:1, :] + _shift_rows(v, prev, 1) * cw_ref[1:2, :]
            + v * cw_ref[2:3, :])
    carry_ref[...] = v[v.shape[0] - SUBLANES:]
    y = (b_gate * conv * jax.nn.silu(z)).astype(_BF16)
    o_ref[...] = x + _dot(y, wout_ref[...])


def _conv_layer(x2, g, w_in, w_out, conv_w, seq):
    n, d = x2.shape
    w = w_out.shape[0]
    tm = ROW_TILE
    return pl.pallas_call(
        functools.partial(_conv_layer_kernel, tiles_per_seq=seq // tm, width=w),
        grid=(n // tm,),
        in_specs=[
            pl.BlockSpec((tm, d), lambda i: (i, 0)),
            _const_spec((1, d)),
            _const_spec((d, 4 * w)),
            _const_spec((w, d)),
            _const_spec((CONV_K, w)),
        ],
        out_specs=pl.BlockSpec((tm, d), lambda i: (i, 0)),
        out_shape=jax.ShapeDtypeStruct((n, d), _F32),
        scratch_shapes=[pltpu.VMEM((SUBLANES, w), _F32)],
        compiler_params=_params(("arbitrary",)),
        name="conv_layer",
    )(x2, g.reshape(1, d), w_in.astype(_BF16), w_out.astype(_BF16), conv_w.T)


def _qkv_kernel(x_ref, g_ref, w_ref, q_ref, k_ref, vt_ref, *, width):
    w = width
    h = _rmsnorm_rows(x_ref[...], g_ref[...]).astype(_BF16)
    q = (_dot(h, w_ref[:, 0:w]) * (HEAD_DIM ** -0.5 * _LOG2E)).astype(_BF16)
    k = _dot(h, w_ref[:, w:2 * w]).astype(_BF16)
    v = _dot(h, w_ref[:, 2 * w:3 * w])
    n_pad = vt_ref.shape[1] - V_DIM
    row = lax.broadcasted_iota(jnp.int32, (n_pad, v.shape[0]), 0)
    ones_rows = jnp.where(row == 0, 1.0, 0.0).astype(_BF16)
    for hd in range(vt_ref.shape[0]):
        cols = slice(hd * V_DIM, (hd + 1) * V_DIM)
        q_ref[hd] = q[:, cols]
        k_ref[hd] = k[:, cols]
        vt_ref[hd, 0:V_DIM, :] = v[:, cols].T.astype(_BF16)
        vt_ref[hd, V_DIM:, :] = ones_rows


def _qkv_proj(x2, g, w_qkv):
    n, d = x2.shape
    w = w_qkv.shape[1] // 3
    tm = ROW_TILE
    n_heads = w // V_DIM
    out = jax.ShapeDtypeStruct((n // tm, n_heads, tm, V_DIM), _BF16)
    head_tiles = pl.BlockSpec((None, n_heads, tm, V_DIM), lambda i: (i, 0, 0, 0))
    return pl.pallas_call(
        functools.partial(_qkv_kernel, width=w),
        grid=(n // tm,),
        in_specs=[
            pl.BlockSpec((tm, d), lambda i: (i, 0)),
            _const_spec((1, d)),
            _const_spec((d, 3 * w)),
        ],
        out_specs=[head_tiles, head_tiles,
                   pl.BlockSpec((None, n_heads, VT_ROWS, tm), lambda i: (i, 0, 0, 0))],
        out_shape=[out, out, jax.ShapeDtypeStruct((n // tm, n_heads, VT_ROWS, tm), _BF16)],
        compiler_params=_params(("arbitrary",)),
        name="qkv_proj",
    )(x2, g.reshape(1, d), w_qkv)


def _attn_kernel(q_ref, k_ref, vt_ref, bias_ref, lam_ref, sg_ref, o_ref,
                 qq_sc, m_sc, acc_sc, s0, s1, t0, t1, p0, p1, a0, a1, *, lambda_init):
    tk = s0.shape[0]
    tq = q_ref.shape[0] * tk
    qi = pl.program_id(2)
    n_tiles = 2 * qi + 2
    slots = ((s0, t0, p0, a0), (s1, t1, p1, a1))

    q = q_ref[...].reshape(tq, V_DIM)
    lane = lax.broadcasted_iota(jnp.int32, q.shape, 1)
    zero = jnp.zeros_like(q)
    qq_sc[0:tq, :] = jnp.where(lane < HEAD_DIM, q, zero)
    qq_sc[tq:2 * tq, :] = jnp.where(lane >= HEAD_DIM, q, zero)

    m_sc[...] = jnp.full_like(m_sc, -jnp.inf)
    acc_sc[...] = jnp.zeros_like(acc_sc)

    def stage_a(j, slot, bias):
        s_ref, t_ref, _, _ = slots[slot]
        s = lax.dot_general(k_ref[j], qq_sc[...], (((1,), (1,)), ((), ())),
                            preferred_element_type=_F32)
        if bias is not None:
            s = s + bias
        s_ref[...] = s
        t_ref[...] = jnp.max(s, axis=0, keepdims=True)

    def stage_a_pair(j, biases):
        kk = k_ref[pl.ds(j, 2)].reshape(2 * tk, V_DIM)
        s = lax.dot_general(kk, qq_sc[...], (((1,), (1,)), ((), ())),
                            preferred_element_type=_F32)
        for slot, bias in enumerate(biases):
            s_ref, t_ref, _, _ = slots[slot]
            tile = s[slot * tk:(slot + 1) * tk]
            if bias is not None:
                tile = tile + bias
            s_ref[...] = tile
            t_ref[...] = jnp.max(tile, axis=0, keepdims=True)

    def stage_b(slot):
        s_ref, t_ref, p_ref, a_ref = slots[slot]
        m_prev = m_sc[...]
        m_new = jnp.maximum(m_prev, t_ref[...])
        alpha = jnp.exp2(m_prev - m_new)
        p_ref[...] = jnp.exp2(s_ref[...] - m_new).astype(_BF16)
        a_ref[...] = alpha
        m_sc[...] = m_new

    def stage_c(j, slot):
        _, _, p_ref, a_ref = slots[slot]
        acc_sc[...] = acc_sc[...] * a_ref[...] + _dot(vt_ref[j], p_ref[...])

    def step_pair(u, bias_even, bias_odd, issue_a=True):
        i = 2 * u
        stage_c(jnp.maximum(i - 2, 0), 0)
        stage_c(jnp.maximum(i - 1, 0), 1)
        stage_b(0)
        stage_b(1)
        if issue_a:
            stage_a_pair(i + 2, (bias_even, bias_odd))

    def sequential(biases):
        for j, bias in enumerate(biases):
            stage_a(j, j % 2, None if bias is None else bias_ref[bias])
            stage_b(j % 2)
            stage_c(j, j % 2)

    prev, diag0, diag1 = 0, 1, 2

    @pl.when(qi == 0)
    def _():
        sequential([diag0, diag1])

    @pl.when(qi == 1)
    def _():
        sequential([None, prev, diag0, diag1])

    @pl.when(qi > 1)
    def _():
        for _, _, p_ref, a_ref in slots:
            p_ref[...] = jnp.zeros_like(p_ref)
            a_ref[...] = jnp.ones_like(a_ref)
        stage_a_pair(0, (None, None))

        def plain(u, carry):
            step_pair(u, None, None)
            return carry

        def plain_twice(v, carry):
            step_pair(2 * v, None, None)
            step_pair(2 * v + 1, None, None)
            return carry

        n_double = (qi - 2) // 2
        lax.fori_loop(0, n_double, plain_twice, 0)
        lax.fori_loop(2 * n_double, qi - 2, plain, 0)
        step_pair(qi - 2, None, bias_ref[prev])
        step_pair(qi - 1, bias_ref[diag0], bias_ref[diag1])
        step_pair(qi, None, None, issue_a=False)
        stage_c(n_tiles - 2, 0)
        stage_c(n_tiles - 1, 1)

    lam = (jnp.exp(jnp.sum(lam_ref[0:1, :] * lam_ref[1:2, :], axis=-1, keepdims=True))
           - jnp.exp(jnp.sum(lam_ref[2:3, :] * lam_ref[3:4, :], axis=-1, keepdims=True))
           + lambda_init)
    o_maps = acc_sc[0:V_DIM, :] / acc_sc[V_DIM:V_DIM + 1, :]
    o = (o_maps[:, :tq] - lam * o_maps[:, tq:]).T
    o_ref[...] = _rmsnorm_rows(o, sg_ref[...]) * (1.0 - lambda_init)


def _attention(q, k, vt, bias_tiles, lam_params, subln_g, lambda_init, bsz):
    tq, tk = ATTN_Q_TILE, ATTN_K_TILE
    n_heads = q.shape[1]
    s = q.shape[0] * tk // bsz
    w = n_heads * V_DIM
    q_tiles = s // tq

    def col(dtype):
        return pltpu.VMEM((1, 2 * tq), dtype)

    def tile(dtype):
        return pltpu.VMEM((tk, 2 * tq), dtype)

    return pl.pallas_call(
        functools.partial(_attn_kernel, lambda_init=lambda_init),
        grid=(bsz, n_heads, s // tq),
        in_specs=[
            pl.BlockSpec((tq // tk, None, tk, V_DIM), lambda b, h, i: (b * q_tiles + i, h, 0, 0)),
            pl.BlockSpec((s // tk, None, tk, V_DIM), lambda b, h, i: (b, h, 0, 0)),
            pl.BlockSpec((s // tk, None, VT_ROWS, tk), lambda b, h, i: (b, h, 0, 0)),
            pl.BlockSpec((None, 3, tk, 2 * tq), lambda b, h, i: (h, 0, 0, 0)),
            _const_spec((4, HEAD_DIM)),
            _const_spec((1, V_DIM)),
        ],
        out_specs=pl.BlockSpec((None, tq, V_DIM), lambda b, h, i: (b, i, h)),
        out_shape=jax.ShapeDtypeStruct((bsz, s, w), _F32),
        scratch_shapes=[
            pltpu.VMEM((2 * tq, V_DIM), _BF16),
            col(_F32),
            pltpu.VMEM((VT_ROWS, 2 * tq), _F32),
            tile(_F32), tile(_F32),
            col(_F32), col(_F32),
            tile(_BF16), tile(_BF16),
            col(_F32), col(_F32),
        ],
        compiler_params=_params(("arbitrary", "arbitrary", "arbitrary")),
        name="diff_attention",
    )(q, k, vt, bias_tiles, lam_params, subln_g.reshape(1, V_DIM))


def _attn_out_kernel(x_ref, o_ref, g_ref, wz_ref, wout_ref, fg_ref, y_ref, *, final_norm):
    x = x_ref[...]
    h = _rmsnorm_rows(x, g_ref[...]).astype(_BF16)
    z = _dot(h, wz_ref[...])
    y = (o_ref[...] * jax.nn.silu(z)).astype(_BF16)
    x_new = x + _dot(y, wout_ref[...])
    if final_norm:
        x_new = _rmsnorm_rows(x_new, fg_ref[...])
    y_ref[...] = x_new


def _attn_out(x2, o2, g, w_z, w_out, final_g, final_norm):
    n, d = x2.shape
    w = w_out.shape[0]
    tm = ROW_TILE
    return pl.pallas_call(
        functools.partial(_attn_out_kernel, final_norm=final_norm),
        grid=(n // tm,),
        in_specs=[
            pl.BlockSpec((tm, d), lambda i: (i, 0)),
            pl.BlockSpec((tm, w), lambda i: (i, 0)),
            _const_spec((1, d)),
            _const_spec((d, w)),
            _const_spec((w, d)),
            _const_spec((1, d)),
        ],
        out_specs=pl.BlockSpec((tm, d), lambda i: (i, 0)),
        out_shape=jax.ShapeDtypeStruct((n, d), _F32),
        compiler_params=_params(("arbitrary",)),
        name="attn_out",
    )(x2, o2, g.reshape(1, d), w_z, w_out, final_g.reshape(1, d))


def _t5_causal_bucket(dist):
    d_safe = jnp.maximum(dist, 1).astype(_F32)
    large = MAX_EXACT + (jnp.log(d_safe / MAX_EXACT) / math.log(REL_MAX_DIST / MAX_EXACT)
                         * (NUM_BUCKETS - MAX_EXACT)).astype(jnp.int32)
    large = jnp.minimum(large, NUM_BUCKETS - 1)
    return jnp.where(dist < MAX_EXACT, dist, large)


def _toeplitz(g_vec, rows, cols):
    period = g_vec.shape[-1]
    flat = jnp.tile(g_vec, (1,) * (g_vec.ndim - 1) + (rows,))
    flat = flat[..., :rows * (period - 1)]
    return flat.reshape(g_vec.shape[:-1] + (rows, period - 1))[..., :cols]


def _bias_tiles(rel_bias, tq, tk):
    period = 2 * (tq + tk)
    delta = jnp.arange(period, dtype=jnp.int32)
    delta = jnp.where(delta >= period // 2, delta - period, delta)
    n_dist = tq + tk
    rel = ((rel_bias[_t5_causal_bucket(jnp.arange(n_dist, dtype=jnp.int32))]
            - rel_bias[NUM_BUCKETS - 1]) * _LOG2E).T

    def vec(offset):
        dist = delta + offset
        vals = rel[:, jnp.clip(dist, 0, n_dist - 1)]
        return jnp.where(dist >= 0, vals, -jnp.inf)

    g_vec = jnp.stack([vec(tk), vec(0), vec(-tk)], axis=1)
    tiles = _toeplitz(g_vec, tk, tq)
    tiles = tiles.reshape(N_DIFF_HEADS, 2, 3, tk, tq).transpose(0, 2, 3, 1, 4)
    return tiles.reshape(N_DIFF_HEADS, 3, tk, 2 * tq)


def _lambda_init_for(layer_idx):
    return 0.8 - 0.6 * math.exp(-0.3 * layer_idx)


def kernel(x, norm_g, w_in, w_out, conv_w, lambda_q1, lambda_k1, lambda_q2, lambda_k2, subln_g,
           rel_bias, final_g):
    bsz, seq, d = x.shape
    depth = norm_g.shape[0]
    w = w_out.shape[1]
    tq, tk = ATTN_Q_TILE, ATTN_K_TILE
    assert seq % tq == 0 and seq % ROW_TILE == 0 and w == N_DIFF_HEADS * V_DIM
    assert depth % N_MIXERS == 0

    bias_tiles = _bias_tiles(rel_bias, tq, tk)
    x2 = x.reshape(bsz * seq, d)
    for i in range(depth):
        j = i // N_MIXERS
        if i % N_MIXERS == 0:
            x2 = _conv_layer(x2, norm_g[i], w_in[i], w_out[i], conv_w[j], seq)
        else:
            w_in_bf = w_in[i].astype(_BF16)
            q, k, vt = _qkv_proj(x2, norm_g[i], w_in_bf[:, :3 * w])
            lam_params = jnp.stack([lambda_q1[j], lambda_k1[j], lambda_q2[j], lambda_k2[j]])
            o = _attention(q, k, vt, bias_tiles, lam_params, subln_g[j], _lambda_init_for(i),
                           bsz)
            x2 = _attn_out(x2, o.reshape(bsz * seq, w), norm_g[i], w_in_bf[:, 3 * w:],
                           w_out[i].astype(_BF16), final_g, final_norm=(i == depth - 1))
    return x2.reshape(bsz, seq, d)
```

```python
import functools
import math

import jax
import jax.numpy as jnp
from jax import lax
from jax.experimental import pallas as pl
from jax.experimental.pallas import tpu as pltpu

N_MIXERS = 2
CONV_K = 3
N_DIFF_HEADS = 8
HEAD_DIM = 64
V_DIM = 2 * HEAD_DIM
N_MAPS = 2 * N_DIFF_HEADS
NUM_BUCKETS = 32
MAX_EXACT = NUM_BUCKETS // 2
REL_MAX_DIST = 128
RMS_EPS = 1e-6

SUBLANES = 8
BF16_SUBLANES = 16
ROW_TILE = 256
ATTN_Q_TILE = 512
ATTN_K_TILE = 256
VT_ROWS = V_DIM + BF16_SUBLANES
VMEM_LIMIT_BYTES = 56 * 1024 * 1024

assert ATTN_Q_TILE == 2 * ATTN_K_TILE and ATTN_K_TILE >= REL_MAX_DIST
assert ROW_TILE == ATTN_K_TILE

_F32 = jnp.float32
_BF16 = jnp.bfloat16
_LOG2E = math.log2(math.e)


def _dot(a, b):
    return jnp.dot(a, b, preferred_element_type=_F32)


def _rmsnorm_rows(x, g):
    return x * lax.rsqrt(jnp.mean(x * x, axis=-1, keepdims=True) + RMS_EPS) * g


def _const_spec(shape):
    return pl.BlockSpec(shape, lambda *_: (0,) * len(shape), pipeline_mode=pl.Buffered(1))


def _params(semantics, flags=None):
    return pltpu.CompilerParams(dimension_semantics=semantics,
                                vmem_limit_bytes=VMEM_LIMIT_BYTES, flags=flags)


def _shift_rows(v, prev, s):
    r = pltpu.roll(v, s, 0)
    p = pltpu.roll(prev, s, 0)
    row = lax.broadcasted_iota(jnp.int32, prev.shape, 0)
    head = jnp.where(row < s, p, r[:SUBLANES])
    return jnp.concatenate([head, r[SUBLANES:]], axis=0)


def _conv_layer_kernel(x_ref, g_ref, win_ref, wout_ref, cw_ref, o_ref, carry_ref, *,
                       tiles_per_seq, width):
    w = width
    x = x_ref[...]
    h = _rmsnorm_rows(x, g_ref[...]).astype(_BF16)
    b_gate = _dot(h, win_ref[:, 0:w])
    c_gate = _dot(h, win_ref[:, w:2 * w])
    u = _dot(h, win_ref[:, 2 * w:3 * w])
    z = _dot(h, win_ref[:, 3 * w:4 * w])
    v = c_gate * u

    @pl.when(pl.program_id(0) % tiles_per_seq == 0)
    def _():
        carry_ref[...] = jnp.zeros_like(carry_ref)

    prev = carry_ref[...]
    conv = (_shift_rows(v, prev, 2) * cw_ref[0:1, :] + _shift_rows(v, prev, 1) * cw_ref[1:2, :]
            + v * cw_ref[2:3, :])
    carry_ref[...] = v[v.shape[0] - SUBLANES:]
    y = (b_gate * conv * jax.nn.silu(z)).astype(_BF16)
    o_ref[...] = x + _dot(y, wout_ref[...])


def _conv_layer(x2, g, w_in, w_out, conv_w, seq):
    n, d = x2.shape
    w = w_out.shape[0]
    tm = ROW_TILE
    return pl.pallas_call(
        functools.partial(_conv_layer_kernel, tiles_per_seq=seq // tm, width=w),
        grid=(n // tm,),
        in_specs=[
            pl.BlockSpec((tm, d), lambda i: (i, 0)),
            _const_spec((1, d)),
            _const_spec((d, 4 * w)),
            _const_spec((w, d)),
            _const_spec((CONV_K, w)),
        ],
        out_specs=pl.BlockSpec((tm, d), lambda i: (i, 0)),
        out_shape=jax.ShapeDtypeStruct((n, d), _F32),
        scratch_shapes=[pltpu.VMEM((SUBLANES, w), _F32)],
        compiler_params=_params(("arbitrary",)),
        name="conv_layer",
    )(x2, g.reshape(1, d), w_in.astype(_BF16), w_out.astype(_BF16), conv_w.T)


def _qkv_kernel(x_ref, g_ref, w_ref, q_ref, k_ref, vt_ref, *, width):
    w = width
    h = _rmsnorm_rows(x_ref[...], g_ref[...]).astype(_BF16)
    q = (_dot(h, w_ref[:, 0:w]) * (HEAD_DIM ** -0.5 * _LOG2E)).astype(_BF16)
    k = _dot(h, w_ref[:, w:2 * w]).astype(_BF16)
    v = _dot(h, w_ref[:, 2 * w:3 * w])
    n_pad = vt_ref.shape[1] - V_DIM
    row = lax.broadcasted_iota(jnp.int32, (n_pad, v.shape[0]), 0)
    ones_rows = jnp.where(row == 0, 1.0, 0.0).astype(_BF16)
    for hd in range(vt_ref.shape[0]):
        cols = slice(hd * V_DIM, (hd + 1) * V_DIM)
        q_ref[hd] = q[:, cols]
        k_ref[hd] = k[:, cols]
        vt_ref[hd, 0:V_DIM, :] = v[:, cols].T.astype(_BF16)
        vt_ref[hd, V_DIM:, :] = ones_rows


def _qkv_proj(x2, g, w_qkv):
    n, d = x2.shape
    w = w_qkv.shape[1] // 3
    tm = ROW_TILE
    n_heads = w // V_DIM
    out = jax.ShapeDtypeStruct((n // tm, n_heads, tm, V_DIM), _BF16)
    head_tiles = pl.BlockSpec((None, n_heads, tm, V_DIM), lambda i: (i, 0, 0, 0))
    return pl.pallas_call(
        functools.partial(_qkv_kernel, width=w),
        grid=(n // tm,),
        in_specs=[
            pl.BlockSpec((tm, d), lambda i: (i, 0)),
            _const_spec((1, d)),
            _const_spec((d, 3 * w)),
        ],
        out_specs=[head_tiles, head_tiles,
                   pl.BlockSpec((None, n_heads, VT_ROWS, tm), lambda i: (i, 0, 0, 0))],
        out_shape=[out, out, jax.ShapeDtypeStruct((n // tm, n_heads, VT_ROWS, tm), _BF16)],
        compiler_params=_params(("arbitrary",)),
        name="qkv_proj",
    )(x2, g.reshape(1, d), w_qkv)


def _attn_kernel(q_ref, k_ref, vt_ref, bias_ref, lam_ref, sg_ref, o_ref,
                 qq_sc, m_sc, acc_sc, s0, s1, t0, t1, p0, p1, a0, a1, *, lambda_init):
    tk = s0.shape[0]
    tq = q_ref.shape[0] * tk
    qi = pl.program_id(2)
    n_tiles = 2 * qi + 2
    slots = ((s0, t0, p0, a0), (s1, t1, p1, a1))

    q = q_ref[...].reshape(tq, V_DIM)
    lane = lax.broadcasted_iota(jnp.int32, q.shape, 1)
    zero = jnp.zeros_like(q)
    qq_sc[0:tq, :] = jnp.where(lane < HEAD_DIM, q, zero)
    qq_sc[tq:2 * tq, :] = jnp.where(lane >= HEAD_DIM, q, zero)

    m_sc[...] = jnp.full_like(m_sc, -jnp.inf)
    acc_sc[...] = jnp.zeros_like(acc_sc)

    def stage_a(j, slot, bias):
        s_ref, t_ref, _, _ = slots[slot]
        s = lax.dot_general(k_ref[j], qq_sc[...], (((1,), (1,)), ((), ())),
                            preferred_element_type=_F32)
        if bias is not None:
            s = s + bias
        s_ref[...] = s
        t_ref[...] = jnp.max(s, axis=0, keepdims=True)

    def stage_a_pair(j, biases):
        kk = k_ref[pl.ds(j, 2)].reshape(2 * tk, V_DIM)
        s = lax.dot_general(kk, qq_sc[...], (((1,), (1,)), ((), ())),
                            preferred_element_type=_F32)
        for slot, bias in enumerate(biases):
            s_ref, t_ref, _, _ = slots[slot]
            tile = s[slot * tk:(slot + 1) * tk]
            if bias is not None:
                tile = tile + bias
            s_ref[...] = tile
            t_ref[...] = jnp.max(tile, axis=0, keepdims=True)

    def stage_b(slot):
        s_ref, t_ref, p_ref, a_ref = slots[slot]
        m_prev = m_sc[...]
        m_new = jnp.maximum(m_prev, t_ref[...])
        alpha = jnp.exp2(m_prev - m_new)
        p_ref[...] = jnp.exp2(s_ref[...] - m_new).astype(_BF16)
        a_ref[...] = alpha
        m_sc[...] = m_new

    def stage_c(j, slot):
        _, _, p_ref, a_ref = slots[slot]
        acc_sc[...] = acc_sc[...] * a_ref[...] + _dot(vt_ref[j], p_ref[...])

    def step_pair(u, bias_even, bias_odd, issue_a=True, issue_c=True):
        i = 2 * u
        if issue_c:
            stage_c(i - 2, 0)
            stage_c(i - 1, 1)
        stage_b(0)
        stage_b(1)
        if issue_a:
            stage_a_pair(i + 2, (bias_even, bias_odd))

    prev, diag0, diag1 = 0, 1, 2
    short_tiles = ([diag0, diag1], [None, prev, diag0, diag1],
                   [None, None, None, prev, diag0, diag1])

    for n_q, tile_biases in enumerate(short_tiles):
        @pl.when(qi == n_q)
        def _(tile_biases=tile_biases):
            for j, bias in enumerate(tile_biases):
                stage_a(j, j % 2, None if bias is None else bias_ref[bias])
                stage_b(j % 2)
                stage_c(j, j % 2)

    @pl.when(qi >= len(short_tiles))
    def _():
        stage_a_pair(0, (None, None))
        step_pair(0, None, None, issue_c=False)

        def plain(u, carry):
            step_pair(u, None, None)
            return carry

        def plain_twice(v, carry):
            step_pair(2 * v + 1, None, None)
            step_pair(2 * v + 2, None, None)
            return carry

        n_double = (qi - 3) // 2
        lax.fori_loop(0, n_double, plain_twice, 0)
        lax.fori_loop(2 * n_double + 1, qi - 2, plain, 0)
        step_pair(qi - 2, None, bias_ref[prev])
        step_pair(qi - 1, bias_ref[diag0], bias_ref[diag1])
        step_pair(qi, None, None, issue_a=False)
        stage_c(n_tiles - 2, 0)
        stage_c(n_tiles - 1, 1)

    lam = (jnp.exp(jnp.sum(lam_ref[0:1, :] * lam_ref[1:2, :], axis=-1, keepdims=True))
           - jnp.exp(jnp.sum(lam_ref[2:3, :] * lam_ref[3:4, :], axis=-1, keepdims=True))
           + lambda_init)
    o_maps = acc_sc[0:V_DIM, :] / acc_sc[V_DIM:V_DIM + 1, :]
    o = (o_maps[:, :tq] - lam * o_maps[:, tq:]).T
    o_ref[...] = (_rmsnorm_rows(o, sg_ref[...]) * (1.0 - lambda_init)).astype(o_ref.dtype)


def _attention(q, k, vt, bias_tiles, lam_params, subln_g, lambda_init, bsz):
    tq, tk = ATTN_Q_TILE, ATTN_K_TILE
    n_heads = q.shape[1]
    s = q.shape[0] * tk // bsz
    w = n_heads * V_DIM
    q_tiles = s // tq

    def col(dtype):
        return pltpu.VMEM((1, 2 * tq), dtype)

    def tile(dtype):
        return pltpu.VMEM((tk, 2 * tq), dtype)

    return pl.pallas_call(
        functools.partial(_attn_kernel, lambda_init=lambda_init),
        grid=(bsz, n_heads, s // tq),
        in_specs=[
            pl.BlockSpec((tq // tk, None, tk, V_DIM), lambda b, h, i: (b * q_tiles + i, h, 0, 0)),
            pl.BlockSpec((s // tk, None, tk, V_DIM), lambda b, h, i: (b, h, 0, 0)),
            pl.BlockSpec((s // tk, None, VT_ROWS, tk), lambda b, h, i: (b, h, 0, 0)),
            pl.BlockSpec((None, 3, tk, 2 * tq), lambda b, h, i: (h, 0, 0, 0)),
            _const_spec((4, HEAD_DIM)),
            _const_spec((1, V_DIM)),
        ],
        out_specs=pl.BlockSpec((None, tq, V_DIM), lambda b, h, i: (b, i, h)),
        out_shape=jax.ShapeDtypeStruct((bsz, s, w), _BF16),
        scratch_shapes=[
            pltpu.VMEM((2 * tq, V_DIM), _BF16),
            col(_F32),
            pltpu.VMEM((VT_ROWS, 2 * tq), _F32),
            tile(_F32), tile(_F32),
            col(_F32), col(_F32),
            tile(_BF16), tile(_BF16),
            col(_F32), col(_F32),
        ],
        compiler_params=_params(("arbitrary", "arbitrary", "arbitrary")),
        name="diff_attention",
    )(q, k, vt, bias_tiles, lam_params, subln_g.reshape(1, V_DIM))


def _attn_out_kernel(x_ref, o_ref, g_ref, wz_ref, wout_ref, fg_ref, y_ref, *, final_norm):
    x = x_ref[...]
    h = _rmsnorm_rows(x, g_ref[...]).astype(_BF16)
    z = _dot(h, wz_ref[...])
    y = (o_ref[...] * jax.nn.silu(z)).astype(_BF16)
    x_new = x + _dot(y, wout_ref[...])
    if final_norm:
        x_new = _rmsnorm_rows(x_new, fg_ref[...])
    y_ref[...] = x_new


def _attn_out(x2, o2, g, w_z, w_out, final_g, final_norm):
    n, d = x2.shape
    w = w_out.shape[0]
    tm = ROW_TILE
    return pl.pallas_call(
        functools.partial(_attn_out_kernel, final_norm=final_norm),
        grid=(n // tm,),
        in_specs=[
            pl.BlockSpec((tm, d), lambda i: (i, 0)),
            pl.BlockSpec((tm, w), lambda i: (i, 0)),
            _const_spec((1, d)),
            _const_spec((d, w)),
            _const_spec((w, d)),
            _const_spec((1, d)),
        ],
        out_specs=pl.BlockSpec((tm, d), lambda i: (i, 0)),
        out_shape=jax.ShapeDtypeStruct((n, d), _F32),
        compiler_params=_params(("arbitrary",)),
        name="attn_out",
    )(x2, o2, g.reshape(1, d), w_z, w_out, final_g.reshape(1, d))


def _t5_causal_bucket(dist):
    d_safe = jnp.maximum(dist, 1).astype(_F32)
    large = MAX_EXACT + (jnp.log(d_safe / MAX_EXACT) / math.log(REL_MAX_DIST / MAX_EXACT)
                         * (NUM_BUCKETS - MAX_EXACT)).astype(jnp.int32)
    large = jnp.minimum(large, NUM_BUCKETS - 1)
    return jnp.where(dist < MAX_EXACT, dist, large)


def _toeplitz_pair(g_vec, rows, cols):
    period = g_vec.shape[-1]
    base = jnp.stack([jnp.roll(g_vec, r, axis=-1) for r in range(SUBLANES)], axis=-2)
    groups = []
    for a in range(rows // SUBLANES):
        shift = a * SUBLANES
        pieces = []
        for m in range(2):
            bm = base[..., m, :, :]
            pieces += [bm[..., period - shift:], bm[..., :cols - shift]]
        groups.append(jnp.concatenate(pieces, axis=-1))
    out = jnp.stack(groups, axis=-3)
    return out.reshape(out.shape[:-3] + (rows, 2 * cols))


def _bias_tiles(rel_bias, tq, tk):
    period = 2 * (tq + tk)
    delta = jnp.arange(period, dtype=jnp.int32)
    delta = jnp.where(delta >= period // 2, delta - period, delta)
    n_dist = tq + tk
    rel = ((rel_bias[_t5_causal_bucket(jnp.arange(n_dist, dtype=jnp.int32))]
            - rel_bias[NUM_BUCKETS - 1]) * _LOG2E).T

    def vec(offset):
        dist = delta + offset
        vals = rel[:, jnp.clip(dist, 0, n_dist - 1)]
        return jnp.where(dist >= 0, vals, -jnp.inf)

    g_vec = jnp.stack([vec(tk), vec(0), vec(-tk)], axis=1)
    g_vec = g_vec.reshape(N_DIFF_HEADS, 2, 3, period).transpose(0, 2, 1, 3)
    return _toeplitz_pair(g_vec, tk, tq)


def _lambda_init_for(layer_idx):
    return 0.8 - 0.6 * math.exp(-0.3 * layer_idx)


def kernel(x, norm_g, w_in, w_out, conv_w, lambda_q1, lambda_k1, lambda_q2, lambda_k2, subln_g,
           rel_bias, final_g):
    bsz, seq, d = x.shape
    depth = norm_g.shape[0]
    w = w_out.shape[1]
    tq, tk = ATTN_Q_TILE, ATTN_K_TILE
    assert seq % tq == 0 and seq % ROW_TILE == 0 and w == N_DIFF_HEADS * V_DIM
    assert depth % N_MIXERS == 0

    bias_tiles = _bias_tiles(rel_bias, tq, tk)
    x2 = x.reshape(bsz * seq, d)
    for i in range(depth):
        j = i // N_MIXERS
        if i % N_MIXERS == 0:
            x2 = _conv_layer(x2, norm_g[i], w_in[i], w_out[i], conv_w[j], seq)
        else:
            w_in_bf = w_in[i].astype(_BF16)
            q, k, vt = _qkv_proj(x2, norm_g[i], w_in_bf[:, :3 * w])
            lam_params = jnp.stack([lambda_q1[j], lambda_k1[j], lambda_q2[j], lambda_k2[j]])
            o = _attention(q, k, vt, bias_tiles, lam_params, subln_g[j], _lambda_init_for(i),
                           bsz)
            x2 = _attn_out(x2, o.reshape(bsz * seq, w), norm_g[i], w_in_bf[:, 3 * w:],
                           w_out[i].astype(_BF16), final_g, final_norm=(i == depth - 1))
    return x2.reshape(bsz, seq, d)
```

```python
import functools
import math

import jax
import jax.numpy as jnp
from jax import lax
from jax.experimental import pallas as pl
from jax.experimental.pallas import tpu as pltpu

N_MIXERS = 2
CONV_K = 3
N_DIFF_HEADS = 8
HEAD_DIM = 64
V_DIM = 2 * HEAD_DIM
N_MAPS = 2 * N_DIFF_HEADS
NUM_BUCKETS = 32
MAX_EXACT = NUM_BUCKETS // 2
REL_MAX_DIST = 128
RMS_EPS = 1e-6

SUBLANES = 8
BF16_SUBLANES = 16
ROW_TILE = 512
ATTN_Q_TILE = 512
ATTN_K_TILE = 256
VT_ROWS = V_DIM + BF16_SUBLANES
VMEM_LIMIT_BYTES = 56 * 1024 * 1024

assert ATTN_Q_TILE == 2 * ATTN_K_TILE and ATTN_K_TILE >= REL_MAX_DIST
assert ROW_TILE % ATTN_K_TILE == 0

_F32 = jnp.float32
_BF16 = jnp.bfloat16
_LOG2E = math.log2(math.e)


def _dot(a, b):
    return jnp.dot(a, b, preferred_element_type=_F32)


def _rmsnorm_rows(x, g):
    return x * lax.rsqrt(jnp.mean(x * x, axis=-1, keepdims=True) + RMS_EPS) * g


def _const_spec(shape):
    return pl.BlockSpec(shape, lambda *_: (0,) * len(shape), pipeline_mode=pl.Buffered(1))


def _params(semantics, flags=None):
    return pltpu.CompilerParams(dimension_semantics=semantics,
                                vmem_limit_bytes=VMEM_LIMIT_BYTES, flags=flags)


def _shift_rows(v, prev, s):
    r = pltpu.roll(v, s, 0)
    p = pltpu.roll(prev, s, 0)
    row = lax.broadcasted_iota(jnp.int32, prev.shape, 0)
    head = jnp.where(row < s, p, r[:SUBLANES])
    return jnp.concatenate([head, r[SUBLANES:]], axis=0)


def _conv_layer_kernel(x_ref, g_ref, win_ref, wout_ref, cw_ref, o_ref, carry_ref, *,
                       tiles_per_seq, width):
    w = width
    x = x_ref[...]
    h = _rmsnorm_rows(x, g_ref[...]).astype(_BF16)
    b_gate = _dot(h, win_ref[:, 0:w])
    c_gate = _dot(h, win_ref[:, w:2 * w])
    u = _dot(h, win_ref[:, 2 * w:3 * w])
    z = _dot(h, win_ref[:, 3 * w:4 * w])
    v = c_gate * u

    @pl.when(pl.program_id(0) % tiles_per_seq == 0)
    def _():
        carry_ref[...] = jnp.zeros_like(carry_ref)

    prev = carry_ref[...]
    conv = (_shift_rows(v, prev, 2) * cw_ref[0:1, :] + _shift_rows(v, prev, 1) * cw_ref[1:2, :]
            + v * cw_ref[2:3, :])
    carry_ref[...] = v[v.shape[0] - SUBLANES:]
    y = (b_gate * conv * jax.nn.silu(z)).astype(_BF16)
    o_ref[...] = x + _dot(y, wout_ref[...])


def _conv_layer(x2, g, w_in, w_out, conv_w, seq):
    n, d = x2.shape
    w = w_out.shape[0]
    tm = ROW_TILE
    return pl.pallas_call(
        functools.partial(_conv_layer_kernel, tiles_per_seq=seq // tm, width=w),
        grid=(n // tm,),
        in_specs=[
            pl.BlockSpec((tm, d), lambda i: (i, 0)),
            _const_spec((1, d)),
            _const_spec((d, 4 * w)),
            _const_spec((w, d)),
            _const_spec((CONV_K, w)),
        ],
        out_specs=pl.BlockSpec((tm, d), lambda i: (i, 0)),
        out_shape=jax.ShapeDtypeStruct((n, d), _F32),
        scratch_shapes=[pltpu.VMEM((SUBLANES, w), _F32)],
        compiler_params=_params(("arbitrary",)),
        name="conv_layer",
    )(x2, g.reshape(1, d), w_in.astype(_BF16), w_out.astype(_BF16), conv_w.T)


def _qkv_kernel(x_ref, g_ref, w_ref, q_ref, k_ref, vt_ref, *, width):
    w = width
    h = _rmsnorm_rows(x_ref[...], g_ref[...]).astype(_BF16)
    q = (_dot(h, w_ref[:, 0:w]) * (HEAD_DIM ** -0.5 * _LOG2E)).astype(_BF16)
    k = _dot(h, w_ref[:, w:2 * w]).astype(_BF16)
    v = _dot(h, w_ref[:, 2 * w:3 * w])
    n_tiles, n_heads, vt_rows, tk = vt_ref.shape
    row = lax.broadcasted_iota(jnp.int32, (vt_rows - V_DIM, tk), 0)
    ones_rows = jnp.where(row == 0, 1.0, 0.0).astype(_BF16)
    for t in range(n_tiles):
        rows = slice(t * tk, (t + 1) * tk)
        for hd in range(n_heads):
            cols = slice(hd * V_DIM, (hd + 1) * V_DIM)
            q_ref[t, hd] = q[rows, cols]
            k_ref[t, hd] = k[rows, cols]
            vt_ref[t, hd, 0:V_DIM, :] = v[rows, cols].T.astype(_BF16)
            vt_ref[t, hd, V_DIM:, :] = ones_rows


def _qkv_proj(x2, g, w_qkv):
    n, d = x2.shape
    w = w_qkv.shape[1] // 3
    tm, tk = ROW_TILE, ATTN_K_TILE
    n_heads = w // V_DIM
    out = jax.ShapeDtypeStruct((n // tk, n_heads, tk, V_DIM), _BF16)
    head_tiles = pl.BlockSpec((tm // tk, n_heads, tk, V_DIM), lambda i: (i, 0, 0, 0))
    return pl.pallas_call(
        functools.partial(_qkv_kernel, width=w),
        grid=(n // tm,),
        in_specs=[
            pl.BlockSpec((tm, d), lambda i: (i, 0)),
            _const_spec((1, d)),
            _const_spec((d, 3 * w)),
        ],
        out_specs=[head_tiles, head_tiles,
                   pl.BlockSpec((tm // tk, n_heads, VT_ROWS, tk), lambda i: (i, 0, 0, 0))],
        out_shape=[out, out, jax.ShapeDtypeStruct((n // tk, n_heads, VT_ROWS, tk), _BF16)],
        compiler_params=_params(("arbitrary",)),
        name="qkv_proj",
    )(x2, g.reshape(1, d), w_qkv)


def _attn_kernel(q_ref, k_ref, vt_ref, bias_ref, lam_ref, sg_ref, o_ref,
                 qq_sc, m_sc, acc_sc, s0, s1, t0, t1, p0, p1, a0, a1, *, lambda_init):
    tk = s0.shape[0]
    tq = q_ref.shape[0] * tk
    qi = pl.program_id(2)
    n_tiles = 2 * qi + 2
    slots = ((s0, t0, p0, a0), (s1, t1, p1, a1))

    q = q_ref[...].reshape(tq, V_DIM)
    lane = lax.broadcasted_iota(jnp.int32, q.shape, 1)
    zero = jnp.zeros_like(q)
    qq_sc[0:tq, :] = jnp.where(lane < HEAD_DIM, q, zero)
    qq_sc[tq:2 * tq, :] = jnp.where(lane >= HEAD_DIM, q, zero)

    m_sc[...] = jnp.full_like(m_sc, -jnp.inf)
    acc_sc[...] = jnp.zeros_like(acc_sc)

    def stage_a(j, slot, bias):
        s_ref, t_ref, _, _ = slots[slot]
        s = lax.dot_general(k_ref[j], qq_sc[...], (((1,), (1,)), ((), ())),
                            preferred_element_type=_F32)
        if bias is not None:
            s = s + bias
        s_ref[...] = s
        t_ref[...] = jnp.max(s, axis=0, keepdims=True)

    def stage_a_pair(j, biases):
        kk = k_ref[pl.ds(j, 2)].reshape(2 * tk, V_DIM)
        s = lax.dot_general(kk, qq_sc[...], (((1,), (1,)), ((), ())),
                            preferred_element_type=_F32)
        for slot, bias in enumerate(biases):
            s_ref, t_ref, _, _ = slots[slot]
            tile = s[slot * tk:(slot + 1) * tk]
            if bias is not None:
                tile = tile + bias
            s_ref[...] = tile
            t_ref[...] = jnp.max(tile, axis=0, keepdims=True)

    def stage_b(slot):
        s_ref, t_ref, p_ref, a_ref = slots[slot]
        m_prev = m_sc[...]
        m_new = jnp.maximum(m_prev, t_ref[...])
        alpha = jnp.exp2(m_prev - m_new)
        p_ref[...] = jnp.exp2(s_ref[...] - m_new).astype(_BF16)
        a_ref[...] = alpha
        m_sc[...] = m_new

    def stage_c(j, slot):
        _, _, p_ref, a_ref = slots[slot]
        acc_sc[...] = acc_sc[...] * a_ref[...] + _dot(vt_ref[j], p_ref[...])

    def step_pair(u, bias_even, bias_odd, issue_a=True, issue_c=True):
        i = 2 * u
        if issue_c:
            stage_c(i - 2, 0)
            stage_c(i - 1, 1)
        stage_b(0)
        stage_b(1)
        if issue_a:
            stage_a_pair(i + 2, (bias_even, bias_odd))

    prev, diag0, diag1 = 0, 1, 2
    short_tiles = ([diag0, diag1], [None, prev, diag0, diag1],
                   [None, None, None, prev, diag0, diag1])

    for n_q, tile_biases in enumerate(short_tiles):
        @pl.when(qi == n_q)
        def _(tile_biases=tile_biases):
            for j, bias in enumerate(tile_biases):
                stage_a(j, j % 2, None if bias is None else bias_ref[bias])
                stage_b(j % 2)
                stage_c(j, j % 2)

    @pl.when(qi >= len(short_tiles))
    def _():
        stage_a_pair(0, (None, None))
        step_pair(0, None, None, issue_c=False)

        def plain(u, carry):
            step_pair(u, None, None)
            return carry

        def plain_twice(v, carry):
            step_pair(2 * v + 1, None, None)
            step_pair(2 * v + 2, None, None)
            return carry

        n_double = (qi - 3) // 2
        lax.fori_loop(0, n_double, plain_twice, 0)
        lax.fori_loop(2 * n_double + 1, qi - 2, plain, 0)
        step_pair(qi - 2, None, bias_ref[prev])
        step_pair(qi - 1, bias_ref[diag0], bias_ref[diag1])
        step_pair(qi, None, None, issue_a=False)
        stage_c(n_tiles - 2, 0)
        stage_c(n_tiles - 1, 1)

    lam = (jnp.exp(jnp.sum(lam_ref[0:1, :] * lam_ref[1:2, :], axis=-1, keepdims=True))
           - jnp.exp(jnp.sum(lam_ref[2:3, :] * lam_ref[3:4, :], axis=-1, keepdims=True))
           + lambda_init)
    o_maps = acc_sc[0:V_DIM, :] / acc_sc[V_DIM:V_DIM + 1, :]
    o_t = o_maps[:, :tq] - lam * o_maps[:, tq:]
    o_t = o_t * lax.rsqrt(jnp.mean(o_t * o_t, axis=0, keepdims=True) + RMS_EPS)
    o_ref[...] = (o_t.T * sg_ref[...] * (1.0 - lambda_init)).astype(o_ref.dtype)


def _attention(q, k, vt, bias_tiles, lam_params, subln_g, lambda_init, bsz):
    tq, tk = ATTN_Q_TILE, ATTN_K_TILE
    n_heads = q.shape[1]
    s = q.shape[0] * tk // bsz
    w = n_heads * V_DIM
    q_tiles = s // tq

    def col(dtype):
        return pltpu.VMEM((1, 2 * tq), dtype)

    def tile(dtype):
        return pltpu.VMEM((tk, 2 * tq), dtype)

    return pl.pallas_call(
        functools.partial(_attn_kernel, lambda_init=lambda_init),
        grid=(bsz, n_heads, s // tq),
        in_specs=[
            pl.BlockSpec((tq // tk, None, tk, V_DIM), lambda b, h, i: (b * q_tiles + i, h, 0, 0)),
            pl.BlockSpec((s // tk, None, tk, V_DIM), lambda b, h, i: (b, h, 0, 0)),
            pl.BlockSpec((s // tk, None, VT_ROWS, tk), lambda b, h, i: (b, h, 0, 0)),
            pl.BlockSpec((None, 3, tk, 2 * tq), lambda b, h, i: (h, 0, 0, 0)),
            _const_spec((4, HEAD_DIM)),
            _const_spec((1, V_DIM)),
        ],
        out_specs=pl.BlockSpec((None, tq, V_DIM), lambda b, h, i: (b, i, h)),
        out_shape=jax.ShapeDtypeStruct((bsz, s, w), _BF16),
        scratch_shapes=[
            pltpu.VMEM((2 * tq, V_DIM), _BF16),
            col(_F32),
            pltpu.VMEM((VT_ROWS, 2 * tq), _F32),
            tile(_F32), tile(_F32),
            col(_F32), col(_F32),
            tile(_BF16), tile(_BF16),
            col(_F32), col(_F32),
        ],
        compiler_params=_params(("arbitrary", "arbitrary", "arbitrary")),
        name="diff_attention",
    )(q, k, vt, bias_tiles, lam_params, subln_g.reshape(1, V_DIM))


def _attn_out_kernel(x_ref, o_ref, g_ref, wz_ref, wout_ref, fg_ref, y_ref, *, final_norm):
    x = x_ref[...]
    h = _rmsnorm_rows(x, g_ref[...]).astype(_BF16)
    z = _dot(h, wz_ref[...])
    y = (o_ref[...] * jax.nn.silu(z)).astype(_BF16)
    x_new = x + _dot(y, wout_ref[...])
    if final_norm:
        x_new = _rmsnorm_rows(x_new, fg_ref[...])
    y_ref[...] = x_new


def _attn_out(x2, o2, g, w_z, w_out, final_g, final_norm):
    n, d = x2.shape
    w = w_out.shape[0]
    tm = ROW_TILE
    return pl.pallas_call(
        functools.partial(_attn_out_kernel, final_norm=final_norm),
        grid=(n // tm,),
        in_specs=[
            pl.BlockSpec((tm, d), lambda i: (i, 0)),
            pl.BlockSpec((tm, w), lambda i: (i, 0)),
            _const_spec((1, d)),
            _const_spec((d, w)),
            _const_spec((w, d)),
            _const_spec((1, d)),
        ],
        out_specs=pl.BlockSpec((tm, d), lambda i: (i, 0)),
        out_shape=jax.ShapeDtypeStruct((n, d), _F32),
        compiler_params=_params(("arbitrary",)),
        name="attn_out",
    )(x2, o2, g.reshape(1, d), w_z, w_out, final_g.reshape(1, d))


def _t5_causal_bucket(dist):
    d_safe = jnp.maximum(dist, 1).astype(_F32)
    large = MAX_EXACT + (jnp.log(d_safe / MAX_EXACT) / math.log(REL_MAX_DIST / MAX_EXACT)
                         * (NUM_BUCKETS - MAX_EXACT)).astype(jnp.int32)
    large = jnp.minimum(large, NUM_BUCKETS - 1)
    return jnp.where(dist < MAX_EXACT, dist, large)


def _toeplitz_pair(g_vec, rows, cols):
    period = g_vec.shape[-1]
    base = jnp.stack([jnp.roll(g_vec, r, axis=-1) for r in range(SUBLANES)], axis=-2)
    groups = []
    for a in range(rows // SUBLANES):
        shift = a * SUBLANES
        pieces = []
        for m in range(2):
            bm = base[..., m, :, :]
            pieces += [bm[..., period - shift:], bm[..., :cols - shift]]
        groups.append(jnp.concatenate(pieces, axis=-1))
    out = jnp.stack(groups, axis=-3)
    return out.reshape(out.shape[:-3] + (rows, 2 * cols))


def _bias_tiles(rel_bias, tq, tk):
    period = 2 * (tq + tk)
    delta = jnp.arange(period, dtype=jnp.int32)
    delta = jnp.where(delta >= period // 2, delta - period, delta)
    n_dist = tq + tk
    rel = ((rel_bias[_t5_causal_bucket(jnp.arange(n_dist, dtype=jnp.int32))]
            - rel_bias[NUM_BUCKETS - 1]) * _LOG2E).T

    def vec(offset):
        dist = delta + offset
        vals = rel[:, jnp.clip(dist, 0, n_dist - 1)]
        return jnp.where(dist >= 0, vals, -jnp.inf)

    g_vec = jnp.stack([vec(tk), vec(0), vec(-tk)], axis=1)
    g_vec = g_vec.reshape(N_DIFF_HEADS, 2, 3, period).transpose(0, 2, 1, 3)
    return _toeplitz_pair(g_vec, tk, tq)


def _lambda_init_for(layer_idx):
    return 0.8 - 0.6 * math.exp(-0.3 * layer_idx)


def kernel(x, norm_g, w_in, w_out, conv_w, lambda_q1, lambda_k1, lambda_q2, lambda_k2, subln_g,
           rel_bias, final_g):
    bsz, seq, d = x.shape
    depth = norm_g.shape[0]
    w = w_out.shape[1]
    tq, tk = ATTN_Q_TILE, ATTN_K_TILE
    assert seq % tq == 0 and seq % ROW_TILE == 0 and w == N_DIFF_HEADS * V_DIM
    assert depth % N_MIXERS == 0

    bias_tiles = _bias_tiles(rel_bias, tq, tk)
    x2 = x.reshape(bsz * seq, d)
    for i in range(depth):
        j = i // N_MIXERS
        if i % N_MIXERS == 0:
            x2 = _conv_layer(x2, norm_g[i], w_in[i], w_out[i], conv_w[j], seq)
        else:
            w_in_bf = w_in[i].astype(_BF16)
            q, k, vt = _qkv_proj(x2, norm_g[i], w_in_bf[:, :3 * w])
            lam_params = jnp.stack([lambda_q1[j], lambda_k1[j], lambda_q2[j], lambda_k2[j]])
            o = _attention(q, k, vt, bias_tiles, lam_params, subln_g[j], _lambda_init_for(i),
                           bsz)
            x2 = _attn_out(x2, o.reshape(bsz * seq, w), norm_g[i], w_in_bf[:, 3 * w:],
                           w_out[i].astype(_BF16), final_g, final_norm=(i == depth - 1))
    return x2.reshape(bsz, seq, d)
```

```python
import functools
import math

import jax
import jax.numpy as jnp
from jax import lax
from jax.experimental import pallas as pl
from jax.experimental.pallas import tpu as pltpu

N_MIXERS = 2
CONV_K = 3
N_DIFF_HEADS = 8
HEAD_DIM = 64
V_DIM = 2 * HEAD_DIM
N_MAPS = 2 * N_DIFF_HEADS
NUM_BUCKETS = 32
MAX_EXACT = NUM_BUCKETS // 2
REL_MAX_DIST = 128
RMS_EPS = 1e-6

SUBLANES = 8
BF16_SUBLANES = 16
ROW_TILE = 512
ATTN_Q_TILE = 512
ATTN_K_TILE = 256
ATTN_HEADS_PER_STEP = 2
VT_ROWS = V_DIM + BF16_SUBLANES
VMEM_LIMIT_BYTES = 56 * 1024 * 1024

assert ATTN_Q_TILE == 2 * ATTN_K_TILE and ATTN_K_TILE >= REL_MAX_DIST
assert ROW_TILE % ATTN_K_TILE == 0

_F32 = jnp.float32
_BF16 = jnp.bfloat16
_LOG2E = math.log2(math.e)


def _dot(a, b):
    return jnp.dot(a, b, preferred_element_type=_F32)


def _rmsnorm_rows(x, g):
    return x * lax.rsqrt(jnp.mean(x * x, axis=-1, keepdims=True) + RMS_EPS) * g


def _const_spec(shape):
    return pl.BlockSpec(shape, lambda *_: (0,) * len(shape), pipeline_mode=pl.Buffered(1))


def _params(semantics):
    return pltpu.CompilerParams(dimension_semantics=semantics,
                                vmem_limit_bytes=VMEM_LIMIT_BYTES)


def _shift_rows(v, prev, s):
    r = pltpu.roll(v, s, 0)
    p = pltpu.roll(prev, s, 0)
    row = lax.broadcasted_iota(jnp.int32, prev.shape, 0)
    head = jnp.where(row < s, p, r[:SUBLANES])
    return jnp.concatenate([head, r[SUBLANES:]], axis=0)


def _conv_layer_kernel(x_ref, g_ref, win_ref, wout_ref, cw_ref, o_ref, carry_ref, *,
                       tiles_per_seq, width):
    w = width
    x = x_ref[...]
    h = _rmsnorm_rows(x, g_ref[...]).astype(_BF16)
    b_gate = _dot(h, win_ref[:, 0:w])
    c_gate = _dot(h, win_ref[:, w:2 * w])
    u = _dot(h, win_ref[:, 2 * w:3 * w])
    z = _dot(h, win_ref[:, 3 * w:4 * w])
    v = c_gate * u

    @pl.when(pl.program_id(0) % tiles_per_seq == 0)
    def _():
        carry_ref[...] = jnp.zeros_like(carry_ref)

    prev = carry_ref[...]
    conv = (_shift_rows(v, prev, 2) * cw_ref[0:1, :] + _shift_rows(v, prev, 1) * cw_ref[1:2, :]
            + v * cw_ref[2:3, :])
    carry_ref[...] = v[v.shape[0] - SUBLANES:]
    y = (b_gate * conv * jax.nn.silu(z)).astype(_BF16)
    o_ref[...] = x + _dot(y, wout_ref[...])


def _conv_layer(x2, g, w_in, w_out, conv_w, seq):
    n, d = x2.shape
    w = w_out.shape[0]
    tm = ROW_TILE
    return pl.pallas_call(
        functools.partial(_conv_layer_kernel, tiles_per_seq=seq // tm, width=w),
        grid=(n // tm,),
        in_specs=[
            pl.BlockSpec((tm, d), lambda i: (i, 0)),
            _const_spec((1, d)),
            _const_spec((d, 4 * w)),
            _const_spec((w, d)),
            _const_spec((CONV_K, w)),
        ],
        out_specs=pl.BlockSpec((tm, d), lambda i: (i, 0)),
        out_shape=jax.ShapeDtypeStruct((n, d), _F32),
        scratch_shapes=[pltpu.VMEM((SUBLANES, w), _F32)],
        compiler_params=_params(("arbitrary",)),
        name="conv_layer",
    )(x2, g.reshape(1, d), w_in.astype(_BF16), w_out.astype(_BF16), conv_w.T)


def _qkv_kernel(x_ref, g_ref, w_ref, q_ref, k_ref, vt_ref, *, width):
    w = width
    h = _rmsnorm_rows(x_ref[...], g_ref[...]).astype(_BF16)
    q = (_dot(h, w_ref[:, 0:w]) * (HEAD_DIM ** -0.5 * _LOG2E)).astype(_BF16)
    k = _dot(h, w_ref[:, w:2 * w]).astype(_BF16)
    v = _dot(h, w_ref[:, 2 * w:3 * w])
    n_tiles, n_heads, vt_rows, tk = vt_ref.shape
    row = lax.broadcasted_iota(jnp.int32, (vt_rows - V_DIM, tk), 0)
    ones_rows = jnp.where(row == 0, 1.0, 0.0).astype(_BF16)
    for t in range(n_tiles):
        rows = slice(t * tk, (t + 1) * tk)
        for hd in range(n_heads):
            cols = slice(hd * V_DIM, (hd + 1) * V_DIM)
            q_ref[t, hd] = q[rows, cols]
            k_ref[t, hd] = k[rows, cols]
            vt_ref[t, hd, 0:V_DIM, :] = v[rows, cols].T.astype(_BF16)
            vt_ref[t, hd, V_DIM:, :] = ones_rows


def _qkv_proj(x2, g, w_qkv):
    n, d = x2.shape
    w = w_qkv.shape[1] // 3
    tm, tk = ROW_TILE, ATTN_K_TILE
    n_heads = w // V_DIM
    out = jax.ShapeDtypeStruct((n // tk, n_heads, tk, V_DIM), _BF16)
    head_tiles = pl.BlockSpec((tm // tk, n_heads, tk, V_DIM), lambda i: (i, 0, 0, 0))
    return pl.pallas_call(
        functools.partial(_qkv_kernel, width=w),
        grid=(n // tm,),
        in_specs=[
            pl.BlockSpec((tm, d), lambda i: (i, 0)),
            _const_spec((1, d)),
            _const_spec((d, 3 * w)),
        ],
        out_specs=[head_tiles, head_tiles,
                   pl.BlockSpec((tm // tk, n_heads, VT_ROWS, tk), lambda i: (i, 0, 0, 0))],
        out_shape=[out, out, jax.ShapeDtypeStruct((n // tk, n_heads, VT_ROWS, tk), _BF16)],
        compiler_params=_params(("arbitrary",)),
        name="qkv_proj",
    )(x2, g.reshape(1, d), w_qkv)


def _attn_kernel(q_ref, k_ref, vt_ref, bias_ref, lam_ref, sg_ref, o_ref, *scratch, lambda_init):
    n_heads = q_ref.shape[1]
    per_head = len(scratch) // n_heads
    heads = tuple(scratch[h * per_head:(h + 1) * per_head] for h in range(n_heads))
    tk = heads[0][3].shape[0]
    tq = q_ref.shape[0] * tk
    qi = pl.program_id(2)
    n_tiles = 2 * qi + 2

    def slot_refs(h, slot):
        _, _, _, s0, s1, t0, t1, p0, p1, a0, a1 = heads[h]
        return ((s0, t0, p0, a0), (s1, t1, p1, a1))[slot]

    for h in range(n_heads):
        qq_sc, m_sc, acc_sc = heads[h][:3]
        q = q_ref[:, h].reshape(tq, V_DIM)
        lane = lax.broadcasted_iota(jnp.int32, q.shape, 1)
        zero = jnp.zeros_like(q)
        qq_sc[0:tq, :] = jnp.where(lane < HEAD_DIM, q, zero)
        qq_sc[tq:2 * tq, :] = jnp.where(lane >= HEAD_DIM, q, zero)
        m_sc[...] = jnp.full_like(m_sc, -jnp.inf)
        acc_sc[...] = jnp.zeros_like(acc_sc)

    def score_tiles(h, keys, biases):
        s = lax.dot_general(keys, heads[h][0][...], (((1,), (1,)), ((), ())),
                            preferred_element_type=_F32)
        for slot, bias in enumerate(biases):
            s_ref, t_ref, _, _ = slot_refs(h, slot)
            tile = s[slot * tk:(slot + 1) * tk]
            if bias is not None:
                tile = tile + bias_ref[h, bias]
            s_ref[...] = tile
            t_ref[...] = jnp.max(tile, axis=0, keepdims=True)

    def stage_a_pair(j, biases):
        for h in range(n_heads):
            score_tiles(h, k_ref[pl.ds(j, 2), h].reshape(2 * tk, V_DIM), biases)

    def stage_b(slot):
        for h in range(n_heads):
            m_sc = heads[h][1]
            s_ref, t_ref, p_ref, a_ref = slot_refs(h, slot)
            m_prev = m_sc[...]
            m_new = jnp.maximum(m_prev, t_ref[...])
            alpha = jnp.exp2(m_prev - m_new)
            p_ref[...] = jnp.exp2(s_ref[...] - m_new).astype(_BF16)
            a_ref[...] = alpha
            m_sc[...] = m_new

    def stage_c(j, slot):
        for h in range(n_heads):
            acc_sc = heads[h][2]
            _, _, p_ref, a_ref = slot_refs(h, slot)
            acc_sc[...] = acc_sc[...] * a_ref[...] + _dot(vt_ref[j, h], p_ref[...])

    def stage_c_pair(j):
        for h in range(n_heads):
            acc_sc = heads[h][2]
            _, _, p0, a0 = slot_refs(h, 0)
            _, _, p1, a1 = slot_refs(h, 1)
            rescale1 = a1[...]
            pv = _dot(vt_ref[j, h], p0[...]) * rescale1 + _dot(vt_ref[j + 1, h], p1[...])
            acc_sc[...] = acc_sc[...] * (a0[...] * rescale1) + pv

    def step_pair(u, bias_even, bias_odd, issue_a=True, issue_c=True):
        i = 2 * u
        if issue_c:
            stage_c_pair(i - 2)
        stage_b(0)
        stage_b(1)
        if issue_a:
            stage_a_pair(i + 2, (bias_even, bias_odd))

    prev, diag0, diag1 = 0, 1, 2
    short_tiles = ([diag0, diag1], [None, prev, diag0, diag1],
                   [None, None, None, prev, diag0, diag1])

    for n_q, tile_biases in enumerate(short_tiles):
        @pl.when(qi == n_q)
        def _(tile_biases=tile_biases):
            for j, bias in enumerate(tile_biases):
                for h in range(n_heads):
                    score_tiles(h, k_ref[j, h], (bias,))
                stage_b(0)
                stage_c(j, 0)

    @pl.when(qi >= len(short_tiles))
    def _():
        stage_a_pair(0, (None, None))
        step_pair(0, None, None, issue_c=False)

        def plain(u, carry):
            step_pair(u, None, None)
            return carry

        lax.fori_loop(1, qi - 2, plain, 0)
        step_pair(qi - 2, None, prev)
        step_pair(qi - 1, diag0, diag1)
        step_pair(qi, None, None, issue_a=False)
        stage_c_pair(n_tiles - 2)

    lam = (jnp.exp(jnp.sum(lam_ref[0:1, :] * lam_ref[1:2, :], axis=-1, keepdims=True))
           - jnp.exp(jnp.sum(lam_ref[2:3, :] * lam_ref[3:4, :], axis=-1, keepdims=True))
           + lambda_init)
    for h in range(n_heads):
        acc_sc = heads[h][2]
        o_maps = acc_sc[0:V_DIM, :] / acc_sc[V_DIM:V_DIM + 1, :]
        o_t = o_maps[:, :tq] - lam * o_maps[:, tq:]
        o_t = o_t * lax.rsqrt(jnp.mean(o_t * o_t, axis=0, keepdims=True) + RMS_EPS)
        o_ref[:, h * V_DIM:(h + 1) * V_DIM] = (
            o_t.T * sg_ref[...] * (1.0 - lambda_init)).astype(o_ref.dtype)


def _attention(q, k, vt, bias_tiles, lam_params, subln_g, lambda_init, bsz):
    tq, tk, hps = ATTN_Q_TILE, ATTN_K_TILE, ATTN_HEADS_PER_STEP
    n_heads = q.shape[1]
    s = q.shape[0] * tk // bsz
    w = n_heads * V_DIM
    q_tiles = s // tq
    col = pltpu.VMEM((1, 2 * tq), _F32)
    per_head_scratch = [
        pltpu.VMEM((2 * tq, V_DIM), _BF16),
        col,
        pltpu.VMEM((VT_ROWS, 2 * tq), _F32),
        pltpu.VMEM((tk, 2 * tq), _F32), pltpu.VMEM((tk, 2 * tq), _F32),
        col, col,
        pltpu.VMEM((tk, 2 * tq), _BF16), pltpu.VMEM((tk, 2 * tq), _BF16),
        col, col,
    ]
    return pl.pallas_call(
        functools.partial(_attn_kernel, lambda_init=lambda_init),
        grid=(bsz, n_heads // hps, q_tiles),
        in_specs=[
            pl.BlockSpec((tq // tk, hps, tk, V_DIM), lambda b, h, i: (b * q_tiles + i, h, 0, 0)),
            pl.BlockSpec((s // tk, hps, tk, V_DIM), lambda b, h, i: (b, h, 0, 0)),
            pl.BlockSpec((s // tk, hps, VT_ROWS, tk), lambda b, h, i: (b, h, 0, 0)),
            pl.BlockSpec((hps, 3, tk, 2 * tq), lambda b, h, i: (h, 0, 0, 0)),
            _const_spec((4, HEAD_DIM)),
            _const_spec((1, V_DIM)),
        ],
        out_specs=pl.BlockSpec((None, tq, hps * V_DIM), lambda b, h, i: (b, i, h)),
        out_shape=jax.ShapeDtypeStruct((bsz, s, w), _BF16),
        scratch_shapes=per_head_scratch * hps,
        compiler_params=_params(("arbitrary", "arbitrary", "arbitrary")),
        name="diff_attention",
    )(q, k, vt, bias_tiles, lam_params, subln_g.reshape(1, V_DIM))


def _attn_out_kernel(x_ref, o_ref, g_ref, wz_ref, wout_ref, fg_ref, y_ref, *, final_norm):
    x = x_ref[...]
    h = _rmsnorm_rows(x, g_ref[...]).astype(_BF16)
    z = _dot(h, wz_ref[...])
    y = (o_ref[...] * jax.nn.silu(z)).astype(_BF16)
    x_new = x + _dot(y, wout_ref[...])
    if final_norm:
        x_new = _rmsnorm_rows(x_new, fg_ref[...])
    y_ref[...] = x_new


def _attn_out(x2, o2, g, w_z, w_out, final_g, final_norm):
    n, d = x2.shape
    w = w_out.shape[0]
    tm = ROW_TILE
    return pl.pallas_call(
        functools.partial(_attn_out_kernel, final_norm=final_norm),
        grid=(n // tm,),
        in_specs=[
            pl.BlockSpec((tm, d), lambda i: (i, 0)),
            pl.BlockSpec((tm, w), lambda i: (i, 0)),
            _const_spec((1, d)),
            _const_spec((d, w)),
            _const_spec((w, d)),
            _const_spec((1, d)),
        ],
        out_specs=pl.BlockSpec((tm, d), lambda i: (i, 0)),
        out_shape=jax.ShapeDtypeStruct((n, d), _F32),
        compiler_params=_params(("arbitrary",)),
        name="attn_out",
    )(x2, o2, g.reshape(1, d), w_z, w_out, final_g.reshape(1, d))


def _t5_causal_bucket(dist):
    d_safe = jnp.maximum(dist, 1).astype(_F32)
    large = MAX_EXACT + (jnp.log(d_safe / MAX_EXACT) / math.log(REL_MAX_DIST / MAX_EXACT)
                         * (NUM_BUCKETS - MAX_EXACT)).astype(jnp.int32)
    large = jnp.minimum(large, NUM_BUCKETS - 1)
    return jnp.where(dist < MAX_EXACT, dist, large)


def _toeplitz_pair(g_vec, rows, cols):
    period = g_vec.shape[-1]
    base = jnp.stack([jnp.roll(g_vec, r, axis=-1) for r in range(SUBLANES)], axis=-2)
    groups = []
    for a in range(rows // SUBLANES):
        shift = a * SUBLANES
        pieces = []
        for m in range(2):
            bm = base[..., m, :, :]
            pieces += [bm[..., period - shift:], bm[..., :cols - shift]]
        groups.append(jnp.concatenate(pieces, axis=-1))
    out = jnp.stack(groups, axis=-3)
    return out.reshape(out.shape[:-3] + (rows, 2 * cols))


def _bias_tiles(rel_bias, tq, tk):
    period = 2 * (tq + tk)
    delta = jnp.arange(period, dtype=jnp.int32)
    delta = jnp.where(delta >= period // 2, delta - period, delta)
    n_dist = tq + tk
    rel = ((rel_bias[_t5_causal_bucket(jnp.arange(n_dist, dtype=jnp.int32))]
            - rel_bias[NUM_BUCKETS - 1]) * _LOG2E).T

    def vec(offset):
        dist = delta + offset
        vals = rel[:, jnp.clip(dist, 0, n_dist - 1)]
        return jnp.where(dist >= 0, vals, -jnp.inf)

    g_vec = jnp.stack([vec(tk), vec(0), vec(-tk)], axis=1)
    g_vec = g_vec.reshape(N_DIFF_HEADS, 2, 3, period).transpose(0, 2, 1, 3)
    return _toeplitz_pair(g_vec, tk, tq)


def _lambda_init_for(layer_idx):
    return 0.8 - 0.6 * math.exp(-0.3 * layer_idx)


def kernel(x, norm_g, w_in, w_out, conv_w, lambda_q1, lambda_k1, lambda_q2, lambda_k2, subln_g,
           rel_bias, final_g):
    bsz, seq, d = x.shape
    depth = norm_g.shape[0]
    w = w_out.shape[1]
    tq, tk = ATTN_Q_TILE, ATTN_K_TILE
    assert seq % tq == 0 and seq % ROW_TILE == 0 and w == N_DIFF_HEADS * V_DIM
    assert N_DIFF_HEADS % ATTN_HEADS_PER_STEP == 0
    assert depth % N_MIXERS == 0

    bias_tiles = _bias_tiles(rel_bias, tq, tk)
    x2 = x.reshape(bsz * seq, d)
    for i in range(depth):
        j = i // N_MIXERS
        if i % N_MIXERS == 0:
            x2 = _conv_layer(x2, norm_g[i], w_in[i], w_out[i], conv_w[j], seq)
        else:
            w_in_bf = w_in[i].astype(_BF16)
            q, k, vt = _qkv_proj(x2, norm_g[i], w_in_bf[:, :3 * w])
            lam_params = jnp.stack([lambda_q1[j], lambda_k1[j], lambda_q2[j], lambda_k2[j]])
            o = _attention(q, k, vt, bias_tiles, lam_params, subln_g[j], _lambda_init_for(i),
                           bsz)
            x2 = _attn_out(x2, o.reshape(bsz * seq, w), norm_g[i], w_in_bf[:, 3 * w:],
                           w_out[i].astype(_BF16), final_g, final_norm=(i == depth - 1))
    return x2.reshape(bsz, seq, d)
```

```python
import functools
import math

import jax
import jax.numpy as jnp
from jax import lax
from jax.experimental import pallas as pl
from jax.experimental.pallas import tpu as pltpu

N_MIXERS = 2
CONV_K = 3
N_DIFF_HEADS = 8
HEAD_DIM = 64
V_DIM = 2 * HEAD_DIM
N_MAPS = 2 * N_DIFF_HEADS
NUM_BUCKETS = 32
MAX_EXACT = NUM_BUCKETS // 2
REL_MAX_DIST = 128
RMS_EPS = 1e-6

SUBLANES = 8
BF16_SUBLANES = 16
ROW_TILE = 512
ATTN_Q_TILE = 512
ATTN_K_TILE = 256
ATTN_HEADS_PER_STEP = 2
VT_ROWS = V_DIM + BF16_SUBLANES
VMEM_LIMIT_BYTES = 56 * 1024 * 1024

assert ATTN_Q_TILE == 2 * ATTN_K_TILE and ATTN_K_TILE >= REL_MAX_DIST
assert ROW_TILE % ATTN_K_TILE == 0

_F32 = jnp.float32
_BF16 = jnp.bfloat16
_LOG2E = math.log2(math.e)


def _dot(a, b):
    return jnp.dot(a, b, preferred_element_type=_F32)


def _rmsnorm_rows(x, g):
    return x * lax.rsqrt(jnp.mean(x * x, axis=-1, keepdims=True) + RMS_EPS) * g


def _const_spec(shape):
    return pl.BlockSpec(shape, lambda *_: (0,) * len(shape), pipeline_mode=pl.Buffered(1))


def _params(semantics):
    return pltpu.CompilerParams(dimension_semantics=semantics,
                                vmem_limit_bytes=VMEM_LIMIT_BYTES)


def _shift_rows(v, prev, s):
    r = pltpu.roll(v, s, 0)
    p = pltpu.roll(prev, s, 0)
    row = lax.broadcasted_iota(jnp.int32, prev.shape, 0)
    head = jnp.where(row < s, p, r[:SUBLANES])
    return jnp.concatenate([head, r[SUBLANES:]], axis=0)


def _conv_layer_kernel(x_ref, g_ref, win_ref, wout_ref, cw_ref, o_ref, carry_ref, *,
                       tiles_per_seq, width):
    w = width
    x = x_ref[...]
    h = _rmsnorm_rows(x, g_ref[...]).astype(_BF16)
    b_gate = _dot(h, win_ref[:, 0:w])
    c_gate = _dot(h, win_ref[:, w:2 * w])
    u = _dot(h, win_ref[:, 2 * w:3 * w])
    z = _dot(h, win_ref[:, 3 * w:4 * w])
    v = c_gate * u

    @pl.when(pl.program_id(0) % tiles_per_seq == 0)
    def _():
        carry_ref[...] = jnp.zeros_like(carry_ref)

    prev = carry_ref[...]
    conv = (_shift_rows(v, prev, 2) * cw_ref[0:1, :] + _shift_rows(v, prev, 1) * cw_ref[1:2, :]
            + v * cw_ref[2:3, :])
    carry_ref[...] = v[v.shape[0] - SUBLANES:]
    y = (b_gate * conv * jax.nn.silu(z)).astype(_BF16)
    o_ref[...] = x + _dot(y, wout_ref[...])


def _conv_layer(x2, g, w_in, w_out, conv_w, seq):
    n, d = x2.shape
    w = w_out.shape[0]
    tm = ROW_TILE
    return pl.pallas_call(
        functools.partial(_conv_layer_kernel, tiles_per_seq=seq // tm, width=w),
        grid=(n // tm,),
        in_specs=[
            pl.BlockSpec((tm, d), lambda i: (i, 0)),
            _const_spec((1, d)),
            _const_spec((d, 4 * w)),
            _const_spec((w, d)),
            _const_spec((CONV_K, w)),
        ],
        out_specs=pl.BlockSpec((tm, d), lambda i: (i, 0)),
        out_shape=jax.ShapeDtypeStruct((n, d), _F32),
        scratch_shapes=[pltpu.VMEM((SUBLANES, w), _F32)],
        compiler_params=_params(("arbitrary",)),
        name="conv_layer",
    )(x2, g.reshape(1, d), w_in.astype(_BF16), w_out.astype(_BF16), conv_w.T)


def _qkv_kernel(x_ref, g_ref, w_ref, q_ref, k_ref, vt_ref, *, width):
    w = width
    h = _rmsnorm_rows(x_ref[...], g_ref[...]).astype(_BF16)
    q = (_dot(h, w_ref[:, 0:w]) * (HEAD_DIM ** -0.5 * _LOG2E)).astype(_BF16)
    k = _dot(h, w_ref[:, w:2 * w]).astype(_BF16)
    v = _dot(h, w_ref[:, 2 * w:3 * w])
    n_tiles, n_heads, vt_rows, tk = vt_ref.shape
    row = lax.broadcasted_iota(jnp.int32, (vt_rows - V_DIM, tk), 0)
    ones_rows = jnp.where(row == 0, 1.0, 0.0).astype(_BF16)
    for t in range(n_tiles):
        rows = slice(t * tk, (t + 1) * tk)
        for hd in range(n_heads):
            cols = slice(hd * V_DIM, (hd + 1) * V_DIM)
            q_ref[t, hd] = q[rows, cols]
            k_ref[t, hd] = k[rows, cols]
            vt_ref[t, hd, 0:V_DIM, :] = v[rows, cols].T.astype(_BF16)
            vt_ref[t, hd, V_DIM:, :] = ones_rows


def _qkv_proj(x2, g, w_qkv):
    n, d = x2.shape
    w = w_qkv.shape[1] // 3
    tm, tk = ROW_TILE, ATTN_K_TILE
    n_heads = w // V_DIM
    out = jax.ShapeDtypeStruct((n // tk, n_heads, tk, V_DIM), _BF16)
    head_tiles = pl.BlockSpec((tm // tk, n_heads, tk, V_DIM), lambda i: (i, 0, 0, 0))
    return pl.pallas_call(
        functools.partial(_qkv_kernel, width=w),
        grid=(n // tm,),
        in_specs=[
            pl.BlockSpec((tm, d), lambda i: (i, 0)),
            _const_spec((1, d)),
            _const_spec((d, 3 * w)),
        ],
        out_specs=[head_tiles, head_tiles,
                   pl.BlockSpec((tm // tk, n_heads, VT_ROWS, tk), lambda i: (i, 0, 0, 0))],
        out_shape=[out, out, jax.ShapeDtypeStruct((n // tk, n_heads, VT_ROWS, tk), _BF16)],
        compiler_params=_params(("arbitrary",)),
        name="qkv_proj",
    )(x2, g.reshape(1, d), w_qkv)


def _attn_kernel(q_ref, k_ref, vt_ref, bias_ref, lam_ref, sg_ref, o_ref, *scratch, lambda_init):
    n_heads = q_ref.shape[1]
    per_head = len(scratch) // n_heads
    heads = tuple(scratch[h * per_head:(h + 1) * per_head] for h in range(n_heads))
    tk = vt_ref.shape[-1]
    tq = q_ref.shape[0] * tk
    qi = pl.program_id(2)

    for h in range(n_heads):
        qq_sc, m_sc, acc_sc = heads[h][:3]
        q = q_ref[:, h].reshape(tq, V_DIM)
        lane = lax.broadcasted_iota(jnp.int32, q.shape, 1)
        zero = jnp.zeros_like(q)
        qq_sc[0:tq, :] = jnp.where(lane < HEAD_DIM, q, zero)
        qq_sc[tq:2 * tq, :] = jnp.where(lane >= HEAD_DIM, q, zero)
        m_sc[...] = jnp.full_like(m_sc, -jnp.inf)
        acc_sc[...] = jnp.zeros_like(acc_sc)

    def stage_a(u, biases):
        for h in range(n_heads):
            qq_sc, _, _, s_sc, t_sc, _, _ = heads[h]
            keys = k_ref[pl.ds(2 * u, 2), h].reshape(2 * tk, V_DIM)
            s = lax.dot_general(keys, qq_sc[...], (((1,), (1,)), ((), ())),
                                preferred_element_type=_F32)
            col_max = None
            for half, bias in enumerate(biases):
                tile = s[half * tk:(half + 1) * tk]
                if bias is not None:
                    tile = tile + bias_ref[h, bias]
                s_sc[half * tk:(half + 1) * tk, :] = tile
                tile_max = jnp.max(tile, axis=0, keepdims=True)
                col_max = tile_max if col_max is None else jnp.maximum(col_max, tile_max)
            t_sc[...] = col_max

    def stage_b():
        for h in range(n_heads):
            _, m_sc, _, s_sc, t_sc, p_sc, a_sc = heads[h]
            m_prev = m_sc[...]
            m_new = jnp.maximum(m_prev, t_sc[...])
            a_sc[...] = jnp.exp2(m_prev - m_new)
            p_sc[...] = jnp.exp2(s_sc[...] - m_new).astype(_BF16)
            m_sc[...] = m_new

    def stage_c(u):
        for h in range(n_heads):
            _, _, acc_sc, _, _, p_sc, a_sc = heads[h]
            vt = jnp.concatenate([vt_ref[2 * u, h], vt_ref[2 * u + 1, h]], axis=1)
            acc_sc[...] = acc_sc[...] * a_sc[...] + _dot(vt, p_sc[...])

    def trip(u, biases=(None, None), issue_a=True, issue_c=True):
        if issue_c:
            stage_c(u - 1)
        stage_b()
        if issue_a:
            stage_a(u + 1, biases)

    prev, diag0, diag1 = 0, 1, 2
    before_diag, on_diag = (None, prev), (diag0, diag1)
    short = ([on_diag], [before_diag, on_diag], [(None, None), before_diag, on_diag])

    for n_q, step_biases in enumerate(short):
        @pl.when(qi == n_q)
        def _(step_biases=step_biases):
            for u, biases in enumerate(step_biases):
                stage_a(u, biases)
                stage_b()
                stage_c(u)

    @pl.when(qi >= len(short))
    def _():
        stage_a(0, (None, None))
        trip(0, issue_c=False)

        def plain(u, carry):
            trip(u)
            return carry

        lax.fori_loop(1, qi - 2, plain, 0)
        trip(qi - 2, before_diag)
        trip(qi - 1, on_diag)
        trip(qi, issue_a=False)
        stage_c(qi)

    lam = (jnp.exp(jnp.sum(lam_ref[0:1, :] * lam_ref[1:2, :], axis=-1, keepdims=True))
           - jnp.exp(jnp.sum(lam_ref[2:3, :] * lam_ref[3:4, :], axis=-1, keepdims=True))
           + lambda_init)
    for h in range(n_heads):
        acc_sc = heads[h][2]
        o_maps = acc_sc[0:V_DIM, :] / acc_sc[V_DIM:V_DIM + 1, :]
        o_t = o_maps[:, :tq] - lam * o_maps[:, tq:]
        o_t = o_t * lax.rsqrt(jnp.mean(o_t * o_t, axis=0, keepdims=True) + RMS_EPS)
        o_ref[:, h * V_DIM:(h + 1) * V_DIM] = (
            o_t.T * sg_ref[...] * (1.0 - lambda_init)).astype(o_ref.dtype)


def _attention(q, k, vt, bias_tiles, lam_params, subln_g, lambda_init, bsz):
    tq, tk, hps = ATTN_Q_TILE, ATTN_K_TILE, ATTN_HEADS_PER_STEP
    n_heads = q.shape[1]
    s = q.shape[0] * tk // bsz
    w = n_heads * V_DIM
    q_tiles = s // tq
    col = pltpu.VMEM((1, 2 * tq), _F32)
    per_head_scratch = [
        pltpu.VMEM((2 * tq, V_DIM), _BF16),
        col,
        pltpu.VMEM((VT_ROWS, 2 * tq), _F32),
        pltpu.VMEM((2 * tk, 2 * tq), _F32),
        col,
        pltpu.VMEM((2 * tk, 2 * tq), _BF16),
        col,
    ]
    return pl.pallas_call(
        functools.partial(_attn_kernel, lambda_init=lambda_init),
        grid=(bsz, n_heads // hps, q_tiles),
        in_specs=[
            pl.BlockSpec((tq // tk, hps, tk, V_DIM), lambda b, h, i: (b * q_tiles + i, h, 0, 0)),
            pl.BlockSpec((s // tk, hps, tk, V_DIM), lambda b, h, i: (b, h, 0, 0)),
            pl.BlockSpec((s // tk, hps, VT_ROWS, tk), lambda b, h, i: (b, h, 0, 0)),
            pl.BlockSpec((hps, 3, tk, 2 * tq), lambda b, h, i: (h, 0, 0, 0)),
            _const_spec((4, HEAD_DIM)),
            _const_spec((1, V_DIM)),
        ],
        out_specs=pl.BlockSpec((None, tq, hps * V_DIM), lambda b, h, i: (b, i, h)),
        out_shape=jax.ShapeDtypeStruct((bsz, s, w), _BF16),
        scratch_shapes=per_head_scratch * hps,
        compiler_params=_params(("arbitrary", "arbitrary", "arbitrary")),
        name="diff_attention",
    )(q, k, vt, bias_tiles, lam_params, subln_g.reshape(1, V_DIM))


def _attn_out_kernel(x_ref, o_ref, g_ref, wz_ref, wout_ref, fg_ref, y_ref, *, final_norm):
    x = x_ref[...]
    h = _rmsnorm_rows(x, g_ref[...]).astype(_BF16)
    z = _dot(h, wz_ref[...])
    y = (o_ref[...] * jax.nn.silu(z)).astype(_BF16)
    x_new = x + _dot(y, wout_ref[...])
    if final_norm:
        x_new = _rmsnorm_rows(x_new, fg_ref[...])
    y_ref[...] = x_new


def _attn_out(x2, o2, g, w_z, w_out, final_g, final_norm):
    n, d = x2.shape
    w = w_out.shape[0]
    tm = ROW_TILE
    return pl.pallas_call(
        functools.partial(_attn_out_kernel, final_norm=final_norm),
        grid=(n // tm,),
        in_specs=[
            pl.BlockSpec((tm, d), lambda i: (i, 0)),
            pl.BlockSpec((tm, w), lambda i: (i, 0)),
            _const_spec((1, d)),
            _const_spec((d, w)),
            _const_spec((w, d)),
            _const_spec((1, d)),
        ],
        out_specs=pl.BlockSpec((tm, d), lambda i: (i, 0)),
        out_shape=jax.ShapeDtypeStruct((n, d), _F32),
        compiler_params=_params(("arbitrary",)),
        name="attn_out",
    )(x2, o2, g.reshape(1, d), w_z, w_out, final_g.reshape(1, d))


def _t5_causal_bucket(dist):
    d_safe = jnp.maximum(dist, 1).astype(_F32)
    large = MAX_EXACT + (jnp.log(d_safe / MAX_EXACT) / math.log(REL_MAX_DIST / MAX_EXACT)
                         * (NUM_BUCKETS - MAX_EXACT)).astype(jnp.int32)
    large = jnp.minimum(large, NUM_BUCKETS - 1)
    return jnp.where(dist < MAX_EXACT, dist, large)


def _toeplitz_pair(g_vec, rows, cols):
    period = g_vec.shape[-1]
    base = jnp.stack([jnp.roll(g_vec, r, axis=-1) for r in range(SUBLANES)], axis=-2)
    groups = []
    for a in range(rows // SUBLANES):
        shift = a * SUBLANES
        pieces = []
        for m in range(2):
            bm = base[..., m, :, :]
            pieces += [bm[..., period - shift:], bm[..., :cols - shift]]
        groups.append(jnp.concatenate(pieces, axis=-1))
    out = jnp.stack(groups, axis=-3)
    return out.reshape(out.shape[:-3] + (rows, 2 * cols))


def _bias_tiles(rel_bias, tq, tk):
    period = 2 * (tq + tk)
    delta = jnp.arange(period, dtype=jnp.int32)
    delta = jnp.where(delta >= period // 2, delta - period, delta)
    n_dist = tq + tk
    rel = ((rel_bias[_t5_causal_bucket(jnp.arange(n_dist, dtype=jnp.int32))]
            - rel_bias[NUM_BUCKETS - 1]) * _LOG2E).T

    def vec(offset):
        dist = delta + offset
        vals = rel[:, jnp.clip(dist, 0, n_dist - 1)]
        return jnp.where(dist >= 0, vals, -jnp.inf)

    g_vec = jnp.stack([vec(tk), vec(0), vec(-tk)], axis=1)
    g_vec = g_vec.reshape(N_DIFF_HEADS, 2, 3, period).transpose(0, 2, 1, 3)
    return _toeplitz_pair(g_vec, tk, tq)


def _lambda_init_for(layer_idx):
    return 0.8 - 0.6 * math.exp(-0.3 * layer_idx)


def kernel(x, norm_g, w_in, w_out, conv_w, lambda_q1, lambda_k1, lambda_q2, lambda_k2, subln_g,
           rel_bias, final_g):
    bsz, seq, d = x.shape
    depth = norm_g.shape[0]
    w = w_out.shape[1]
    tq, tk = ATTN_Q_TILE, ATTN_K_TILE
    assert seq % tq == 0 and seq % ROW_TILE == 0 and w == N_DIFF_HEADS * V_DIM
    assert N_DIFF_HEADS % ATTN_HEADS_PER_STEP == 0
    assert depth % N_MIXERS == 0

    bias_tiles = _bias_tiles(rel_bias, tq, tk)
    x2 = x.reshape(bsz * seq, d)
    for i in range(depth):
        j = i // N_MIXERS
        if i % N_MIXERS == 0:
            x2 = _conv_layer(x2, norm_g[i], w_in[i], w_out[i], conv_w[j], seq)
        else:
            w_in_bf = w_in[i].astype(_BF16)
            q, k, vt = _qkv_proj(x2, norm_g[i], w_in_bf[:, :3 * w])
            lam_params = jnp.stack([lambda_q1[j], lambda_k1[j], lambda_q2[j], lambda_k2[j]])
            o = _attention(q, k, vt, bias_tiles, lam_params, subln_g[j], _lambda_init_for(i),
                           bsz)
            x2 = _attn_out(x2, o.reshape(bsz * seq, w), norm_g[i], w_in_bf[:, 3 * w:],
                           w_out[i].astype(_BF16), final_g, final_norm=(i == depth - 1))
    return x2.reshape(bsz, seq, d)
```

```python
import functools
import math

import jax
import jax.numpy as jnp
from jax import lax
from jax.experimental import pallas as pl
from jax.experimental.pallas import tpu as pltpu

N_MIXERS = 2
CONV_K = 3
N_DIFF_HEADS = 8
HEAD_DIM = 64
V_DIM = 2 * HEAD_DIM
N_MAPS = 2 * N_DIFF_HEADS
NUM_BUCKETS = 32
MAX_EXACT = NUM_BUCKETS // 2
REL_MAX_DIST = 128
RMS_EPS = 1e-6

SUBLANES = 8
BF16_SUBLANES = 16
ROW_TILE = 512
ATTN_ROW_TILE = 1024
ATTN_Q_TILE = 512
ATTN_K_TILE = 256
ATTN_HEADS_PER_STEP = 2
VT_ROWS = V_DIM + BF16_SUBLANES
VMEM_LIMIT_BYTES = 56 * 1024 * 1024

assert ATTN_Q_TILE == 2 * ATTN_K_TILE and ATTN_K_TILE >= REL_MAX_DIST
assert ATTN_ROW_TILE % ATTN_K_TILE == 0

_F32 = jnp.float32
_BF16 = jnp.bfloat16
_LOG2E = math.log2(math.e)


def _dot(a, b):
    return jnp.dot(a, b, preferred_element_type=_F32)


def _rmsnorm_rows(x, g):
    return x * lax.rsqrt(jnp.mean(x * x, axis=-1, keepdims=True) + RMS_EPS) * g


def _const_spec(shape):
    return pl.BlockSpec(shape, lambda *_: (0,) * len(shape), pipeline_mode=pl.Buffered(1))


def _params(semantics):
    return pltpu.CompilerParams(dimension_semantics=semantics,
                                vmem_limit_bytes=VMEM_LIMIT_BYTES)


def _shift_rows(v, prev, s):
    r = pltpu.roll(v, s, 0)
    p = pltpu.roll(prev, s, 0)
    row = lax.broadcasted_iota(jnp.int32, prev.shape, 0)
    head = jnp.where(row < s, p, r[:SUBLANES])
    return jnp.concatenate([head, r[SUBLANES:]], axis=0)


def _conv_layer_kernel(x_ref, g_ref, win_ref, wout_ref, cw_ref, o_ref, carry_ref, *,
                       tiles_per_seq, width):
    w = width
    x = x_ref[...]
    h = _rmsnorm_rows(x, g_ref[...]).astype(_BF16)
    b_gate = _dot(h, win_ref[:, 0:w])
    c_gate = _dot(h, win_ref[:, w:2 * w])
    u = _dot(h, win_ref[:, 2 * w:3 * w])
    z = _dot(h, win_ref[:, 3 * w:4 * w])
    v = c_gate * u

    @pl.when(pl.program_id(0) % tiles_per_seq == 0)
    def _():
        carry_ref[...] = jnp.zeros_like(carry_ref)

    prev = carry_ref[...]
    conv = (_shift_rows(v, prev, 2) * cw_ref[0:1, :] + _shift_rows(v, prev, 1) * cw_ref[1:2, :]
            + v * cw_ref[2:3, :])
    carry_ref[...] = v[v.shape[0] - SUBLANES:]
    y = (b_gate * conv * jax.nn.silu(z)).astype(_BF16)
    o_ref[...] = x + _dot(y, wout_ref[...])


def _conv_layer(x2, g, w_in, w_out, conv_w, seq):
    n, d = x2.shape
    w = w_out.shape[0]
    tm = ROW_TILE
    return pl.pallas_call(
        functools.partial(_conv_layer_kernel, tiles_per_seq=seq // tm, width=w),
        grid=(n // tm,),
        in_specs=[
            pl.BlockSpec((tm, d), lambda i: (i, 0)),
            _const_spec((1, d)),
            _const_spec((d, 4 * w)),
            _const_spec((w, d)),
            _const_spec((CONV_K, w)),
        ],
        out_specs=pl.BlockSpec((tm, d), lambda i: (i, 0)),
        out_shape=jax.ShapeDtypeStruct((n, d), _F32),
        scratch_shapes=[pltpu.VMEM((SUBLANES, w), _F32)],
        compiler_params=_params(("arbitrary",)),
        name="conv_layer",
    )(x2, g.reshape(1, d), w_in.astype(_BF16), w_out.astype(_BF16), conv_w.T)


def _qkv_kernel(x_ref, g_ref, w_ref, q_ref, k_ref, vt_ref, *, width):
    w = width
    h = _rmsnorm_rows(x_ref[...], g_ref[...]).astype(_BF16)
    q = (_dot(h, w_ref[:, 0:w]) * (HEAD_DIM ** -0.5 * _LOG2E)).astype(_BF16)
    k = _dot(h, w_ref[:, w:2 * w]).astype(_BF16)
    v = _dot(h, w_ref[:, 2 * w:3 * w])
    n_tiles, n_heads, vt_rows, tk = vt_ref.shape
    row = lax.broadcasted_iota(jnp.int32, (vt_rows - V_DIM, tk), 0)
    ones_rows = jnp.where(row == 0, 1.0, 0.0).astype(_BF16)
    for t in range(n_tiles):
        rows = slice(t * tk, (t + 1) * tk)
        for hd in range(n_heads):
            cols = slice(hd * V_DIM, (hd + 1) * V_DIM)
            q_ref[t, hd] = q[rows, cols]
            k_ref[t, hd] = k[rows, cols]
            vt_ref[t, hd, 0:V_DIM, :] = v[rows, cols].T.astype(_BF16)
            vt_ref[t, hd, V_DIM:, :] = ones_rows


def _qkv_proj(x2, g, w_qkv):
    n, d = x2.shape
    w = w_qkv.shape[1] // 3
    tm, tk = ATTN_ROW_TILE, ATTN_K_TILE
    n_heads = w // V_DIM
    out = jax.ShapeDtypeStruct((n // tk, n_heads, tk, V_DIM), _BF16)
    head_tiles = pl.BlockSpec((tm // tk, n_heads, tk, V_DIM), lambda i: (i, 0, 0, 0))
    return pl.pallas_call(
        functools.partial(_qkv_kernel, width=w),
        grid=(n // tm,),
        in_specs=[
            pl.BlockSpec((tm, d), lambda i: (i, 0)),
            _const_spec((1, d)),
            _const_spec((d, 3 * w)),
        ],
        out_specs=[head_tiles, head_tiles,
                   pl.BlockSpec((tm // tk, n_heads, VT_ROWS, tk), lambda i: (i, 0, 0, 0))],
        out_shape=[out, out, jax.ShapeDtypeStruct((n // tk, n_heads, VT_ROWS, tk), _BF16)],
        compiler_params=_params(("arbitrary",)),
        name="qkv_proj",
    )(x2, g.reshape(1, d), w_qkv)


def _attn_kernel(q_ref, k_ref, vt_ref, bias_ref, lam_ref, sg_ref, o_ref, *scratch, lambda_init):
    n_heads = q_ref.shape[1]
    per_head = len(scratch) // n_heads
    heads = tuple(scratch[h * per_head:(h + 1) * per_head] for h in range(n_heads))
    tk = vt_ref.shape[-1]
    tq = q_ref.shape[0] * tk
    qi = pl.program_id(2)

    for h in range(n_heads):
        qq_sc, m_sc, acc_sc = heads[h][:3]
        q = q_ref[:, h].reshape(tq, V_DIM)
        lane = lax.broadcasted_iota(jnp.int32, q.shape, 1)
        zero = jnp.zeros_like(q)
        qq_sc[0:tq, :] = jnp.where(lane < HEAD_DIM, q, zero)
        qq_sc[tq:2 * tq, :] = jnp.where(lane >= HEAD_DIM, q, zero)
        m_sc[...] = jnp.full_like(m_sc, -jnp.inf)
        acc_sc[...] = jnp.zeros_like(acc_sc)

    def stage_a(u, biases):
        for h in range(n_heads):
            qq_sc, _, _, s_sc, t_sc, _, _ = heads[h]
            keys = k_ref[pl.ds(2 * u, 2), h].reshape(2 * tk, V_DIM)
            s = lax.dot_general(keys, qq_sc[...], (((1,), (1,)), ((), ())),
                                preferred_element_type=_F32)
            col_max = None
            for half, bias in enumerate(biases):
                tile = s[half * tk:(half + 1) * tk]
                if bias is not None:
                    tile = tile + bias_ref[h, bias]
                s_sc[half * tk:(half + 1) * tk, :] = tile
                tile_max = jnp.max(tile, axis=0, keepdims=True)
                col_max = tile_max if col_max is None else jnp.maximum(col_max, tile_max)
            t_sc[...] = col_max

    def stage_b():
        for h in range(n_heads):
            _, m_sc, _, s_sc, t_sc, p_sc, a_sc = heads[h]
            m_prev = m_sc[...]
            m_new = jnp.maximum(m_prev, t_sc[...])
            a_sc[...] = jnp.exp2(m_prev - m_new)
            p_sc[...] = jnp.exp2(s_sc[...] - m_new).astype(_BF16)
            m_sc[...] = m_new

    def stage_c(u):
        for h in range(n_heads):
            _, _, acc_sc, _, _, p_sc, a_sc = heads[h]
            vt = jnp.concatenate([vt_ref[2 * u, h], vt_ref[2 * u + 1, h]], axis=1)
            acc_sc[...] = acc_sc[...] * a_sc[...] + _dot(vt, p_sc[...])

    def trip(u, biases=(None, None), issue_a=True, issue_c=True):
        if issue_c:
            stage_c(u - 1)
        stage_b()
        if issue_a:
            stage_a(u + 1, biases)

    prev, diag0, diag1 = 0, 1, 2
    before_diag, on_diag = (None, prev), (diag0, diag1)
    short = ([on_diag], [before_diag, on_diag], [(None, None), before_diag, on_diag])

    for n_q, step_biases in enumerate(short):
        @pl.when(qi == n_q)
        def _(step_biases=step_biases):
            for u, biases in enumerate(step_biases):
                stage_a(u, biases)
                stage_b()
                stage_c(u)

    @pl.when(qi >= len(short))
    def _():
        stage_a(0, (None, None))
        trip(0, issue_c=False)

        def plain(u, carry):
            trip(u)
            return carry

        def plain_twice(v, carry):
            trip(2 * v + 1)
            trip(2 * v + 2)
            return carry

        n_double = (qi - 3) // 2
        lax.fori_loop(0, n_double, plain_twice, 0)
        lax.fori_loop(2 * n_double + 1, qi - 2, plain, 0)
        trip(qi - 2, before_diag)
        trip(qi - 1, on_diag)
        trip(qi, issue_a=False)
        stage_c(qi)

    lam = (jnp.exp(jnp.sum(lam_ref[0:1, :] * lam_ref[1:2, :], axis=-1, keepdims=True))
           - jnp.exp(jnp.sum(lam_ref[2:3, :] * lam_ref[3:4, :], axis=-1, keepdims=True))
           + lambda_init)
    for h in range(n_heads):
        acc_sc = heads[h][2]
        o_maps = acc_sc[0:V_DIM, :] / acc_sc[V_DIM:V_DIM + 1, :]
        o_t = o_maps[:, :tq] - lam * o_maps[:, tq:]
        o_t = o_t * lax.rsqrt(jnp.mean(o_t * o_t, axis=0, keepdims=True) + RMS_EPS)
        o_ref[:, h * V_DIM:(h + 1) * V_DIM] = (
            o_t.T * sg_ref[...] * (1.0 - lambda_init)).astype(o_ref.dtype)


def _attention(q, k, vt, bias_tiles, lam_params, subln_g, lambda_init, bsz):
    tq, tk, hps = ATTN_Q_TILE, ATTN_K_TILE, ATTN_HEADS_PER_STEP
    n_heads = q.shape[1]
    s = q.shape[0] * tk // bsz
    w = n_heads * V_DIM
    q_tiles = s // tq
    col = pltpu.VMEM((1, 2 * tq), _F32)
    per_head_scratch = [
        pltpu.VMEM((2 * tq, V_DIM), _BF16),
        col,
        pltpu.VMEM((VT_ROWS, 2 * tq), _F32),
        pltpu.VMEM((2 * tk, 2 * tq), _F32),
        col,
        pltpu.VMEM((2 * tk, 2 * tq), _BF16),
        col,
    ]
    return pl.pallas_call(
        functools.partial(_attn_kernel, lambda_init=lambda_init),
        grid=(bsz, n_heads // hps, q_tiles),
        in_specs=[
            pl.BlockSpec((tq // tk, hps, tk, V_DIM), lambda b, h, i: (b * q_tiles + i, h, 0, 0)),
            pl.BlockSpec((s // tk, hps, tk, V_DIM), lambda b, h, i: (b, h, 0, 0)),
            pl.BlockSpec((s // tk, hps, VT_ROWS, tk), lambda b, h, i: (b, h, 0, 0)),
            pl.BlockSpec((hps, 3, tk, 2 * tq), lambda b, h, i: (h, 0, 0, 0)),
            _const_spec((4, HEAD_DIM)),
            _const_spec((1, V_DIM)),
        ],
        out_specs=pl.BlockSpec((None, tq, hps * V_DIM), lambda b, h, i: (b, i, h)),
        out_shape=jax.ShapeDtypeStruct((bsz, s, w), _BF16),
        scratch_shapes=per_head_scratch * hps,
        compiler_params=_params(("arbitrary", "arbitrary", "arbitrary")),
        name="diff_attention",
    )(q, k, vt, bias_tiles, lam_params, subln_g.reshape(1, V_DIM))


def _attn_out_kernel(x_ref, o_ref, g_ref, wz_ref, wout_ref, fg_ref, y_ref, *, final_norm):
    x = x_ref[...]
    h = _rmsnorm_rows(x, g_ref[...]).astype(_BF16)
    z = _dot(h, wz_ref[...])
    y = (o_ref[...] * jax.nn.silu(z)).astype(_BF16)
    x_new = x + _dot(y, wout_ref[...])
    if final_norm:
        x_new = _rmsnorm_rows(x_new, fg_ref[...])
    y_ref[...] = x_new


def _attn_out(x2, o2, g, w_z, w_out, final_g, final_norm):
    n, d = x2.shape
    w = w_out.shape[0]
    tm = ATTN_ROW_TILE
    return pl.pallas_call(
        functools.partial(_attn_out_kernel, final_norm=final_norm),
        grid=(n // tm,),
        in_specs=[
            pl.BlockSpec((tm, d), lambda i: (i, 0)),
            pl.BlockSpec((tm, w), lambda i: (i, 0)),
            _const_spec((1, d)),
            _const_spec((d, w)),
            _const_spec((w, d)),
            _const_spec((1, d)),
        ],
        out_specs=pl.BlockSpec((tm, d), lambda i: (i, 0)),
        out_shape=jax.ShapeDtypeStruct((n, d), _F32),
        compiler_params=_params(("arbitrary",)),
        name="attn_out",
    )(x2, o2, g.reshape(1, d), w_z, w_out, final_g.reshape(1, d))


def _t5_causal_bucket(dist):
    d_safe = jnp.maximum(dist, 1).astype(_F32)
    large = MAX_EXACT + (jnp.log(d_safe / MAX_EXACT) / math.log(REL_MAX_DIST / MAX_EXACT)
                         * (NUM_BUCKETS - MAX_EXACT)).astype(jnp.int32)
    large = jnp.minimum(large, NUM_BUCKETS - 1)
    return jnp.where(dist < MAX_EXACT, dist, large)


def _toeplitz_pair(g_vec, rows, cols):
    period = g_vec.shape[-1]
    base = jnp.stack([jnp.roll(g_vec, r, axis=-1) for r in range(SUBLANES)], axis=-2)
    groups = []
    for a in range(rows // SUBLANES):
        shift = a * SUBLANES
        pieces = []
        for m in range(2):
            bm = base[..., m, :, :]
            pieces += [bm[..., period - shift:], bm[..., :cols - shift]]
        groups.append(jnp.concatenate(pieces, axis=-1))
    out = jnp.stack(groups, axis=-3)
    return out.reshape(out.shape[:-3] + (rows, 2 * cols))


def _bias_tiles(rel_bias, tq, tk):
    period = 2 * (tq + tk)
    delta = jnp.arange(period, dtype=jnp.int32)
    delta = jnp.where(delta >= period // 2, delta - period, delta)
    n_dist = tq + tk
    rel = ((rel_bias[_t5_causal_bucket(jnp.arange(n_dist, dtype=jnp.int32))]
            - rel_bias[NUM_BUCKETS - 1]) * _LOG2E).T

    def vec(offset):
        dist = delta + offset
        vals = rel[:, jnp.clip(dist, 0, n_dist - 1)]
        return jnp.where(dist >= 0, vals, -jnp.inf)

    g_vec = jnp.stack([vec(tk), vec(0), vec(-tk)], axis=1)
    g_vec = g_vec.reshape(N_DIFF_HEADS, 2, 3, period).transpose(0, 2, 1, 3)
    return _toeplitz_pair(g_vec, tk, tq)


def _lambda_init_for(layer_idx):
    return 0.8 - 0.6 * math.exp(-0.3 * layer_idx)


def kernel(x, norm_g, w_in, w_out, conv_w, lambda_q1, lambda_k1, lambda_q2, lambda_k2, subln_g,
           rel_bias, final_g):
    bsz, seq, d = x.shape
    depth = norm_g.shape[0]
    w = w_out.shape[1]
    tq, tk = ATTN_Q_TILE, ATTN_K_TILE
    assert seq % tq == 0 and seq % ROW_TILE == 0 and w == N_DIFF_HEADS * V_DIM
    assert (bsz * seq) % ATTN_ROW_TILE == 0
    assert N_DIFF_HEADS % ATTN_HEADS_PER_STEP == 0
    assert depth % N_MIXERS == 0

    bias_tiles = _bias_tiles(rel_bias, tq, tk)
    x2 = x.reshape(bsz * seq, d)
    for i in range(depth):
        j = i // N_MIXERS
        if i % N_MIXERS == 0:
            x2 = _conv_layer(x2, norm_g[i], w_in[i], w_out[i], conv_w[j], seq)
        else:
            w_in_bf = w_in[i].astype(_BF16)
            q, k, vt = _qkv_proj(x2, norm_g[i], w_in_bf[:, :3 * w])
            lam_params = jnp.stack([lambda_q1[j], lambda_k1[j], lambda_q2[j], lambda_k2[j]])
            o = _attention(q, k, vt, bias_tiles, lam_params, subln_g[j], _lambda_init_for(i),
                           bsz)
            x2 = _attn_out(x2, o.reshape(bsz * seq, w), norm_g[i], w_in_bf[:, 3 * w:],
                           w_out[i].astype(_BF16), final_g, final_norm=(i == depth - 1))
    return x2.reshape(bsz, seq, d)
```

```python
import functools
import math

import jax
import jax.numpy as jnp
from jax import lax
from jax.experimental import pallas as pl
from jax.experimental.pallas import tpu as pltpu

N_MIXERS = 2
CONV_K = 3
N_DIFF_HEADS = 8
HEAD_DIM = 64
V_DIM = 2 * HEAD_DIM
N_MAPS = 2 * N_DIFF_HEADS
NUM_BUCKETS = 32
MAX_EXACT = NUM_BUCKETS // 2
REL_MAX_DIST = 128
RMS_EPS = 1e-6

SUBLANES = 8
BF16_SUBLANES = 16
ROW_TILE = 512
ATTN_ROW_TILE = 1024
ATTN_Q_TILE = 512
ATTN_K_TILE = 256
ATTN_HEADS_PER_STEP = 2
VT_ROWS = V_DIM + BF16_SUBLANES
VMEM_LIMIT_BYTES = 56 * 1024 * 1024

assert ATTN_Q_TILE == 2 * ATTN_K_TILE and ATTN_K_TILE >= REL_MAX_DIST
assert ATTN_ROW_TILE % ATTN_K_TILE == 0

_F32 = jnp.float32
_BF16 = jnp.bfloat16
_LOG2E = math.log2(math.e)


def _dot(a, b):
    return jnp.dot(a, b, preferred_element_type=_F32)


def _rmsnorm_rows(x, g):
    return x * lax.rsqrt(jnp.mean(x * x, axis=-1, keepdims=True) + RMS_EPS) * g


def _const_spec(shape):
    return pl.BlockSpec(shape, lambda *_: (0,) * len(shape), pipeline_mode=pl.Buffered(1))


def _params(semantics):
    return pltpu.CompilerParams(dimension_semantics=semantics,
                                vmem_limit_bytes=VMEM_LIMIT_BYTES)


def _shift_rows(v, prev, s):
    r = pltpu.roll(v, s, 0)
    p = pltpu.roll(prev, s, 0)
    row = lax.broadcasted_iota(jnp.int32, prev.shape, 0)
    head = jnp.where(row < s, p, r[:SUBLANES])
    return jnp.concatenate([head, r[SUBLANES:]], axis=0)


def _conv_layer_kernel(x_ref, g_ref, win_ref, wout_ref, cw_ref, o_ref, carry_ref, *,
                       tiles_per_seq, width):
    w = width
    x = x_ref[...]
    h = _rmsnorm_rows(x, g_ref[...]).astype(_BF16)
    b_gate = _dot(h, win_ref[:, 0:w])
    c_gate = _dot(h, win_ref[:, w:2 * w])
    u = _dot(h, win_ref[:, 2 * w:3 * w])
    z = _dot(h, win_ref[:, 3 * w:4 * w])
    v = c_gate * u

    @pl.when(pl.program_id(0) % tiles_per_seq == 0)
    def _():
        carry_ref[...] = jnp.zeros_like(carry_ref)

    prev = carry_ref[...]
    conv = (_shift_rows(v, prev, 2) * cw_ref[0:1, :] + _shift_rows(v, prev, 1) * cw_ref[1:2, :]
            + v * cw_ref[2:3, :])
    carry_ref[...] = v[v.shape[0] - SUBLANES:]
    y = (b_gate * conv * jax.nn.silu(z)).astype(_BF16)
    o_ref[...] = x + _dot(y, wout_ref[...])


def _conv_layer(x2, g, w_in, w_out, conv_w, seq):
    n, d = x2.shape
    w = w_out.shape[0]
    tm = ROW_TILE
    return pl.pallas_call(
        functools.partial(_conv_layer_kernel, tiles_per_seq=seq // tm, width=w),
        grid=(n // tm,),
        in_specs=[
            pl.BlockSpec((tm, d), lambda i: (i, 0)),
            _const_spec((1, d)),
            _const_spec((d, 4 * w)),
            _const_spec((w, d)),
            _const_spec((CONV_K, w)),
        ],
        out_specs=pl.BlockSpec((tm, d), lambda i: (i, 0)),
        out_shape=jax.ShapeDtypeStruct((n, d), _F32),
        scratch_shapes=[pltpu.VMEM((SUBLANES, w), _F32)],
        compiler_params=_params(("arbitrary",)),
        name="conv_layer",
    )(x2, g.reshape(1, d), w_in.astype(_BF16), w_out.astype(_BF16), conv_w.T)


def _qkv_kernel(x_ref, g_ref, w_ref, q_ref, k_ref, vt_ref, *, width):
    w = width
    h = _rmsnorm_rows(x_ref[...], g_ref[...]).astype(_BF16)
    q = (_dot(h, w_ref[:, 0:w]) * (HEAD_DIM ** -0.5 * _LOG2E)).astype(_BF16)
    k = _dot(h, w_ref[:, w:2 * w]).astype(_BF16)
    v = _dot(h, w_ref[:, 2 * w:3 * w])
    n_tiles, n_heads, vt_rows, tk = vt_ref.shape
    row = lax.broadcasted_iota(jnp.int32, (vt_rows - V_DIM, tk), 0)
    ones_rows = jnp.where(row == 0, 1.0, 0.0).astype(_BF16)
    for t in range(n_tiles):
        rows = slice(t * tk, (t + 1) * tk)
        for hd in range(n_heads):
            cols = slice(hd * V_DIM, (hd + 1) * V_DIM)
            q_ref[t, hd] = q[rows, cols]
            k_ref[t, hd] = k[rows, cols]
            vt_ref[t, hd, 0:V_DIM, :] = v[rows, cols].T.astype(_BF16)
            vt_ref[t, hd, V_DIM:, :] = ones_rows


def _qkv_proj(x2, g, w_in):
    n, d = x2.shape
    w = w_in.shape[1] // 4
    tm, tk = ATTN_ROW_TILE, ATTN_K_TILE
    n_heads = w // V_DIM
    out = jax.ShapeDtypeStruct((n // tk, n_heads, tk, V_DIM), _BF16)
    head_tiles = pl.BlockSpec((tm // tk, n_heads, tk, V_DIM), lambda i: (i, 0, 0, 0))
    return pl.pallas_call(
        functools.partial(_qkv_kernel, width=w),
        grid=(n // tm,),
        in_specs=[
            pl.BlockSpec((tm, d), lambda i: (i, 0)),
            _const_spec((1, d)),
            _const_spec((d, 3 * w)),
        ],
        out_specs=[head_tiles, head_tiles,
                   pl.BlockSpec((tm // tk, n_heads, VT_ROWS, tk), lambda i: (i, 0, 0, 0))],
        out_shape=[out, out, jax.ShapeDtypeStruct((n // tk, n_heads, VT_ROWS, tk), _BF16)],
        compiler_params=_params(("arbitrary",)),
        name="qkv_proj",
    )(x2, g.reshape(1, d), w_in)


def _attn_kernel(q_ref, k_ref, vt_ref, bias_ref, lam_ref, sg_ref, o_ref, *scratch, lambda_init):
    n_heads = q_ref.shape[1]
    per_head = len(scratch) // n_heads
    heads = tuple(scratch[h * per_head:(h + 1) * per_head] for h in range(n_heads))
    tk = vt_ref.shape[-1]
    tq = q_ref.shape[0] * tk
    qi = pl.program_id(2)

    for h in range(n_heads):
        qq_sc, m_sc, acc_sc = heads[h][:3]
        q = q_ref[:, h].reshape(tq, V_DIM)
        lane = lax.broadcasted_iota(jnp.int32, q.shape, 1)
        zero = jnp.zeros_like(q)
        qq_sc[0:tq, :] = jnp.where(lane < HEAD_DIM, q, zero)
        qq_sc[tq:2 * tq, :] = jnp.where(lane >= HEAD_DIM, q, zero)
        m_sc[...] = jnp.full_like(m_sc, -jnp.inf)
        acc_sc[...] = jnp.zeros_like(acc_sc)

    prev, diag0, diag1 = 0, 1, 2

    def stage_a(u, biases):
        for h in range(n_heads):
            qq_sc, _, _, s_sc, t_sc, _, _ = heads[h]
            keys = k_ref[pl.ds(2 * u, 2), h].reshape(2 * tk, V_DIM)
            s = lax.dot_general(keys, qq_sc[...], (((1,), (1,)), ((), ())),
                                preferred_element_type=_F32)
            col_max = None
            for half, bias in enumerate(biases):
                tile = s[half * tk:(half + 1) * tk]
                if bias is not None:
                    tile = tile + bias_ref[h, bias]
                s_sc[half * tk:(half + 1) * tk, :] = tile
                tile_max = jnp.max(tile, axis=0, keepdims=True)
                col_max = tile_max if col_max is None else jnp.maximum(col_max, tile_max)
            t_sc[...] = col_max

    hq = tq // 2

    def late_queries(x):
        return jnp.concatenate([x[:, hq:tq], x[:, tq + hq:]], axis=1)

    def stage_a_diag(u):
        for h in range(n_heads):
            qq_sc, _, _, s_sc, t_sc, _, _ = heads[h]
            nt = (((1,), (1,)), ((), ()))
            s_lo = lax.dot_general(k_ref[2 * u, h], qq_sc[...], nt,
                                   preferred_element_type=_F32) + bias_ref[h, diag0]
            qq_late = jnp.concatenate([qq_sc[hq:tq, :], qq_sc[tq + hq:, :]], axis=0)
            s_hi = lax.dot_general(k_ref[2 * u + 1, h], qq_late, nt,
                                   preferred_element_type=_F32)
            s_hi = s_hi + late_queries(bias_ref[h, diag1])
            masked = jnp.full((tk, hq), -jnp.inf, _F32)
            s_sc[0:tk, :] = s_lo
            s_sc[tk:, :] = jnp.concatenate([masked, s_hi[:, :hq], masked, s_hi[:, hq:]], axis=1)
            max_lo = jnp.max(s_lo, axis=0, keepdims=True)
            max_hi = jnp.max(s_hi, axis=0, keepdims=True)
            t_sc[...] = jnp.concatenate(
                [max_lo[:, :hq], jnp.maximum(max_lo[:, hq:tq], max_hi[:, :hq]),
                 max_lo[:, tq:tq + hq], jnp.maximum(max_lo[:, tq + hq:], max_hi[:, hq:])], axis=1)

    def stage_b():
        for h in range(n_heads):
            _, m_sc, _, s_sc, t_sc, p_sc, a_sc = heads[h]
            m_prev = m_sc[...]
            m_new = jnp.maximum(m_prev, t_sc[...])
            a_sc[...] = jnp.exp2(m_prev - m_new)
            p_sc[...] = jnp.exp2(s_sc[...] - m_new).astype(_BF16)
            m_sc[...] = m_new

    def stage_c(u):
        for h in range(n_heads):
            _, _, acc_sc, _, _, p_sc, a_sc = heads[h]
            vt = jnp.concatenate([vt_ref[2 * u, h], vt_ref[2 * u + 1, h]], axis=1)
            acc_sc[...] = acc_sc[...] * a_sc[...] + _dot(vt, p_sc[...])

    def stage_c_diag(u):
        for h in range(n_heads):
            _, _, acc_sc, _, _, p_sc, a_sc = heads[h]
            pv = _dot(vt_ref[2 * u, h], p_sc[0:tk, :])
            pv_late = _dot(vt_ref[2 * u + 1, h], late_queries(p_sc[tk:, :]))
            pv = jnp.concatenate([pv[:, :hq], pv[:, hq:tq] + pv_late[:, :hq],
                                  pv[:, tq:tq + hq], pv[:, tq + hq:] + pv_late[:, hq:]], axis=1)
            acc_sc[...] = acc_sc[...] * a_sc[...] + pv

    def trip(u, biases=(None, None), issue_a=True, issue_c=True):
        if issue_c:
            stage_c(u - 1)
        stage_b()
        if issue_a and biases is None:
            stage_a_diag(u + 1)
        elif issue_a:
            stage_a(u + 1, biases)

    before_diag = (None, prev)
    short = ([], [before_diag], [(None, None), before_diag])

    for n_q, step_biases in enumerate(short):
        @pl.when(qi == n_q)
        def _(step_biases=step_biases):
            for u, biases in enumerate(step_biases):
                stage_a(u, biases)
                stage_b()
                stage_c(u)
            stage_a_diag(len(step_biases))
            stage_b()
            stage_c_diag(len(step_biases))

    @pl.when(qi >= len(short))
    def _():
        stage_a(0, (None, None))
        trip(0, issue_c=False)

        def plain(u, carry):
            trip(u)
            return carry

        lax.fori_loop(1, qi - 2, plain, 0)
        trip(qi - 2, before_diag)
        trip(qi - 1, None)
        trip(qi, issue_a=False)
        stage_c_diag(qi)

    lam = (jnp.exp(jnp.sum(lam_ref[0:1, :] * lam_ref[1:2, :], axis=-1, keepdims=True))
           - jnp.exp(jnp.sum(lam_ref[2:3, :] * lam_ref[3:4, :], axis=-1, keepdims=True))
           + lambda_init)
    for h in range(n_heads):
        acc_sc = heads[h][2]
        o_maps = acc_sc[0:V_DIM, :] / acc_sc[V_DIM:V_DIM + 1, :]
        o_t = o_maps[:, :tq] - lam * o_maps[:, tq:]
        o_t = o_t * lax.rsqrt(jnp.mean(o_t * o_t, axis=0, keepdims=True) + RMS_EPS)
        o_ref[:, h * V_DIM:(h + 1) * V_DIM] = (
            o_t.T * sg_ref[...] * (1.0 - lambda_init)).astype(o_ref.dtype)


def _attention(q, k, vt, bias_tiles, lam_params, subln_g, lambda_init, bsz):
    tq, tk, hps = ATTN_Q_TILE, ATTN_K_TILE, ATTN_HEADS_PER_STEP
    n_heads = q.shape[1]
    s = q.shape[0] * tk // bsz
    w = n_heads * V_DIM
    q_tiles = s // tq
    col = pltpu.VMEM((1, 2 * tq), _F32)
    per_head_scratch = [
        pltpu.VMEM((2 * tq, V_DIM), _BF16),
        col,
        pltpu.VMEM((VT_ROWS, 2 * tq), _F32),
        pltpu.VMEM((2 * tk, 2 * tq), _F32),
        col,
        pltpu.VMEM((2 * tk, 2 * tq), _BF16),
        col,
    ]
    return pl.pallas_call(
        functools.partial(_attn_kernel, lambda_init=lambda_init),
        grid=(bsz, n_heads // hps, q_tiles),
        in_specs=[
            pl.BlockSpec((tq // tk, hps, tk, V_DIM), lambda b, h, i: (b * q_tiles + i, h, 0, 0)),
            pl.BlockSpec((s // tk, hps, tk, V_DIM), lambda b, h, i: (b, h, 0, 0)),
            pl.BlockSpec((s // tk, hps, VT_ROWS, tk), lambda b, h, i: (b, h, 0, 0)),
            pl.BlockSpec((hps, 3, tk, 2 * tq), lambda b, h, i: (h, 0, 0, 0)),
            _const_spec((4, HEAD_DIM)),
            _const_spec((1, V_DIM)),
        ],
        out_specs=pl.BlockSpec((None, tq, hps * V_DIM), lambda b, h, i: (b, i, h)),
        out_shape=jax.ShapeDtypeStruct((bsz, s, w), _BF16),
        scratch_shapes=per_head_scratch * hps,
        compiler_params=_params(("arbitrary", "arbitrary", "arbitrary")),
        name="diff_attention",
    )(q, k, vt, bias_tiles, lam_params, subln_g.reshape(1, V_DIM))


def _attn_out_kernel(x_ref, o_ref, g_ref, wz_ref, wout_ref, fg_ref, y_ref, *, final_norm):
    x = x_ref[...]
    h = _rmsnorm_rows(x, g_ref[...]).astype(_BF16)
    z = _dot(h, wz_ref[...])
    y = (o_ref[...] * jax.nn.silu(z)).astype(_BF16)
    x_new = x + _dot(y, wout_ref[...])
    if final_norm:
        x_new = _rmsnorm_rows(x_new, fg_ref[...])
    y_ref[...] = x_new


def _attn_out(x2, o2, g, w_in, w_out, final_g, final_norm):
    n, d = x2.shape
    w = w_out.shape[0]
    tm = ATTN_ROW_TILE
    return pl.pallas_call(
        functools.partial(_attn_out_kernel, final_norm=final_norm),
        grid=(n // tm,),
        in_specs=[
            pl.BlockSpec((tm, d), lambda i: (i, 0)),
            pl.BlockSpec((tm, w), lambda i: (i, 0)),
            _const_spec((1, d)),
            pl.BlockSpec((d, w), lambda i: (0, 3), pipeline_mode=pl.Buffered(1)),
            _const_spec((w, d)),
            _const_spec((1, d)),
        ],
        out_specs=pl.BlockSpec((tm, d), lambda i: (i, 0)),
        out_shape=jax.ShapeDtypeStruct((n, d), _F32),
        compiler_params=_params(("arbitrary",)),
        name="attn_out",
    )(x2, o2, g.reshape(1, d), w_in, w_out, final_g.reshape(1, d))


def _t5_causal_bucket(dist):
    d_safe = jnp.maximum(dist, 1).astype(_F32)
    large = MAX_EXACT + (jnp.log(d_safe / MAX_EXACT) / math.log(REL_MAX_DIST / MAX_EXACT)
                         * (NUM_BUCKETS - MAX_EXACT)).astype(jnp.int32)
    large = jnp.minimum(large, NUM_BUCKETS - 1)
    return jnp.where(dist < MAX_EXACT, dist, large)


def _toeplitz_pair(g_vec, rows, cols):
    period = g_vec.shape[-1]
    base = jnp.stack([jnp.roll(g_vec, r, axis=-1) for r in range(SUBLANES)], axis=-2)
    groups = []
    for a in range(rows // SUBLANES):
        shift = a * SUBLANES
        pieces = []
        for m in range(2):
            bm = base[..., m, :, :]
            pieces += [bm[..., period - shift:], bm[..., :cols - shift]]
        groups.append(jnp.concatenate(pieces, axis=-1))
    out = jnp.stack(groups, axis=-3)
    return out.reshape(out.shape[:-3] + (rows, 2 * cols))


def _bias_tiles(rel_bias, tq, tk):
    period = 2 * (tq + tk)
    delta = jnp.arange(period, dtype=jnp.int32)
    delta = jnp.where(delta >= period // 2, delta - period, delta)
    n_dist = tq + tk
    rel = ((rel_bias[_t5_causal_bucket(jnp.arange(n_dist, dtype=jnp.int32))]
            - rel_bias[NUM_BUCKETS - 1]) * _LOG2E).T

    def vec(offset):
        dist = delta + offset
        vals = rel[:, jnp.clip(dist, 0, n_dist - 1)]
        return jnp.where(dist >= 0, vals, -jnp.inf)

    g_vec = jnp.stack([vec(tk), vec(0), vec(-tk)], axis=1)
    g_vec = g_vec.reshape(N_DIFF_HEADS, 2, 3, period).transpose(0, 2, 1, 3)
    return _toeplitz_pair(g_vec, tk, tq)


def _lambda_init_for(layer_idx):
    return 0.8 - 0.6 * math.exp(-0.3 * layer_idx)


def kernel(x, norm_g, w_in, w_out, conv_w, lambda_q1, lambda_k1, lambda_q2, lambda_k2, subln_g,
           rel_bias, final_g):
    bsz, seq, d = x.shape
    depth = norm_g.shape[0]
    w = w_out.shape[1]
    tq, tk = ATTN_Q_TILE, ATTN_K_TILE
    assert seq % tq == 0 and seq % ROW_TILE == 0 and w == N_DIFF_HEADS * V_DIM
    assert (bsz * seq) % ATTN_ROW_TILE == 0
    assert N_DIFF_HEADS % ATTN_HEADS_PER_STEP == 0
    assert depth % N_MIXERS == 0

    bias_tiles = _bias_tiles(rel_bias, tq, tk)
    x2 = x.reshape(bsz * seq, d)
    for i in range(depth):
        j = i // N_MIXERS
        if i % N_MIXERS == 0:
            x2 = _conv_layer(x2, norm_g[i], w_in[i], w_out[i], conv_w[j], seq)
        else:
            w_in_bf = w_in[i].astype(_BF16)
            q, k, vt = _qkv_proj(x2, norm_g[i], w_in_bf)
            lam_params = jnp.stack([lambda_q1[j], lambda_k1[j], lambda_q2[j], lambda_k2[j]])
            o = _attention(q, k, vt, bias_tiles, lam_params, subln_g[j], _lambda_init_for(i),
                           bsz)
            x2 = _attn_out(x2, o.reshape(bsz * seq, w), norm_g[i], w_in_bf,
                           w_out[i].astype(_BF16), final_g, final_norm=(i == depth - 1))
    return x2.reshape(bsz, seq, d)
```

```python
import functools
import math

import jax
import jax.numpy as jnp
from jax import lax
from jax.experimental import pallas as pl
from jax.experimental.pallas import tpu as pltpu

N_MIXERS = 2
CONV_K = 3
N_DIFF_HEADS = 8
HEAD_DIM = 64
V_DIM = 2 * HEAD_DIM
N_MAPS = 2 * N_DIFF_HEADS
NUM_BUCKETS = 32
MAX_EXACT = NUM_BUCKETS // 2
REL_MAX_DIST = 128
RMS_EPS = 1e-6

SUBLANES = 8
BF16_SUBLANES = 16
ROW_TILE = 512
CONV_SUB_ROWS = 256
ATTN_ROW_TILE = 1024
ATTN_Q_TILE = 512
ATTN_K_TILE = 256
ATTN_HEADS_PER_STEP = 2
VT_ROWS = V_DIM + BF16_SUBLANES
VMEM_LIMIT_BYTES = 56 * 1024 * 1024

assert ATTN_Q_TILE == 2 * ATTN_K_TILE and ATTN_K_TILE >= REL_MAX_DIST
assert ATTN_ROW_TILE % ATTN_K_TILE == 0

_F32 = jnp.float32
_BF16 = jnp.bfloat16
_LOG2E = math.log2(math.e)


def _dot(a, b):
    return jnp.dot(a, b, preferred_element_type=_F32)


def _rmsnorm_rows(x, g):
    return x * lax.rsqrt(jnp.mean(x * x, axis=-1, keepdims=True) + RMS_EPS) * g


def _const_spec(shape):
    return pl.BlockSpec(shape, lambda *_: (0,) * len(shape), pipeline_mode=pl.Buffered(1))


def _params(semantics):
    return pltpu.CompilerParams(dimension_semantics=semantics,
                                vmem_limit_bytes=VMEM_LIMIT_BYTES)


def _shift_rows(v, prev, s):
    r = pltpu.roll(v, s, 0)
    p = pltpu.roll(prev, s, 0)
    row = lax.broadcasted_iota(jnp.int32, prev.shape, 0)
    head = jnp.where(row < s, p, r[:SUBLANES])
    return jnp.concatenate([head, r[SUBLANES:]], axis=0)


def _conv_layer_kernel(x_ref, g_ref, win_ref, wout_ref, cw_ref, o_ref, carry_ref, *,
                       tiles_per_seq, width, sub_rows):
    w = width

    @pl.when(pl.program_id(0) % tiles_per_seq == 0)
    def _():
        carry_ref[...] = jnp.zeros_like(carry_ref)

    prev = carry_ref[...]
    for r in range(x_ref.shape[0] // sub_rows):
        rows = slice(r * sub_rows, (r + 1) * sub_rows)
        x = x_ref[rows, :]
        h = _rmsnorm_rows(x, g_ref[...]).astype(_BF16)
        b_gate = _dot(h, win_ref[:, 0:w])
        c_gate = _dot(h, win_ref[:, w:2 * w])
        u = _dot(h, win_ref[:, 2 * w:3 * w])
        z = _dot(h, win_ref[:, 3 * w:4 * w])
        v = c_gate * u
        conv = (_shift_rows(v, prev, 2) * cw_ref[0:1, :]
                + _shift_rows(v, prev, 1) * cw_ref[1:2, :] + v * cw_ref[2:3, :])
        prev = v[sub_rows - SUBLANES:]
        y = (b_gate * conv * jax.nn.silu(z)).astype(_BF16)
        o_ref[rows, :] = x + _dot(y, wout_ref[...])
    carry_ref[...] = prev


def _conv_layer(x2, g, w_in, w_out, conv_w, seq):
    n, d = x2.shape
    w = w_out.shape[0]
    tm = ROW_TILE
    return pl.pallas_call(
        functools.partial(_conv_layer_kernel, tiles_per_seq=seq // tm, width=w,
                          sub_rows=CONV_SUB_ROWS),
        grid=(n // tm,),
        in_specs=[
            pl.BlockSpec((tm, d), lambda i: (i, 0)),
            _const_spec((1, d)),
            _const_spec((d, 4 * w)),
            _const_spec((w, d)),
            _const_spec((CONV_K, w)),
        ],
        out_specs=pl.BlockSpec((tm, d), lambda i: (i, 0)),
        out_shape=jax.ShapeDtypeStruct((n, d), _F32),
        scratch_shapes=[pltpu.VMEM((SUBLANES, w), _F32)],
        compiler_params=_params(("arbitrary",)),
        name="conv_layer",
    )(x2, g.reshape(1, d), w_in.astype(_BF16), w_out.astype(_BF16), conv_w.T)


def _qkv_kernel(x_ref, g_ref, w_ref, q_ref, k_ref, vt_ref, *, width):
    w = width
    h = _rmsnorm_rows(x_ref[...], g_ref[...]).astype(_BF16)
    q = (_dot(h, w_ref[:, 0:w]) * (HEAD_DIM ** -0.5 * _LOG2E)).astype(_BF16)
    k = _dot(h, w_ref[:, w:2 * w]).astype(_BF16)
    v = _dot(h, w_ref[:, 2 * w:3 * w])
    n_tiles, n_heads, vt_rows, tk = vt_ref.shape
    row = lax.broadcasted_iota(jnp.int32, (vt_rows - V_DIM, tk), 0)
    ones_rows = jnp.where(row == 0, 1.0, 0.0).astype(_BF16)
    for t in range(n_tiles):
        rows = slice(t * tk, (t + 1) * tk)
        for hd in range(n_heads):
            cols = slice(hd * V_DIM, (hd + 1) * V_DIM)
            q_ref[t, hd] = q[rows, cols]
            k_ref[t, hd] = k[rows, cols]
            vt_ref[t, hd, 0:V_DIM, :] = v[rows, cols].T.astype(_BF16)
            vt_ref[t, hd, V_DIM:, :] = ones_rows


def _qkv_proj(x2, g, w_in):
    n, d = x2.shape
    w = w_in.shape[1] // 4
    tm, tk = ATTN_ROW_TILE, ATTN_K_TILE
    n_heads = w // V_DIM
    out = jax.ShapeDtypeStruct((n // tk, n_heads, tk, V_DIM), _BF16)
    head_tiles = pl.BlockSpec((tm // tk, n_heads, tk, V_DIM), lambda i: (i, 0, 0, 0))
    return pl.pallas_call(
        functools.partial(_qkv_kernel, width=w),
        grid=(n // tm,),
        in_specs=[
            pl.BlockSpec((tm, d), lambda i: (i, 0)),
            _const_spec((1, d)),
            _const_spec((d, 3 * w)),
        ],
        out_specs=[head_tiles, head_tiles,
                   pl.BlockSpec((tm // tk, n_heads, VT_ROWS, tk), lambda i: (i, 0, 0, 0))],
        out_shape=[out, out, jax.ShapeDtypeStruct((n // tk, n_heads, VT_ROWS, tk), _BF16)],
        compiler_params=_params(("arbitrary",)),
        name="qkv_proj",
    )(x2, g.reshape(1, d), w_in)


def _attn_kernel(q_ref, k_ref, vt_ref, bias_ref, lam_ref, sg_ref, o_ref, *scratch, lambda_init):
    n_heads = q_ref.shape[1]
    per_head = len(scratch) // n_heads
    heads = tuple(scratch[h * per_head:(h + 1) * per_head] for h in range(n_heads))
    tk = vt_ref.shape[-1]
    tq = q_ref.shape[0] * tk
    qi = pl.program_id(2)

    for h in range(n_heads):
        qq_sc, m_sc, acc_sc = heads[h][:3]
        q = q_ref[:, h].reshape(tq, V_DIM)
        lane = lax.broadcasted_iota(jnp.int32, q.shape, 1)
        zero = jnp.zeros_like(q)
        qq_sc[0:tq, :] = jnp.where(lane < HEAD_DIM, q, zero)
        qq_sc[tq:2 * tq, :] = jnp.where(lane >= HEAD_DIM, q, zero)
        m_sc[...] = jnp.full_like(m_sc, -jnp.inf)
        acc_sc[...] = jnp.zeros_like(acc_sc)

    prev, diag0, diag1 = 0, 1, 2

    def stage_a(u, biases):
        for h in range(n_heads):
            qq_sc, _, _, s_sc, t_sc, _, _ = heads[h]
            keys = k_ref[pl.ds(2 * u, 2), h].reshape(2 * tk, V_DIM)
            s = lax.dot_general(keys, qq_sc[...], (((1,), (1,)), ((), ())),
                                preferred_element_type=_F32)
            col_max = None
            for half, bias in enumerate(biases):
                tile = s[half * tk:(half + 1) * tk]
                if bias is not None:
                    tile = tile + bias_ref[h, bias]
                s_sc[half * tk:(half + 1) * tk, :] = tile
                tile_max = jnp.max(tile, axis=0, keepdims=True)
                col_max = tile_max if col_max is None else jnp.maximum(col_max, tile_max)
            t_sc[...] = col_max

    hq = tq // 2

    def late_queries(x):
        return jnp.concatenate([x[:, hq:tq], x[:, tq + hq:]], axis=1)

    def stage_a_diag(u):
        for h in range(n_heads):
            qq_sc, _, _, s_sc, t_sc, _, _ = heads[h]
            nt = (((1,), (1,)), ((), ()))
            s_lo = lax.dot_general(k_ref[2 * u, h], qq_sc[...], nt,
                                   preferred_element_type=_F32) + bias_ref[h, diag0]
            qq_late = jnp.concatenate([qq_sc[hq:tq, :], qq_sc[tq + hq:, :]], axis=0)
            s_hi = lax.dot_general(k_ref[2 * u + 1, h], qq_late, nt,
                                   preferred_element_type=_F32)
            s_hi = s_hi + late_queries(bias_ref[h, diag1])
            masked = jnp.full((tk, hq), -jnp.inf, _F32)
            s_sc[0:tk, :] = s_lo
            s_sc[tk:, :] = jnp.concatenate([masked, s_hi[:, :hq], masked, s_hi[:, hq:]], axis=1)
            max_lo = jnp.max(s_lo, axis=0, keepdims=True)
            max_hi = jnp.max(s_hi, axis=0, keepdims=True)
            t_sc[...] = jnp.concatenate(
                [max_lo[:, :hq], jnp.maximum(max_lo[:, hq:tq], max_hi[:, :hq]),
                 max_lo[:, tq:tq + hq], jnp.maximum(max_lo[:, tq + hq:], max_hi[:, hq:])], axis=1)

    def stage_b():
        for h in range(n_heads):
            _, m_sc, _, s_sc, t_sc, p_sc, a_sc = heads[h]
            m_prev = m_sc[...]
            m_new = jnp.maximum(m_prev, t_sc[...])
            a_sc[...] = jnp.exp2(m_prev - m_new)
            p_sc[...] = jnp.exp2(s_sc[...] - m_new).astype(_BF16)
            m_sc[...] = m_new

    def stage_c(u):
        for h in range(n_heads):
            _, _, acc_sc, _, _, p_sc, a_sc = heads[h]
            vt = jnp.concatenate([vt_ref[2 * u, h], vt_ref[2 * u + 1, h]], axis=1)
            acc_sc[...] = acc_sc[...] * a_sc[...] + _dot(vt, p_sc[...])

    def stage_c_diag(u):
        for h in range(n_heads):
            _, _, acc_sc, _, _, p_sc, a_sc = heads[h]
            pv = _dot(vt_ref[2 * u, h], p_sc[0:tk, :])
            pv_late = _dot(vt_ref[2 * u + 1, h], late_queries(p_sc[tk:, :]))
            pv = jnp.concatenate([pv[:, :hq], pv[:, hq:tq] + pv_late[:, :hq],
                                  pv[:, tq:tq + hq], pv[:, tq + hq:] + pv_late[:, hq:]], axis=1)
            acc_sc[...] = acc_sc[...] * a_sc[...] + pv

    def trip(u, biases=(None, None), issue_a=True, issue_c=True):
        if issue_c:
            stage_c(u - 1)
        stage_b()
        if issue_a and biases is None:
            stage_a_diag(u + 1)
        elif issue_a:
            stage_a(u + 1, biases)

    before_diag = (None, prev)
    short = ([], [before_diag], [(None, None), before_diag])

    for n_q, step_biases in enumerate(short):
        @pl.when(qi == n_q)
        def _(step_biases=step_biases):
            for u, biases in enumerate(step_biases):
                stage_a(u, biases)
                stage_b()
                stage_c(u)
            stage_a_diag(len(step_biases))
            stage_b()
            stage_c_diag(len(step_biases))

    @pl.when(qi >= len(short))
    def _():
        stage_a(0, (None, None))
        trip(0, issue_c=False)

        def plain(u, carry):
            trip(u)
            return carry

        lax.fori_loop(1, qi - 2, plain, 0)
        trip(qi - 2, before_diag)
        trip(qi - 1, None)
        trip(qi, issue_a=False)
        stage_c_diag(qi)

    lam = (jnp.exp(jnp.sum(lam_ref[0:1, :] * lam_ref[1:2, :], axis=-1, keepdims=True))
           - jnp.exp(jnp.sum(lam_ref[2:3, :] * lam_ref[3:4, :], axis=-1, keepdims=True))
           + lambda_init)
    for h in range(n_heads):
        acc_sc = heads[h][2]
        o_maps = acc_sc[0:V_DIM, :] * (1.0 / acc_sc[V_DIM:V_DIM + 1, :])
        o_t = o_maps[:, :tq] - lam * o_maps[:, tq:]
        o_t = o_t * lax.rsqrt(jnp.mean(o_t * o_t, axis=0, keepdims=True) + RMS_EPS)
        o_ref[:, h * V_DIM:(h + 1) * V_DIM] = (
            o_t.T * sg_ref[...] * (1.0 - lambda_init)).astype(o_ref.dtype)


def _attention(q, k, vt, bias_tiles, lam_params, subln_g, lambda_init, bsz):
    tq, tk, hps = ATTN_Q_TILE, ATTN_K_TILE, ATTN_HEADS_PER_STEP
    n_heads = q.shape[1]
    s = q.shape[0] * tk // bsz
    w = n_heads * V_DIM
    q_tiles = s // tq
    col = pltpu.VMEM((1, 2 * tq), _F32)
    per_head_scratch = [
        pltpu.VMEM((2 * tq, V_DIM), _BF16),
        col,
        pltpu.VMEM((VT_ROWS, 2 * tq), _F32),
        pltpu.VMEM((2 * tk, 2 * tq), _F32),
        col,
        pltpu.VMEM((2 * tk, 2 * tq), _BF16),
        col,
    ]
    return pl.pallas_call(
        functools.partial(_attn_kernel, lambda_init=lambda_init),
        grid=(bsz, n_heads // hps, q_tiles),
        in_specs=[
            pl.BlockSpec((tq // tk, hps, tk, V_DIM), lambda b, h, i: (b * q_tiles + i, h, 0, 0)),
            pl.BlockSpec((s // tk, hps, tk, V_DIM), lambda b, h, i: (b, h, 0, 0)),
            pl.BlockSpec((s // tk, hps, VT_ROWS, tk), lambda b, h, i: (b, h, 0, 0)),
            pl.BlockSpec((hps, 3, tk, 2 * tq), lambda b, h, i: (h, 0, 0, 0)),
            _const_spec((4, HEAD_DIM)),
            _const_spec((1, V_DIM)),
        ],
        out_specs=pl.BlockSpec((None, tq, hps * V_DIM), lambda b, h, i: (b, i, h)),
        out_shape=jax.ShapeDtypeStruct((bsz, s, w), _BF16),
        scratch_shapes=per_head_scratch * hps,
        compiler_params=_params(("arbitrary", "arbitrary", "arbitrary")),
        name="diff_attention",
    )(q, k, vt, bias_tiles, lam_params, subln_g.reshape(1, V_DIM))


def _attn_out_kernel(x_ref, o_ref, g_ref, wz_ref, wout_ref, fg_ref, y_ref, *, final_norm):
    x = x_ref[...]
    h = _rmsnorm_rows(x, g_ref[...]).astype(_BF16)
    z = _dot(h, wz_ref[...])
    y = (o_ref[...] * jax.nn.silu(z)).astype(_BF16)
    x_new = x + _dot(y, wout_ref[...])
    if final_norm:
        x_new = _rmsnorm_rows(x_new, fg_ref[...])
    y_ref[...] = x_new


def _attn_out(x2, o2, g, w_in, w_out, final_g, final_norm):
    n, d = x2.shape
    w = w_out.shape[0]
    tm = ATTN_ROW_TILE
    return pl.pallas_call(
        functools.partial(_attn_out_kernel, final_norm=final_norm),
        grid=(n // tm,),
        in_specs=[
            pl.BlockSpec((tm, d), lambda i: (i, 0)),
            pl.BlockSpec((tm, w), lambda i: (i, 0)),
            _const_spec((1, d)),
            pl.BlockSpec((d, w), lambda i: (0, 3), pipeline_mode=pl.Buffered(1)),
            _const_spec((w, d)),
            _const_spec((1, d)),
        ],
        out_specs=pl.BlockSpec((tm, d), lambda i: (i, 0)),
        out_shape=jax.ShapeDtypeStruct((n, d), _F32),
        compiler_params=_params(("arbitrary",)),
        name="attn_out",
    )(x2, o2, g.reshape(1, d), w_in, w_out, final_g.reshape(1, d))


def _t5_causal_bucket(dist):
    d_safe = jnp.maximum(dist, 1).astype(_F32)
    large = MAX_EXACT + (jnp.log(d_safe / MAX_EXACT) / math.log(REL_MAX_DIST / MAX_EXACT)
                         * (NUM_BUCKETS - MAX_EXACT)).astype(jnp.int32)
    large = jnp.minimum(large, NUM_BUCKETS - 1)
    return jnp.where(dist < MAX_EXACT, dist, large)


def _toeplitz_pair(g_vec, rows, cols):
    period = g_vec.shape[-1]
    base = jnp.stack([jnp.roll(g_vec, r, axis=-1) for r in range(SUBLANES)], axis=-2)
    groups = []
    for a in range(rows // SUBLANES):
        shift = a * SUBLANES
        pieces = []
        for m in range(2):
            bm = base[..., m, :, :]
            pieces += [bm[..., period - shift:], bm[..., :cols - shift]]
        groups.append(jnp.concatenate(pieces, axis=-1))
    out = jnp.stack(groups, axis=-3)
    return out.reshape(out.shape[:-3] + (rows, 2 * cols))


def _bias_tiles(rel_bias, tq, tk):
    period = 2 * (tq + tk)
    delta = jnp.arange(period, dtype=jnp.int32)
    delta = jnp.where(delta >= period // 2, delta - period, delta)
    n_dist = tq + tk
    rel = ((rel_bias[_t5_causal_bucket(jnp.arange(n_dist, dtype=jnp.int32))]
            - rel_bias[NUM_BUCKETS - 1]) * _LOG2E).T

    def vec(offset):
        dist = delta + offset
        vals = rel[:, jnp.clip(dist, 0, n_dist - 1)]
        return jnp.where(dist >= 0, vals, -jnp.inf)

    g_vec = jnp.stack([vec(tk), vec(0), vec(-tk)], axis=1)
    g_vec = g_vec.reshape(N_DIFF_HEADS, 2, 3, period).transpose(0, 2, 1, 3)
    return _toeplitz_pair(g_vec, tk, tq)


def _lambda_init_for(layer_idx):
    return 0.8 - 0.6 * math.exp(-0.3 * layer_idx)


def kernel(x, norm_g, w_in, w_out, conv_w, lambda_q1, lambda_k1, lambda_q2, lambda_k2, subln_g,
           rel_bias, final_g):
    bsz, seq, d = x.shape
    depth = norm_g.shape[0]
    w = w_out.shape[1]
    tq, tk = ATTN_Q_TILE, ATTN_K_TILE
    assert seq % tq == 0 and seq % ROW_TILE == 0 and w == N_DIFF_HEADS * V_DIM
    assert (bsz * seq) % ATTN_ROW_TILE == 0
    assert N_DIFF_HEADS % ATTN_HEADS_PER_STEP == 0
    assert depth % N_MIXERS == 0

    bias_tiles = _bias_tiles(rel_bias, tq, tk)
    x2 = x.reshape(bsz * seq, d)
    for i in range(depth):
        j = i // N_MIXERS
        if i % N_MIXERS == 0:
            x2 = _conv_layer(x2, norm_g[i], w_in[i], w_out[i], conv_w[j], seq)
        else:
            w_in_bf = w_in[i].astype(_BF16)
            q, k, vt = _qkv_proj(x2, norm_g[i], w_in_bf)
            lam_params = jnp.stack([lambda_q1[j], lambda_k1[j], lambda_q2[j], lambda_k2[j]])
            o = _attention(q, k, vt, bias_tiles, lam_params, subln_g[j], _lambda_init_for(i),
                           bsz)
            x2 = _attn_out(x2, o.reshape(bsz * seq, w), norm_g[i], w_in_bf,
                           w_out[i].astype(_BF16), final_g, final_norm=(i == depth - 1))
    return x2.reshape(bsz, seq, d)
```

```python
import functools
import math

import jax
import jax.numpy as jnp
from jax import lax
from jax.experimental import pallas as pl
from jax.experimental.pallas import tpu as pltpu

N_MIXERS = 2
CONV_K = 3
N_DIFF_HEADS = 8
HEAD_DIM = 64
V_DIM = 2 * HEAD_DIM
N_MAPS = 2 * N_DIFF_HEADS
NUM_BUCKETS = 32
MAX_EXACT = NUM_BUCKETS // 2
REL_MAX_DIST = 128
RMS_EPS = 1e-6

SUBLANES = 8
LANES = 128
BF16_SUBLANES = 16
ROW_TILE = 512
CONV_SUB_ROWS = 256
ATTN_ROW_TILE = 1024
ATTN_Q_TILE = 512
ATTN_K_TILE = 256
ATTN_HEADS_PER_STEP = 2
VT_ROWS = V_DIM + BF16_SUBLANES
VMEM_LIMIT_BYTES = 56 * 1024 * 1024

assert ATTN_Q_TILE == 2 * ATTN_K_TILE and ATTN_K_TILE >= REL_MAX_DIST
assert ATTN_ROW_TILE % ATTN_K_TILE == 0

_F32 = jnp.float32
_BF16 = jnp.bfloat16
_LOG2E = math.log2(math.e)


def _dot(a, b):
    return jnp.dot(a, b, preferred_element_type=_F32)


def _rmsnorm_rows(x, g):
    return x * lax.rsqrt(jnp.mean(x * x, axis=-1, keepdims=True) + RMS_EPS) * g


def _const_spec(shape):
    return pl.BlockSpec(shape, lambda *_: (0,) * len(shape), pipeline_mode=pl.Buffered(1))


def _params(semantics):
    return pltpu.CompilerParams(dimension_semantics=semantics,
                                vmem_limit_bytes=VMEM_LIMIT_BYTES)


def _shift_rows(v, prev, s):
    r = pltpu.roll(v, s, 0)
    p = pltpu.roll(prev, s, 0)
    row = lax.broadcasted_iota(jnp.int32, prev.shape, 0)
    head = jnp.where(row < s, p, r[:SUBLANES])
    return jnp.concatenate([head, r[SUBLANES:]], axis=0)


def _conv_layer_kernel(x_ref, g_ref, win_ref, wout_ref, cw_ref, o_ref, carry_ref, *,
                       tiles_per_seq, width, sub_rows):
    w = width

    @pl.when(pl.program_id(0) % tiles_per_seq == 0)
    def _():
        carry_ref[...] = jnp.zeros_like(carry_ref)

    prev = carry_ref[...]
    for r in range(x_ref.shape[0] // sub_rows):
        rows = slice(r * sub_rows, (r + 1) * sub_rows)
        x = x_ref[rows, :]
        h = _rmsnorm_rows(x, g_ref[...]).astype(_BF16)
        b_gate = _dot(h, win_ref[:, 0:w])
        c_gate = _dot(h, win_ref[:, w:2 * w])
        u = _dot(h, win_ref[:, 2 * w:3 * w])
        z = _dot(h, win_ref[:, 3 * w:4 * w])
        v = c_gate * u
        conv = (_shift_rows(v, prev, 2) * cw_ref[0:1, :]
                + _shift_rows(v, prev, 1) * cw_ref[1:2, :] + v * cw_ref[2:3, :])
        prev = v[sub_rows - SUBLANES:]
        y = (b_gate * conv * jax.nn.silu(z)).astype(_BF16)
        o_ref[rows, :] = x + _dot(y, wout_ref[...])
    carry_ref[...] = prev


def _conv_layer(x2, g, w_in, w_out, conv_w, seq):
    n, d = x2.shape
    w = w_out.shape[0]
    tm = ROW_TILE
    return pl.pallas_call(
        functools.partial(_conv_layer_kernel, tiles_per_seq=seq // tm, width=w,
                          sub_rows=CONV_SUB_ROWS),
        grid=(n // tm,),
        in_specs=[
            pl.BlockSpec((tm, d), lambda i: (i, 0)),
            _const_spec((1, d)),
            _const_spec((d, 4 * w)),
            _const_spec((w, d)),
            _const_spec((CONV_K, w)),
        ],
        out_specs=pl.BlockSpec((tm, d), lambda i: (i, 0)),
        out_shape=jax.ShapeDtypeStruct((n, d), _F32),
        scratch_shapes=[pltpu.VMEM((SUBLANES, w), _F32)],
        compiler_params=_params(("arbitrary",)),
        name="conv_layer",
    )(x2, g.reshape(1, d), w_in.astype(_BF16), w_out.astype(_BF16), conv_w.T)


def _qkv_kernel(x_ref, g_ref, w_ref, q_ref, k_ref, vt_ref, *, width):
    w = width
    h = _rmsnorm_rows(x_ref[...], g_ref[...]).astype(_BF16)
    q = (_dot(h, w_ref[:, 0:w]) * (HEAD_DIM ** -0.5 * _LOG2E)).astype(_BF16)
    k = _dot(h, w_ref[:, w:2 * w]).astype(_BF16)
    v = _dot(h, w_ref[:, 2 * w:3 * w])
    n_tiles, n_heads, vt_rows, tk = vt_ref.shape
    row = lax.broadcasted_iota(jnp.int32, (vt_rows - V_DIM, tk), 0)
    ones_rows = jnp.where(row == 0, 1.0, 0.0).astype(_BF16)
    for t in range(n_tiles):
        rows = slice(t * tk, (t + 1) * tk)
        for hd in range(n_heads):
            cols = slice(hd * V_DIM, (hd + 1) * V_DIM)
            q_ref[t, hd] = q[rows, cols]
            k_ref[t, hd] = k[rows, cols]
            vt_ref[t, hd, 0:V_DIM, :] = v[rows, cols].T.astype(_BF16)
            vt_ref[t, hd, V_DIM:, :] = ones_rows


def _qkv_proj(x2, g, w_in):
    n, d = x2.shape
    w = w_in.shape[1] // 4
    tm, tk = ATTN_ROW_TILE, ATTN_K_TILE
    n_heads = w // V_DIM
    out = jax.ShapeDtypeStruct((n // tk, n_heads, tk, V_DIM), _BF16)
    head_tiles = pl.BlockSpec((tm // tk, n_heads, tk, V_DIM), lambda i: (i, 0, 0, 0))
    return pl.pallas_call(
        functools.partial(_qkv_kernel, width=w),
        grid=(n // tm,),
        in_specs=[
            pl.BlockSpec((tm, d), lambda i: (i, 0)),
            _const_spec((1, d)),
            _const_spec((d, 3 * w)),
        ],
        out_specs=[head_tiles, head_tiles,
                   pl.BlockSpec((tm // tk, n_heads, VT_ROWS, tk), lambda i: (i, 0, 0, 0))],
        out_shape=[out, out, jax.ShapeDtypeStruct((n // tk, n_heads, VT_ROWS, tk), _BF16)],
        compiler_params=_params(("arbitrary",)),
        name="qkv_proj",
    )(x2, g.reshape(1, d), w_in)


def _attn_kernel(q_ref, k_ref, vt_ref, bias_ref, lam_ref, sg_ref, o_ref, *scratch, lambda_init):
    n_heads = q_ref.shape[1]
    per_head = len(scratch) // n_heads
    heads = tuple(scratch[h * per_head:(h + 1) * per_head] for h in range(n_heads))
    tk = vt_ref.shape[-1]
    tq = q_ref.shape[0] * tk
    qi = pl.program_id(2)

    for h in range(n_heads):
        qq_sc, m_sc, acc_sc = heads[h][:3]
        q = q_ref[:, h].reshape(tq, V_DIM)
        lane = lax.broadcasted_iota(jnp.int32, q.shape, 1)
        zero = jnp.zeros_like(q)
        qq_sc[0:tq, :] = jnp.where(lane < HEAD_DIM, q, zero)
        qq_sc[tq:2 * tq, :] = jnp.where(lane >= HEAD_DIM, q, zero)
        m_sc[...] = jnp.full_like(m_sc, -jnp.inf)
        acc_sc[...] = jnp.zeros_like(acc_sc)

    prev, diag0, diag1 = 0, 1, 2

    def stage_a(u, biases):
        for h in range(n_heads):
            qq_sc, _, _, s_sc, t_sc, _, _ = heads[h]
            keys = k_ref[pl.ds(2 * u, 2), h].reshape(2 * tk, V_DIM)
            s = lax.dot_general(keys, qq_sc[...], (((1,), (1,)), ((), ())),
                                preferred_element_type=_F32)
            col_max = None
            for half, bias in enumerate(biases):
                tile = s[half * tk:(half + 1) * tk]
                if bias is not None:
                    tile = tile + bias_ref[h, bias]
                s_sc[half * tk:(half + 1) * tk, 0:2 * tq] = tile
                tile_max = jnp.max(tile, axis=0, keepdims=True)
                col_max = tile_max if col_max is None else jnp.maximum(col_max, tile_max)
            t_sc[...] = col_max

    hq = tq // 2

    def late_queries(x):
        return jnp.concatenate([x[:, hq:tq], x[:, tq + hq:]], axis=1)

    def stage_a_diag(u):
        for h in range(n_heads):
            qq_sc, _, _, s_sc, t_sc, _, _ = heads[h]
            nt = (((1,), (1,)), ((), ()))
            s_lo = lax.dot_general(k_ref[2 * u, h], qq_sc[...], nt,
                                   preferred_element_type=_F32) + bias_ref[h, diag0]
            qq_late = jnp.concatenate([qq_sc[hq:tq, :], qq_sc[tq + hq:, :]], axis=0)
            s_hi = lax.dot_general(k_ref[2 * u + 1, h], qq_late, nt,
                                   preferred_element_type=_F32)
            s_hi = s_hi + late_queries(bias_ref[h, diag1])
            masked = jnp.full((tk, hq), -jnp.inf, _F32)
            s_sc[0:tk, 0:2 * tq] = s_lo
            s_sc[tk:, 0:2 * tq] = jnp.concatenate([masked, s_hi[:, :hq], masked, s_hi[:, hq:]], axis=1)
            max_lo = jnp.max(s_lo, axis=0, keepdims=True)
            max_hi = jnp.max(s_hi, axis=0, keepdims=True)
            t_sc[...] = jnp.concatenate(
                [max_lo[:, :hq], jnp.maximum(max_lo[:, hq:tq], max_hi[:, :hq]),
                 max_lo[:, tq:tq + hq], jnp.maximum(max_lo[:, tq + hq:], max_hi[:, hq:])], axis=1)

    def stage_b():
        for h in range(n_heads):
            _, m_sc, _, s_sc, t_sc, p_sc, a_sc = heads[h]
            m_prev = m_sc[...]
            m_new = jnp.maximum(m_prev, t_sc[...])
            a_sc[...] = jnp.exp2(m_prev - m_new)
            p_sc[...] = jnp.exp2(s_sc[:, 0:2 * tq] - m_new).astype(_BF16)
            m_sc[...] = m_new

    def stage_c(u):
        for h in range(n_heads):
            _, _, acc_sc, _, _, p_sc, a_sc = heads[h]
            vt = jnp.concatenate([vt_ref[2 * u, h], vt_ref[2 * u + 1, h]], axis=1)
            acc_sc[...] = acc_sc[...] * a_sc[...] + _dot(vt, p_sc[...])

    def stage_c_diag(u):
        for h in range(n_heads):
            _, _, acc_sc, _, _, p_sc, a_sc = heads[h]
            pv = _dot(vt_ref[2 * u, h], p_sc[0:tk, :])
            pv_late = _dot(vt_ref[2 * u + 1, h], late_queries(p_sc[tk:, :]))
            pv = jnp.concatenate([pv[:, :hq], pv[:, hq:tq] + pv_late[:, :hq],
                                  pv[:, tq:tq + hq], pv[:, tq + hq:] + pv_late[:, hq:]], axis=1)
            acc_sc[...] = acc_sc[...] * a_sc[...] + pv

    def trip(u, biases=(None, None), issue_a=True, issue_c=True):
        if issue_c:
            stage_c(u - 1)
        stage_b()
        if issue_a and biases is None:
            stage_a_diag(u + 1)
        elif issue_a:
            stage_a(u + 1, biases)

    before_diag = (None, prev)
    short = ([], [before_diag], [(None, None), before_diag])

    for n_q, step_biases in enumerate(short):
        @pl.when(qi == n_q)
        def _(step_biases=step_biases):
            for u, biases in enumerate(step_biases):
                stage_a(u, biases)
                stage_b()
                stage_c(u)
            stage_a_diag(len(step_biases))
            stage_b()
            stage_c_diag(len(step_biases))

    @pl.when(qi >= len(short))
    def _():
        stage_a(0, (None, None))
        trip(0, issue_c=False)

        def plain(u, carry):
            trip(u)
            return carry

        lax.fori_loop(1, qi - 2, plain, 0)
        trip(qi - 2, before_diag)
        trip(qi - 1, None)
        trip(qi, issue_a=False)
        stage_c_diag(qi)

    lam = (jnp.exp(jnp.sum(lam_ref[0:1, :] * lam_ref[1:2, :], axis=-1, keepdims=True))
           - jnp.exp(jnp.sum(lam_ref[2:3, :] * lam_ref[3:4, :], axis=-1, keepdims=True))
           + lambda_init)
    for h in range(n_heads):
        acc_sc = heads[h][2]
        o_maps = acc_sc[0:V_DIM, :] * (1.0 / acc_sc[V_DIM:V_DIM + 1, :])
        o_t = o_maps[:, :tq] - lam * o_maps[:, tq:]
        o_t = o_t * lax.rsqrt(jnp.mean(o_t * o_t, axis=0, keepdims=True) + RMS_EPS)
        o_ref[:, h * V_DIM:(h + 1) * V_DIM] = (
            o_t.T * sg_ref[...] * (1.0 - lambda_init)).astype(o_ref.dtype)


def _attention(q, k, vt, bias_tiles, lam_params, subln_g, lambda_init, bsz):
    tq, tk, hps = ATTN_Q_TILE, ATTN_K_TILE, ATTN_HEADS_PER_STEP
    n_heads = q.shape[1]
    s = q.shape[0] * tk // bsz
    w = n_heads * V_DIM
    q_tiles = s // tq
    col = pltpu.VMEM((1, 2 * tq), _F32)
    per_head_scratch = [
        pltpu.VMEM((2 * tq, V_DIM), _BF16),
        col,
        pltpu.VMEM((VT_ROWS, 2 * tq), _F32),
        pltpu.VMEM((2 * tk, 2 * tq + LANES), _F32),
        col,
        pltpu.VMEM((2 * tk, 2 * tq), _BF16),
        col,
    ]
    return pl.pallas_call(
        functools.partial(_attn_kernel, lambda_init=lambda_init),
        grid=(bsz, n_heads // hps, q_tiles),
        in_specs=[
            pl.BlockSpec((tq // tk, hps, tk, V_DIM), lambda b, h, i: (b * q_tiles + i, h, 0, 0)),
            pl.BlockSpec((s // tk, hps, tk, V_DIM), lambda b, h, i: (b, h, 0, 0)),
            pl.BlockSpec((s // tk, hps, VT_ROWS, tk), lambda b, h, i: (b, h, 0, 0)),
            pl.BlockSpec((hps, 3, tk, 2 * tq), lambda b, h, i: (h, 0, 0, 0)),
            _const_spec((4, HEAD_DIM)),
            _const_spec((1, V_DIM)),
        ],
        out_specs=pl.BlockSpec((None, tq, hps * V_DIM), lambda b, h, i: (b, i, h)),
        out_shape=jax.ShapeDtypeStruct((bsz, s, w), _BF16),
        scratch_shapes=per_head_scratch * hps,
        compiler_params=_params(("arbitrary", "arbitrary", "arbitrary")),
        name="diff_attention",
    )(q, k, vt, bias_tiles, lam_params, subln_g.reshape(1, V_DIM))


def _attn_out_kernel(x_ref, o_ref, g_ref, wz_ref, wout_ref, fg_ref, y_ref, *, final_norm):
    x = x_ref[...]
    h = _rmsnorm_rows(x, g_ref[...]).astype(_BF16)
    z = _dot(h, wz_ref[...])
    y = (o_ref[...] * jax.nn.silu(z)).astype(_BF16)
    x_new = x + _dot(y, wout_ref[...])
    if final_norm:
        x_new = _rmsnorm_rows(x_new, fg_ref[...])
    y_ref[...] = x_new


def _attn_out(x2, o2, g, w_in, w_out, final_g, final_norm):
    n, d = x2.shape
    w = w_out.shape[0]
    tm = ATTN_ROW_TILE
    return pl.pallas_call(
        functools.partial(_attn_out_kernel, final_norm=final_norm),
        grid=(n // tm,),
        in_specs=[
            pl.BlockSpec((tm, d), lambda i: (i, 0)),
            pl.BlockSpec((tm, w), lambda i: (i, 0)),
            _const_spec((1, d)),
            pl.BlockSpec((d, w), lambda i: (0, 3), pipeline_mode=pl.Buffered(1)),
            _const_spec((w, d)),
            _const_spec((1, d)),
        ],
        out_specs=pl.BlockSpec((tm, d), lambda i: (i, 0)),
        out_shape=jax.ShapeDtypeStruct((n, d), _F32),
        compiler_params=_params(("arbitrary",)),
        name="attn_out",
    )(x2, o2, g.reshape(1, d), w_in, w_out, final_g.reshape(1, d))


def _t5_causal_bucket(dist):
    d_safe = jnp.maximum(dist, 1).astype(_F32)
    large = MAX_EXACT + (jnp.log(d_safe / MAX_EXACT) / math.log(REL_MAX_DIST / MAX_EXACT)
                         * (NUM_BUCKETS - MAX_EXACT)).astype(jnp.int32)
    large = jnp.minimum(large, NUM_BUCKETS - 1)
    return jnp.where(dist < MAX_EXACT, dist, large)


def _toeplitz_pair(g_vec, rows, cols):
    period = g_vec.shape[-1]
    base = jnp.stack([jnp.roll(g_vec, r, axis=-1) for r in range(SUBLANES)], axis=-2)
    groups = []
    for a in range(rows // SUBLANES):
        shift = a * SUBLANES
        pieces = []
        for m in range(2):
            bm = base[..., m, :, :]
            pieces += [bm[..., period - shift:], bm[..., :cols - shift]]
        groups.append(jnp.concatenate(pieces, axis=-1))
    out = jnp.stack(groups, axis=-3)
    return out.reshape(out.shape[:-3] + (rows, 2 * cols))


def _bias_tiles(rel_bias, tq, tk):
    period = 2 * (tq + tk)
    delta = jnp.arange(period, dtype=jnp.int32)
    delta = jnp.where(delta >= period // 2, delta - period, delta)
    n_dist = tq + tk
    rel = ((rel_bias[_t5_causal_bucket(jnp.arange(n_dist, dtype=jnp.int32))]
            - rel_bias[NUM_BUCKETS - 1]) * _LOG2E).T

    def vec(offset):
        dist = delta + offset
        vals = rel[:, jnp.clip(dist, 0, n_dist - 1)]
        return jnp.where(dist >= 0, vals, -jnp.inf)

    g_vec = jnp.stack([vec(tk), vec(0), vec(-tk)], axis=1)
    g_vec = g_vec.reshape(N_DIFF_HEADS, 2, 3, period).transpose(0, 2, 1, 3)
    return _toeplitz_pair(g_vec, tk, tq)


def _lambda_init_for(layer_idx):
    return 0.8 - 0.6 * math.exp(-0.3 * layer_idx)


def kernel(x, norm_g, w_in, w_out, conv_w, lambda_q1, lambda_k1, lambda_q2, lambda_k2, subln_g,
           rel_bias, final_g):
    bsz, seq, d = x.shape
    depth = norm_g.shape[0]
    w = w_out.shape[1]
    tq, tk = ATTN_Q_TILE, ATTN_K_TILE
    assert seq % tq == 0 and seq % ROW_TILE == 0 and w == N_DIFF_HEADS * V_DIM
    assert (bsz * seq) % ATTN_ROW_TILE == 0
    assert N_DIFF_HEADS % ATTN_HEADS_PER_STEP == 0
    assert depth % N_MIXERS == 0

    bias_tiles = _bias_tiles(rel_bias, tq, tk)
    x2 = x.reshape(bsz * seq, d)
    for i in range(depth):
        j = i // N_MIXERS
        if i % N_MIXERS == 0:
            x2 = _conv_layer(x2, norm_g[i], w_in[i], w_out[i], conv_w[j], seq)
        else:
            w_in_bf = w_in[i].astype(_BF16)
            q, k, vt = _qkv_proj(x2, norm_g[i], w_in_bf)
            lam_params = jnp.stack([lambda_q1[j], lambda_k1[j], lambda_q2[j], lambda_k2[j]])
            o = _attention(q, k, vt, bias_tiles, lam_params, subln_g[j], _lambda_init_for(i),
                           bsz)
            x2 = _attn_out(x2, o.reshape(bsz * seq, w), norm_g[i], w_in_bf,
                           w_out[i].astype(_BF16), final_g, final_norm=(i == depth - 1))
    return x2.reshape(bsz, seq, d)
```

```python
import functools
import math

import jax
import jax.numpy as jnp
from jax import lax
from jax.experimental import pallas as pl
from jax.experimental.pallas import tpu as pltpu

N_MIXERS = 2
CONV_K = 3
N_DIFF_HEADS = 8
HEAD_DIM = 64
V_DIM = 2 * HEAD_DIM
N_MAPS = 2 * N_DIFF_HEADS
NUM_BUCKETS = 32
MAX_EXACT = NUM_BUCKETS // 2
REL_MAX_DIST = 128
RMS_EPS = 1e-6

SUBLANES = 8
BF16_SUBLANES = 16
ROW_TILE = 512
CONV_SUB_ROWS = 256
ATTN_ROW_TILE = 1024
ATTN_Q_TILE = 512
ATTN_K_TILE = 256
ATTN_HEADS_PER_STEP = 2
VT_ROWS = V_DIM + BF16_SUBLANES
VMEM_LIMIT_BYTES = 56 * 1024 * 1024

assert ATTN_Q_TILE == 2 * ATTN_K_TILE and ATTN_K_TILE >= REL_MAX_DIST
assert ATTN_ROW_TILE % ATTN_K_TILE == 0

_F32 = jnp.float32
_BF16 = jnp.bfloat16
_LOG2E = math.log2(math.e)


def _dot(a, b):
    return jnp.dot(a, b, preferred_element_type=_F32)


def _rmsnorm_rows(x, g):
    return x * lax.rsqrt(jnp.mean(x * x, axis=-1, keepdims=True) + RMS_EPS) * g


def _const_spec(shape):
    return pl.BlockSpec(shape, lambda *_: (0,) * len(shape), pipeline_mode=pl.Buffered(1))


def _params(semantics):
    return pltpu.CompilerParams(dimension_semantics=semantics,
                                vmem_limit_bytes=VMEM_LIMIT_BYTES)


def _shift_rows(v, prev, s):
    r = pltpu.roll(v, s, 0)
    p = pltpu.roll(prev, s, 0)
    row = lax.broadcasted_iota(jnp.int32, prev.shape, 0)
    head = jnp.where(row < s, p, r[:SUBLANES])
    return jnp.concatenate([head, r[SUBLANES:]], axis=0)


def _conv_layer_kernel(x_ref, g_ref, win_ref, wout_ref, cw_ref, o_ref, carry_ref, *,
                       tiles_per_seq, width, sub_rows):
    w = width

    @pl.when(pl.program_id(0) % tiles_per_seq == 0)
    def _():
        carry_ref[...] = jnp.zeros_like(carry_ref)

    prev = carry_ref[...]
    for r in range(x_ref.shape[0] // sub_rows):
        rows = slice(r * sub_rows, (r + 1) * sub_rows)
        x = x_ref[rows, :]
        h = _rmsnorm_rows(x, g_ref[...]).astype(_BF16)
        b_gate = _dot(h, win_ref[:, 0:w])
        c_gate = _dot(h, win_ref[:, w:2 * w])
        u = _dot(h, win_ref[:, 2 * w:3 * w])
        z = _dot(h, win_ref[:, 3 * w:4 * w])
        v = c_gate * u
        conv = (_shift_rows(v, prev, 2) * cw_ref[0:1, :]
                + _shift_rows(v, prev, 1) * cw_ref[1:2, :] + v * cw_ref[2:3, :])
        prev = v[sub_rows - SUBLANES:]
        y = (b_gate * conv * jax.nn.silu(z)).astype(_BF16)
        o_ref[rows, :] = x + _dot(y, wout_ref[...])
    carry_ref[...] = prev


def _conv_layer(x2, g, w_in, w_out, conv_w, seq):
    n, d = x2.shape
    w = w_out.shape[0]
    tm = ROW_TILE
    return pl.pallas_call(
        functools.partial(_conv_layer_kernel, tiles_per_seq=seq // tm, width=w,
                          sub_rows=CONV_SUB_ROWS),
        grid=(n // tm,),
        in_specs=[
            pl.BlockSpec((tm, d), lambda i: (i, 0)),
            _const_spec((1, d)),
            _const_spec((d, 4 * w)),
            _const_spec((w, d)),
            _const_spec((CONV_K, w)),
        ],
        out_specs=pl.BlockSpec((tm, d), lambda i: (i, 0)),
        out_shape=jax.ShapeDtypeStruct((n, d), _F32),
        scratch_shapes=[pltpu.VMEM((SUBLANES, w), _F32)],
        compiler_params=_params(("arbitrary",)),
        name="conv_layer",
    )(x2, g.reshape(1, d), w_in.astype(_BF16), w_out.astype(_BF16), conv_w.T)


def _qkv_kernel(x_ref, g_ref, w_ref, q_ref, k_ref, vt_ref, *, width):
    w = width
    h = _rmsnorm_rows(x_ref[...], g_ref[...]).astype(_BF16)
    q = (_dot(h, w_ref[:, 0:w]) * (HEAD_DIM ** -0.5 * _LOG2E)).astype(_BF16)
    k = _dot(h, w_ref[:, w:2 * w]).astype(_BF16)
    v = _dot(h, w_ref[:, 2 * w:3 * w])
    n_tiles, n_heads, vt_rows, tk = vt_ref.shape
    row = lax.broadcasted_iota(jnp.int32, (vt_rows - V_DIM, tk), 0)
    ones_rows = jnp.where(row == 0, 1.0, 0.0).astype(_BF16)
    for t in range(n_tiles):
        rows = slice(t * tk, (t + 1) * tk)
        for hd in range(n_heads):
            cols = slice(hd * V_DIM, (hd + 1) * V_DIM)
            q_ref[t, hd] = q[rows, cols]
            k_ref[t, hd] = k[rows, cols]
            vt_ref[t, hd, 0:V_DIM, :] = v[rows, cols].T.astype(_BF16)
            vt_ref[t, hd, V_DIM:, :] = ones_rows


def _qkv_proj(x2, g, w_in):
    n, d = x2.shape
    w = w_in.shape[1] // 4
    tm, tk = ATTN_ROW_TILE, ATTN_K_TILE
    n_heads = w // V_DIM
    out = jax.ShapeDtypeStruct((n // tk, n_heads, tk, V_DIM), _BF16)
    head_tiles = pl.BlockSpec((tm // tk, n_heads, tk, V_DIM), lambda i: (i, 0, 0, 0))
    return pl.pallas_call(
        functools.partial(_qkv_kernel, width=w),
        grid=(n // tm,),
        in_specs=[
            pl.BlockSpec((tm, d), lambda i: (i, 0)),
            _const_spec((1, d)),
            _const_spec((d, 3 * w)),
        ],
        out_specs=[head_tiles, head_tiles,
                   pl.BlockSpec((tm // tk, n_heads, VT_ROWS, tk), lambda i: (i, 0, 0, 0))],
        out_shape=[out, out, jax.ShapeDtypeStruct((n // tk, n_heads, VT_ROWS, tk), _BF16)],
        compiler_params=_params(("arbitrary",)),
        name="qkv_proj",
    )(x2, g.reshape(1, d), w_in)


def _attn_kernel(q_ref, k_ref, vt_ref, bias_ref, lam_ref, sg_ref, o_ref, *scratch, lambda_init):
    n_heads = q_ref.shape[1]
    per_head = len(scratch) // n_heads
    heads = tuple(scratch[h * per_head:(h + 1) * per_head] for h in range(n_heads))
    tk = vt_ref.shape[-1]
    tq = q_ref.shape[0] * tk
    qi = pl.program_id(2)

    for h in range(n_heads):
        qq_sc, m_sc, acc_sc = heads[h][:3]
        q = q_ref[:, h].reshape(tq, V_DIM)
        lane = lax.broadcasted_iota(jnp.int32, q.shape, 1)
        zero = jnp.zeros_like(q)
        qq_sc[0:tq, :] = jnp.where(lane < HEAD_DIM, q, zero)
        qq_sc[tq:2 * tq, :] = jnp.where(lane >= HEAD_DIM, q, zero)
        m_sc[...] = jnp.full_like(m_sc, -jnp.inf)
        acc_sc[...] = jnp.zeros_like(acc_sc)

    prev, diag0, diag1 = 0, 1, 2

    def stage_a(u, biases):
        for h in range(n_heads):
            qq_sc, _, _, s_sc, t_sc, _, _ = heads[h]
            keys = k_ref[pl.ds(2 * u, 2), h].reshape(2 * tk, V_DIM)
            s = lax.dot_general(keys, qq_sc[...], (((1,), (1,)), ((), ())),
                                preferred_element_type=_F32)
            col_max = None
            for half, bias in enumerate(biases):
                tile = s[half * tk:(half + 1) * tk]
                if bias is not None:
                    tile = tile + bias_ref[h, bias]
                s_sc[half * tk:(half + 1) * tk, :] = tile
                tile_max = jnp.max(tile, axis=0, keepdims=True)
                col_max = tile_max if col_max is None else jnp.maximum(col_max, tile_max)
            t_sc[...] = col_max

    hq = tq // 2

    def late_queries(x):
        return jnp.concatenate([x[:, hq:tq], x[:, tq + hq:]], axis=1)

    def stage_a_diag(u):
        for h in range(n_heads):
            qq_sc, _, _, s_sc, t_sc, _, _ = heads[h]
            nt = (((1,), (1,)), ((), ()))
            s_lo = lax.dot_general(k_ref[2 * u, h], qq_sc[...], nt,
                                   preferred_element_type=_F32) + bias_ref[h, diag0]
            qq_late = jnp.concatenate([qq_sc[hq:tq, :], qq_sc[tq + hq:, :]], axis=0)
            s_hi = lax.dot_general(k_ref[2 * u + 1, h], qq_late, nt,
                                   preferred_element_type=_F32)
            s_hi = s_hi + late_queries(bias_ref[h, diag1])
            masked = jnp.full((tk, hq), -jnp.inf, _F32)
            s_sc[0:tk, :] = s_lo
            s_sc[tk:, :] = jnp.concatenate([masked, s_hi[:, :hq], masked, s_hi[:, hq:]], axis=1)
            max_lo = jnp.max(s_lo, axis=0, keepdims=True)
            max_hi = jnp.max(s_hi, axis=0, keepdims=True)
            t_sc[...] = jnp.concatenate(
                [max_lo[:, :hq], jnp.maximum(max_lo[:, hq:tq], max_hi[:, :hq]),
                 max_lo[:, tq:tq + hq], jnp.maximum(max_lo[:, tq + hq:], max_hi[:, hq:])], axis=1)

    def stage_b():
        for h in range(n_heads):
            _, m_sc, _, s_sc, t_sc, p_sc, a_sc = heads[h]
            m_prev = m_sc[...]
            m_new = jnp.maximum(m_prev, t_sc[...])
            a_sc[...] = jnp.exp2(m_prev - m_new)
            p_sc[...] = jnp.exp2(s_sc[...] - m_new).astype(_BF16)
            m_sc[...] = m_new

    def stage_c(u):
        for h in range(n_heads):
            _, _, acc_sc, _, _, p_sc, a_sc = heads[h]
            vt = jnp.concatenate([vt_ref[2 * u, h], vt_ref[2 * u + 1, h]], axis=1)
            acc_sc[...] = acc_sc[...] * a_sc[...] + _dot(vt, p_sc[...])

    def stage_c_diag(u):
        for h in range(n_heads):
            _, _, acc_sc, _, _, p_sc, a_sc = heads[h]
            pv = _dot(vt_ref[2 * u, h], p_sc[0:tk, :])
            pv_late = _dot(vt_ref[2 * u + 1, h], late_queries(p_sc[tk:, :]))
            pv = jnp.concatenate([pv[:, :hq], pv[:, hq:tq] + pv_late[:, :hq],
                                  pv[:, tq:tq + hq], pv[:, tq + hq:] + pv_late[:, hq:]], axis=1)
            acc_sc[...] = acc_sc[...] * a_sc[...] + pv

    def trip(u, biases=(None, None), issue_a=True, issue_c=True):
        if issue_c:
            stage_c(u - 1)
        stage_b()
        if issue_a and biases is None:
            stage_a_diag(u + 1)
        elif issue_a:
            stage_a(u + 1, biases)

    before_diag = (None, prev)
    short = ([], [before_diag], [(None, None), before_diag])

    for n_q, step_biases in enumerate(short):
        @pl.when(qi == n_q)
        def _(step_biases=step_biases):
            n_before = len(step_biases)
            if n_before:
                stage_a(0, step_biases[0])
            else:
                stage_a_diag(0)
            for u in range(n_before + 1):
                next_biases = step_biases[u + 1] if u + 1 < n_before else None
                trip(u, next_biases, issue_a=u < n_before, issue_c=u > 0)
            stage_c_diag(n_before)

    @pl.when(qi >= len(short))
    def _():
        stage_a(0, (None, None))
        trip(0, issue_c=False)

        def plain(u, carry):
            trip(u)
            return carry

        lax.fori_loop(1, qi - 2, plain, 0)
        trip(qi - 2, before_diag)
        trip(qi - 1, None)
        trip(qi, issue_a=False)
        stage_c_diag(qi)

    lam = (jnp.exp(jnp.sum(lam_ref[0:1, :] * lam_ref[1:2, :], axis=-1, keepdims=True))
           - jnp.exp(jnp.sum(lam_ref[2:3, :] * lam_ref[3:4, :], axis=-1, keepdims=True))
           + lambda_init)
    for h in range(n_heads):
        acc_sc = heads[h][2]
        o_maps = acc_sc[0:V_DIM, :] * (1.0 / acc_sc[V_DIM:V_DIM + 1, :])
        o_t = o_maps[:, :tq] - lam * o_maps[:, tq:]
        o_t = o_t * lax.rsqrt(jnp.mean(o_t * o_t, axis=0, keepdims=True) + RMS_EPS)
        o_ref[:, h * V_DIM:(h + 1) * V_DIM] = (
            o_t.T * sg_ref[...] * (1.0 - lambda_init)).astype(o_ref.dtype)


def _attention(q, k, vt, bias_tiles, lam_params, subln_g, lambda_init, bsz):
    tq, tk, hps = ATTN_Q_TILE, ATTN_K_TILE, ATTN_HEADS_PER_STEP
    n_heads = q.shape[1]
    s = q.shape[0] * tk // bsz
    w = n_heads * V_DIM
    q_tiles = s // tq
    col = pltpu.VMEM((1, 2 * tq), _F32)
    per_head_scratch = [
        pltpu.VMEM((2 * tq, V_DIM), _BF16),
        col,
        pltpu.VMEM((VT_ROWS, 2 * tq), _F32),
        pltpu.VMEM((2 * tk, 2 * tq), _F32),
        col,
        pltpu.VMEM((2 * tk, 2 * tq), _BF16),
        col,
    ]
    return pl.pallas_call(
        functools.partial(_attn_kernel, lambda_init=lambda_init),
        grid=(bsz, n_heads // hps, q_tiles),
        in_specs=[
            pl.BlockSpec((tq // tk, hps, tk, V_DIM), lambda b, h, i: (b * q_tiles + i, h, 0, 0)),
            pl.BlockSpec((s // tk, hps, tk, V_DIM), lambda b, h, i: (b, h, 0, 0)),
            pl.BlockSpec((s // tk, hps, VT_ROWS, tk), lambda b, h, i: (b, h, 0, 0)),
            pl.BlockSpec((hps, 3, tk, 2 * tq), lambda b, h, i: (h, 0, 0, 0)),
            _const_spec((4, HEAD_DIM)),
            _const_spec((1, V_DIM)),
        ],
        out_specs=pl.BlockSpec((None, tq, hps * V_DIM), lambda b, h, i: (b, i, h)),
        out_shape=jax.ShapeDtypeStruct((bsz, s, w), _BF16),
        scratch_shapes=per_head_scratch * hps,
        compiler_params=_params(("arbitrary", "arbitrary", "arbitrary")),
        name="diff_attention",
    )(q, k, vt, bias_tiles, lam_params, subln_g.reshape(1, V_DIM))


def _attn_out_kernel(x_ref, o_ref, g_ref, wz_ref, wout_ref, fg_ref, y_ref, *, final_norm):
    x = x_ref[...]
    h = _rmsnorm_rows(x, g_ref[...]).astype(_BF16)
    z = _dot(h, wz_ref[...])
    y = (o_ref[...] * jax.nn.silu(z)).astype(_BF16)
    x_new = x + _dot(y, wout_ref[...])
    if final_norm:
        x_new = _rmsnorm_rows(x_new, fg_ref[...])
    y_ref[...] = x_new


def _attn_out(x2, o2, g, w_in, w_out, final_g, final_norm):
    n, d = x2.shape
    w = w_out.shape[0]
    tm = ATTN_ROW_TILE
    return pl.pallas_call(
        functools.partial(_attn_out_kernel, final_norm=final_norm),
        grid=(n // tm,),
        in_specs=[
            pl.BlockSpec((tm, d), lambda i: (i, 0)),
            pl.BlockSpec((tm, w), lambda i: (i, 0)),
            _const_spec((1, d)),
            pl.BlockSpec((d, w), lambda i: (0, 3), pipeline_mode=pl.Buffered(1)),
            _const_spec((w, d)),
            _const_spec((1, d)),
        ],
        out_specs=pl.BlockSpec((tm, d), lambda i: (i, 0)),
        out_shape=jax.ShapeDtypeStruct((n, d), _F32),
        compiler_params=_params(("arbitrary",)),
        name="attn_out",
    )(x2, o2, g.reshape(1, d), w_in, w_out, final_g.reshape(1, d))


def _t5_causal_bucket(dist):
    d_safe = jnp.maximum(dist, 1).astype(_F32)
    large = MAX_EXACT + (jnp.log(d_safe / MAX_EXACT) / math.log(REL_MAX_DIST / MAX_EXACT)
                         * (NUM_BUCKETS - MAX_EXACT)).astype(jnp.int32)
    large = jnp.minimum(large, NUM_BUCKETS - 1)
    return jnp.where(dist < MAX_EXACT, dist, large)


def _toeplitz_pair(g_vec, rows, cols):
    period = g_vec.shape[-1]
    base = jnp.stack([jnp.roll(g_vec, r, axis=-1) for r in range(SUBLANES)], axis=-2)
    groups = []
    for a in range(rows // SUBLANES):
        shift = a * SUBLANES
        pieces = []
        for m in range(2):
            bm = base[..., m, :, :]
            pieces += [bm[..., period - shift:], bm[..., :cols - shift]]
        groups.append(jnp.concatenate(pieces, axis=-1))
    out = jnp.stack(groups, axis=-3)
    return out.reshape(out.shape[:-3] + (rows, 2 * cols))


def _bias_tiles(rel_bias, tq, tk):
    period = 2 * (tq + tk)
    delta = jnp.arange(period, dtype=jnp.int32)
    delta = jnp.where(delta >= period // 2, delta - period, delta)
    n_dist = tq + tk
    rel = ((rel_bias[_t5_causal_bucket(jnp.arange(n_dist, dtype=jnp.int32))]
            - rel_bias[NUM_BUCKETS - 1]) * _LOG2E).T

    def vec(offset):
        dist = delta + offset
        vals = rel[:, jnp.clip(dist, 0, n_dist - 1)]
        return jnp.where(dist >= 0, vals, -jnp.inf)

    g_vec = jnp.stack([vec(tk), vec(0), vec(-tk)], axis=1)
    g_vec = g_vec.reshape(N_DIFF_HEADS, 2, 3, period).transpose(0, 2, 1, 3)
    return _toeplitz_pair(g_vec, tk, tq)


def _lambda_init_for(layer_idx):
    return 0.8 - 0.6 * math.exp(-0.3 * layer_idx)


def kernel(x, norm_g, w_in, w_out, conv_w, lambda_q1, lambda_k1, lambda_q2, lambda_k2, subln_g,
           rel_bias, final_g):
    bsz, seq, d = x.shape
    depth = norm_g.shape[0]
    w = w_out.shape[1]
    tq, tk = ATTN_Q_TILE, ATTN_K_TILE
    assert seq % tq == 0 and seq % ROW_TILE == 0 and w == N_DIFF_HEADS * V_DIM
    assert (bsz * seq) % ATTN_ROW_TILE == 0
    assert N_DIFF_HEADS % ATTN_HEADS_PER_STEP == 0
    assert depth % N_MIXERS == 0

    bias_tiles = _bias_tiles(rel_bias, tq, tk)
    x2 = x.reshape(bsz * seq, d)
    for i in range(depth):
        j = i // N_MIXERS
        if i % N_MIXERS == 0:
            x2 = _conv_layer(x2, norm_g[i], w_in[i], w_out[i], conv_w[j], seq)
        else:
            w_in_bf = w_in[i].astype(_BF16)
            q, k, vt = _qkv_proj(x2, norm_g[i], w_in_bf)
            lam_params = jnp.stack([lambda_q1[j], lambda_k1[j], lambda_q2[j], lambda_k2[j]])
            o = _attention(q, k, vt, bias_tiles, lam_params, subln_g[j], _lambda_init_for(i),
                           bsz)
            x2 = _attn_out(x2, o.reshape(bsz * seq, w), norm_g[i], w_in_bf,
                           w_out[i].astype(_BF16), final_g, final_norm=(i == depth - 1))
    return x2.reshape(bsz, seq, d)
```

```python
import functools
import math

import jax
import jax.numpy as jnp
from jax import lax
from jax.experimental import pallas as pl
from jax.experimental.pallas import tpu as pltpu

N_MIXERS = 2
CONV_K = 3
N_DIFF_HEADS = 8
HEAD_DIM = 64
V_DIM = 2 * HEAD_DIM
N_MAPS = 2 * N_DIFF_HEADS
NUM_BUCKETS = 32
MAX_EXACT = NUM_BUCKETS // 2
REL_MAX_DIST = 128
RMS_EPS = 1e-6

SUBLANES = 8
BF16_SUBLANES = 16
ROW_TILE = 512
CONV_SUB_ROWS = 256
ATTN_ROW_TILE = 1024
ATTN_Q_TILE = 512
ATTN_K_TILE = 256
ATTN_HEADS_PER_STEP = 2
VT_ROWS = V_DIM + BF16_SUBLANES
VMEM_LIMIT_BYTES = 56 * 1024 * 1024

assert ATTN_Q_TILE == 2 * ATTN_K_TILE and ATTN_K_TILE >= REL_MAX_DIST
assert ATTN_ROW_TILE % ATTN_K_TILE == 0

_F32 = jnp.float32
_BF16 = jnp.bfloat16
_LOG2E = math.log2(math.e)


def _dot(a, b):
    return jnp.dot(a, b, preferred_element_type=_F32)


def _rmsnorm_rows(x, g):
    return x * lax.rsqrt(jnp.mean(x * x, axis=-1, keepdims=True) + RMS_EPS) * g


def _const_spec(shape):
    return pl.BlockSpec(shape, lambda *_: (0,) * len(shape), pipeline_mode=pl.Buffered(1))


def _params(semantics):
    return pltpu.CompilerParams(dimension_semantics=semantics,
                                vmem_limit_bytes=VMEM_LIMIT_BYTES)


def _shift_rows(v, prev, s):
    r = pltpu.roll(v, s, 0)
    p = pltpu.roll(prev, s, 0)
    row = lax.broadcasted_iota(jnp.int32, prev.shape, 0)
    head = jnp.where(row < s, p, r[:SUBLANES])
    return jnp.concatenate([head, r[SUBLANES:]], axis=0)


def _conv_layer_kernel(x_ref, g_ref, win_ref, wout_ref, cw_ref, o_ref, carry_ref, *,
                       tiles_per_seq, width, sub_rows):
    w = width

    @pl.when(pl.program_id(0) % tiles_per_seq == 0)
    def _():
        carry_ref[...] = jnp.zeros_like(carry_ref)

    prev = carry_ref[...]
    for r in range(x_ref.shape[0] // sub_rows):
        rows = slice(r * sub_rows, (r + 1) * sub_rows)
        x = x_ref[rows, :]
        h = _rmsnorm_rows(x, g_ref[...]).astype(_BF16)
        b_gate = _dot(h, win_ref[:, 0:w])
        c_gate = _dot(h, win_ref[:, w:2 * w])
        u = _dot(h, win_ref[:, 2 * w:3 * w])
        z = _dot(h, win_ref[:, 3 * w:4 * w])
        v = c_gate * u
        conv = (_shift_rows(v, prev, 2) * cw_ref[0:1, :]
                + _shift_rows(v, prev, 1) * cw_ref[1:2, :] + v * cw_ref[2:3, :])
        prev = v[sub_rows - SUBLANES:]
        y = (b_gate * conv * jax.nn.silu(z)).astype(_BF16)
        o_ref[rows, :] = x + _dot(y, wout_ref[...])
    carry_ref[...] = prev


def _conv_layer(x2, g, w_in, w_out, conv_w, seq):
    n, d = x2.shape
    w = w_out.shape[0]
    tm = ROW_TILE
    return pl.pallas_call(
        functools.partial(_conv_layer_kernel, tiles_per_seq=seq // tm, width=w,
                          sub_rows=CONV_SUB_ROWS),
        grid=(n // tm,),
        in_specs=[
            pl.BlockSpec((tm, d), lambda i: (i, 0)),
            _const_spec((1, d)),
            _const_spec((d, 4 * w)),
            _const_spec((w, d)),
            _const_spec((CONV_K, w)),
        ],
        out_specs=pl.BlockSpec((tm, d), lambda i: (i, 0)),
        out_shape=jax.ShapeDtypeStruct((n, d), _F32),
        scratch_shapes=[pltpu.VMEM((SUBLANES, w), _F32)],
        compiler_params=_params(("arbitrary",)),
        name="conv_layer",
    )(x2, g.reshape(1, d), w_in.astype(_BF16), w_out.astype(_BF16), conv_w.T)


def _qkv_kernel(x_ref, g_ref, w_ref, qt_ref, k_ref, vt_ref, *, width):
    w = width
    h = _rmsnorm_rows(x_ref[...], g_ref[...]).astype(_BF16)
    q = _dot(h, w_ref[:, 0:w]) * (HEAD_DIM ** -0.5 * _LOG2E)
    k = _dot(h, w_ref[:, w:2 * w]).astype(_BF16)
    v = _dot(h, w_ref[:, 2 * w:3 * w])
    n_tiles, n_heads, vt_rows, tk = vt_ref.shape
    row = lax.broadcasted_iota(jnp.int32, (vt_rows - V_DIM, tk), 0)
    ones_rows = jnp.where(row == 0, 1.0, 0.0).astype(_BF16)
    for t in range(n_tiles):
        rows = slice(t * tk, (t + 1) * tk)
        for hd in range(n_heads):
            cols = slice(hd * V_DIM, (hd + 1) * V_DIM)
            qt_ref[t, hd] = q[rows, cols].T.astype(_BF16)
            k_ref[t, hd] = k[rows, cols]
            vt_ref[t, hd, 0:V_DIM, :] = v[rows, cols].T.astype(_BF16)
            vt_ref[t, hd, V_DIM:, :] = ones_rows


def _qkv_proj(x2, g, w_in):
    n, d = x2.shape
    w = w_in.shape[1] // 4
    tm, tk = ATTN_ROW_TILE, ATTN_K_TILE
    n_heads = w // V_DIM
    out = jax.ShapeDtypeStruct((n // tk, n_heads, tk, V_DIM), _BF16)
    head_tiles = pl.BlockSpec((tm // tk, n_heads, tk, V_DIM), lambda i: (i, 0, 0, 0))
    return pl.pallas_call(
        functools.partial(_qkv_kernel, width=w),
        grid=(n // tm,),
        in_specs=[
            pl.BlockSpec((tm, d), lambda i: (i, 0)),
            _const_spec((1, d)),
            _const_spec((d, 3 * w)),
        ],
        out_specs=[pl.BlockSpec((tm // tk, n_heads, V_DIM, tk), lambda i: (i, 0, 0, 0)), head_tiles,
                   pl.BlockSpec((tm // tk, n_heads, VT_ROWS, tk), lambda i: (i, 0, 0, 0))],
        out_shape=[jax.ShapeDtypeStruct((n // tk, n_heads, V_DIM, tk), _BF16), out,
                   jax.ShapeDtypeStruct((n // tk, n_heads, VT_ROWS, tk), _BF16)],
        compiler_params=_params(("arbitrary",)),
        name="qkv_proj",
    )(x2, g.reshape(1, d), w_in)


def _attn_kernel(qt_ref, k_ref, vt_ref, bias_ref, lam_ref, sg_ref, o_ref, *scratch, lambda_init):
    n_heads = qt_ref.shape[1]
    per_head = len(scratch) // n_heads
    heads = tuple(scratch[h * per_head:(h + 1) * per_head] for h in range(n_heads))
    tk = vt_ref.shape[-1]
    tq = qt_ref.shape[0] * tk
    qi = pl.program_id(2)

    for h in range(n_heads):
        qq_sc, m_sc, acc_sc = heads[h][:3]
        q_t = jnp.concatenate([qt_ref[t, h] for t in range(qt_ref.shape[0])], axis=1)
        row = lax.broadcasted_iota(jnp.int32, q_t.shape, 0)
        zero = jnp.zeros_like(q_t)
        qq_sc[:, 0:tq] = jnp.where(row < HEAD_DIM, q_t, zero)
        qq_sc[:, tq:2 * tq] = jnp.where(row >= HEAD_DIM, q_t, zero)
        m_sc[...] = jnp.full_like(m_sc, -jnp.inf)
        acc_sc[...] = jnp.zeros_like(acc_sc)

    prev, diag0, diag1 = 0, 1, 2

    def stage_a(u, biases):
        for h in range(n_heads):
            qq_sc, _, _, s_sc, t_sc, _, _ = heads[h]
            keys = k_ref[pl.ds(2 * u, 2), h].reshape(2 * tk, V_DIM)
            s = _dot(keys, qq_sc[...])
            col_max = None
            for half, bias in enumerate(biases):
                tile = s[half * tk:(half + 1) * tk]
                if bias is not None:
                    tile = tile + bias_ref[h, bias]
                s_sc[half * tk:(half + 1) * tk, :] = tile
                tile_max = jnp.max(tile, axis=0, keepdims=True)
                col_max = tile_max if col_max is None else jnp.maximum(col_max, tile_max)
            t_sc[...] = col_max

    hq = tq // 2

    def late_queries(x):
        return jnp.concatenate([x[:, hq:tq], x[:, tq + hq:]], axis=1)

    def stage_a_diag(u):
        for h in range(n_heads):
            qq_sc, _, _, s_sc, t_sc, _, _ = heads[h]
            s_lo = _dot(k_ref[2 * u, h], qq_sc[...]) + bias_ref[h, diag0]
            s_hi = _dot(k_ref[2 * u + 1, h], late_queries(qq_sc[...]))
            s_hi = s_hi + late_queries(bias_ref[h, diag1])
            masked = jnp.full((tk, hq), -jnp.inf, _F32)
            s_sc[0:tk, :] = s_lo
            s_sc[tk:, :] = jnp.concatenate([masked, s_hi[:, :hq], masked, s_hi[:, hq:]], axis=1)
            max_lo = jnp.max(s_lo, axis=0, keepdims=True)
            max_hi = jnp.max(s_hi, axis=0, keepdims=True)
            t_sc[...] = jnp.concatenate(
                [max_lo[:, :hq], jnp.maximum(max_lo[:, hq:tq], max_hi[:, :hq]),
                 max_lo[:, tq:tq + hq], jnp.maximum(max_lo[:, tq + hq:], max_hi[:, hq:])], axis=1)

    def stage_b():
        for h in range(n_heads):
            _, m_sc, _, s_sc, t_sc, p_sc, a_sc = heads[h]
            m_prev = m_sc[...]
            m_new = jnp.maximum(m_prev, t_sc[...])
            a_sc[...] = jnp.exp2(m_prev - m_new)
            p_sc[...] = jnp.exp2(s_sc[...] - m_new).astype(_BF16)
            m_sc[...] = m_new

    def stage_c(u):
        for h in range(n_heads):
            _, _, acc_sc, _, _, p_sc, a_sc = heads[h]
            vt = jnp.concatenate([vt_ref[2 * u, h], vt_ref[2 * u + 1, h]], axis=1)
            acc_sc[...] = acc_sc[...] * a_sc[...] + _dot(vt, p_sc[...])

    def stage_c_diag(u):
        for h in range(n_heads):
            _, _, acc_sc, _, _, p_sc, a_sc = heads[h]
            pv = _dot(vt_ref[2 * u, h], p_sc[0:tk, :])
            pv_late = _dot(vt_ref[2 * u + 1, h], late_queries(p_sc[tk:, :]))
            pv = jnp.concatenate([pv[:, :hq], pv[:, hq:tq] + pv_late[:, :hq],
                                  pv[:, tq:tq + hq], pv[:, tq + hq:] + pv_late[:, hq:]], axis=1)
            acc_sc[...] = acc_sc[...] * a_sc[...] + pv

    def trip(u, biases=(None, None), issue_a=True, issue_c=True):
        if issue_c:
            stage_c(u - 1)
        stage_b()
        if issue_a and biases is None:
            stage_a_diag(u + 1)
        elif issue_a:
            stage_a(u + 1, biases)

    before_diag = (None, prev)
    short = ([], [before_diag], [(None, None), before_diag])

    for n_q, step_biases in enumerate(short):
        @pl.when(qi == n_q)
        def _(step_biases=step_biases):
            n_before = len(step_biases)
            if n_before:
                stage_a(0, step_biases[0])
            else:
                stage_a_diag(0)
            for u in range(n_before + 1):
                next_biases = step_biases[u + 1] if u + 1 < n_before else None
                trip(u, next_biases, issue_a=u < n_before, issue_c=u > 0)
            stage_c_diag(n_before)

    @pl.when(qi >= len(short))
    def _():
        stage_a(0, (None, None))
        trip(0, issue_c=False)

        def plain(u, carry):
            trip(u)
            return carry

        lax.fori_loop(1, qi - 2, plain, 0)
        trip(qi - 2, before_diag)
        trip(qi - 1, None)
        trip(qi, issue_a=False)
        stage_c_diag(qi)

    lam = (jnp.exp(jnp.sum(lam_ref[0:1, :] * lam_ref[1:2, :], axis=-1, keepdims=True))
           - jnp.exp(jnp.sum(lam_ref[2:3, :] * lam_ref[3:4, :], axis=-1, keepdims=True))
           + lambda_init)
    for h in range(n_heads):
        acc_sc = heads[h][2]
        o_maps = acc_sc[0:V_DIM, :] * (1.0 / acc_sc[V_DIM:V_DIM + 1, :])
        o_t = o_maps[:, :tq] - lam * o_maps[:, tq:]
        o_t = o_t * lax.rsqrt(jnp.mean(o_t * o_t, axis=0, keepdims=True) + RMS_EPS)
        o_ref[:, h * V_DIM:(h + 1) * V_DIM] = (
            o_t.T * sg_ref[...] * (1.0 - lambda_init)).astype(o_ref.dtype)


def _attention(q, k, vt, bias_tiles, lam_params, subln_g, lambda_init, bsz):
    tq, tk, hps = ATTN_Q_TILE, ATTN_K_TILE, ATTN_HEADS_PER_STEP
    n_heads = k.shape[1]
    s = k.shape[0] * tk // bsz
    w = n_heads * V_DIM
    q_tiles = s // tq
    col = pltpu.VMEM((1, 2 * tq), _F32)
    per_head_scratch = [
        pltpu.VMEM((V_DIM, 2 * tq), _BF16),
        col,
        pltpu.VMEM((VT_ROWS, 2 * tq), _F32),
        pltpu.VMEM((2 * tk, 2 * tq), _F32),
        col,
        pltpu.VMEM((2 * tk, 2 * tq), _BF16),
        col,
    ]
    return pl.pallas_call(
        functools.partial(_attn_kernel, lambda_init=lambda_init),
        grid=(bsz, n_heads // hps, q_tiles),
        in_specs=[
            pl.BlockSpec((tq // tk, hps, V_DIM, tk), lambda b, h, i: (b * q_tiles + i, h, 0, 0)),
            pl.BlockSpec((s // tk, hps, tk, V_DIM), lambda b, h, i: (b, h, 0, 0)),
            pl.BlockSpec((s // tk, hps, VT_ROWS, tk), lambda b, h, i: (b, h, 0, 0)),
            pl.BlockSpec((hps, 3, tk, 2 * tq), lambda b, h, i: (h, 0, 0, 0)),
            _const_spec((4, HEAD_DIM)),
            _const_spec((1, V_DIM)),
        ],
        out_specs=pl.BlockSpec((None, tq, hps * V_DIM), lambda b, h, i: (b, i, h)),
        out_shape=jax.ShapeDtypeStruct((bsz, s, w), _BF16),
        scratch_shapes=per_head_scratch * hps,
        compiler_params=_params(("arbitrary", "arbitrary", "arbitrary")),
        name="diff_attention",
    )(q, k, vt, bias_tiles, lam_params, subln_g.reshape(1, V_DIM))


def _attn_out_kernel(x_ref, o_ref, g_ref, wz_ref, wout_ref, fg_ref, y_ref, *, final_norm):
    x = x_ref[...]
    h = _rmsnorm_rows(x, g_ref[...]).astype(_BF16)
    z = _dot(h, wz_ref[...])
    y = (o_ref[...] * jax.nn.silu(z)).astype(_BF16)
    x_new = x + _dot(y, wout_ref[...])
    if final_norm:
        x_new = _rmsnorm_rows(x_new, fg_ref[...])
    y_ref[...] = x_new


def _attn_out(x2, o2, g, w_in, w_out, final_g, final_norm):
    n, d = x2.shape
    w = w_out.shape[0]
    tm = ATTN_ROW_TILE
    return pl.pallas_call(
        functools.partial(_attn_out_kernel, final_norm=final_norm),
        grid=(n // tm,),
        in_specs=[
            pl.BlockSpec((tm, d), lambda i: (i, 0)),
            pl.BlockSpec((tm, w), lambda i: (i, 0)),
            _const_spec((1, d)),
            pl.BlockSpec((d, w), lambda i: (0, 3), pipeline_mode=pl.Buffered(1)),
            _const_spec((w, d)),
            _const_spec((1, d)),
        ],
        out_specs=pl.BlockSpec((tm, d), lambda i: (i, 0)),
        out_shape=jax.ShapeDtypeStruct((n, d), _F32),
        compiler_params=_params(("arbitrary",)),
        name="attn_out",
    )(x2, o2, g.reshape(1, d), w_in, w_out, final_g.reshape(1, d))


def _t5_causal_bucket(dist):
    d_safe = jnp.maximum(dist, 1).astype(_F32)
    large = MAX_EXACT + (jnp.log(d_safe / MAX_EXACT) / math.log(REL_MAX_DIST / MAX_EXACT)
                         * (NUM_BUCKETS - MAX_EXACT)).astype(jnp.int32)
    large = jnp.minimum(large, NUM_BUCKETS - 1)
    return jnp.where(dist < MAX_EXACT, dist, large)


def _toeplitz_pair(g_vec, rows, cols):
    period = g_vec.shape[-1]
    base = jnp.stack([jnp.roll(g_vec, r, axis=-1) for r in range(SUBLANES)], axis=-2)
    groups = []
    for a in range(rows // SUBLANES):
        shift = a * SUBLANES
        pieces = []
        for m in range(2):
            bm = base[..., m, :, :]
            pieces += [bm[..., period - shift:], bm[..., :cols - shift]]
        groups.append(jnp.concatenate(pieces, axis=-1))
    out = jnp.stack(groups, axis=-3)
    return out.reshape(out.shape[:-3] + (rows, 2 * cols))


def _bias_tiles(rel_bias, tq, tk):
    period = 2 * (tq + tk)
    delta = jnp.arange(period, dtype=jnp.int32)
    delta = jnp.where(delta >= period // 2, delta - period, delta)
    n_dist = tq + tk
    rel = ((rel_bias[_t5_causal_bucket(jnp.arange(n_dist, dtype=jnp.int32))]
            - rel_bias[NUM_BUCKETS - 1]) * _LOG2E).T

    def vec(offset):
        dist = delta + offset
        vals = rel[:, jnp.clip(dist, 0, n_dist - 1)]
        return jnp.where(dist >= 0, vals, -jnp.inf)

    g_vec = jnp.stack([vec(tk), vec(0), vec(-tk)], axis=1)
    g_vec = g_vec.reshape(N_DIFF_HEADS, 2, 3, period).transpose(0, 2, 1, 3)
    return _toeplitz_pair(g_vec, tk, tq)


def _lambda_init_for(layer_idx):
    return 0.8 - 0.6 * math.exp(-0.3 * layer_idx)


def kernel(x, norm_g, w_in, w_out, conv_w, lambda_q1, lambda_k1, lambda_q2, lambda_k2, subln_g,
           rel_bias, final_g):
    bsz, seq, d = x.shape
    depth = norm_g.shape[0]
    w = w_out.shape[1]
    tq, tk = ATTN_Q_TILE, ATTN_K_TILE
    assert seq % tq == 0 and seq % ROW_TILE == 0 and w == N_DIFF_HEADS * V_DIM
    assert (bsz * seq) % ATTN_ROW_TILE == 0
    assert N_DIFF_HEADS % ATTN_HEADS_PER_STEP == 0
    assert depth % N_MIXERS == 0

    bias_tiles = _bias_tiles(rel_bias, tq, tk)
    x2 = x.reshape(bsz * seq, d)
    for i in range(depth):
        j = i // N_MIXERS
        if i % N_MIXERS == 0:
            x2 = _conv_layer(x2, norm_g[i], w_in[i], w_out[i], conv_w[j], seq)
        else:
            w_in_bf = w_in[i].astype(_BF16)
            q, k, vt = _qkv_proj(x2, norm_g[i], w_in_bf)
            lam_params = jnp.stack([lambda_q1[j], lambda_k1[j], lambda_q2[j], lambda_k2[j]])
            o = _attention(q, k, vt, bias_tiles, lam_params, subln_g[j], _lambda_init_for(i),
                           bsz)
            x2 = _attn_out(x2, o.reshape(bsz * seq, w), norm_g[i], w_in_bf,
                           w_out[i].astype(_BF16), final_g, final_norm=(i == depth - 1))
    return x2.reshape(bsz, seq, d)
```

```python
import functools
import math

import jax
import jax.numpy as jnp
from jax import lax
from jax.experimental import pallas as pl
from jax.experimental.pallas import tpu as pltpu

N_MIXERS = 2
CONV_K = 3
N_DIFF_HEADS = 8
HEAD_DIM = 64
V_DIM = 2 * HEAD_DIM
N_MAPS = 2 * N_DIFF_HEADS
NUM_BUCKETS = 32
MAX_EXACT = NUM_BUCKETS // 2
REL_MAX_DIST = 128
RMS_EPS = 1e-6

SUBLANES = 8
BF16_SUBLANES = 16
ROW_TILE = 512
CONV_SUB_ROWS = 256
ATTN_ROW_TILE = 1024
ATTN_Q_TILE = 512
ATTN_K_TILE = 256
ATTN_HEADS_PER_STEP = 4
VT_ROWS = V_DIM + BF16_SUBLANES
VMEM_LIMIT_BYTES = 56 * 1024 * 1024

assert ATTN_Q_TILE == 2 * ATTN_K_TILE and ATTN_K_TILE >= REL_MAX_DIST
assert ATTN_ROW_TILE % ATTN_K_TILE == 0

_F32 = jnp.float32
_BF16 = jnp.bfloat16
_LOG2E = math.log2(math.e)


def _dot(a, b):
    return jnp.dot(a, b, preferred_element_type=_F32)


def _rmsnorm_rows(x, g):
    return x * lax.rsqrt(jnp.mean(x * x, axis=-1, keepdims=True) + RMS_EPS) * g


def _const_spec(shape):
    return pl.BlockSpec(shape, lambda *_: (0,) * len(shape), pipeline_mode=pl.Buffered(1))


def _params(semantics):
    return pltpu.CompilerParams(dimension_semantics=semantics,
                                vmem_limit_bytes=VMEM_LIMIT_BYTES)


def _shift_rows(v, prev, s):
    r = pltpu.roll(v, s, 0)
    p = pltpu.roll(prev, s, 0)
    row = lax.broadcasted_iota(jnp.int32, prev.shape, 0)
    head = jnp.where(row < s, p, r[:SUBLANES])
    return jnp.concatenate([head, r[SUBLANES:]], axis=0)


def _conv_layer_kernel(x_ref, g_ref, win_ref, wout_ref, cw_ref, o_ref, carry_ref, *,
                       tiles_per_seq, width, sub_rows):
    w = width

    @pl.when(pl.program_id(0) % tiles_per_seq == 0)
    def _():
        carry_ref[...] = jnp.zeros_like(carry_ref)

    prev = carry_ref[...]
    for r in range(x_ref.shape[0] // sub_rows):
        rows = slice(r * sub_rows, (r + 1) * sub_rows)
        x = x_ref[rows, :]
        h = _rmsnorm_rows(x, g_ref[...]).astype(_BF16)
        b_gate = _dot(h, win_ref[:, 0:w])
        c_gate = _dot(h, win_ref[:, w:2 * w])
        u = _dot(h, win_ref[:, 2 * w:3 * w])
        z = _dot(h, win_ref[:, 3 * w:4 * w])
        v = c_gate * u
        conv = (_shift_rows(v, prev, 2) * cw_ref[0:1, :]
                + _shift_rows(v, prev, 1) * cw_ref[1:2, :] + v * cw_ref[2:3, :])
        prev = v[sub_rows - SUBLANES:]
        y = (b_gate * conv * jax.nn.silu(z)).astype(_BF16)
        o_ref[rows, :] = x + _dot(y, wout_ref[...])
    carry_ref[...] = prev


def _conv_layer(x2, g, w_in, w_out, conv_w, seq):
    n, d = x2.shape
    w = w_out.shape[0]
    tm = ROW_TILE
    return pl.pallas_call(
        functools.partial(_conv_layer_kernel, tiles_per_seq=seq // tm, width=w,
                          sub_rows=CONV_SUB_ROWS),
        grid=(n // tm,),
        in_specs=[
            pl.BlockSpec((tm, d), lambda i: (i, 0)),
            _const_spec((1, d)),
            _const_spec((d, 4 * w)),
            _const_spec((w, d)),
            _const_spec((CONV_K, w)),
        ],
        out_specs=pl.BlockSpec((tm, d), lambda i: (i, 0)),
        out_shape=jax.ShapeDtypeStruct((n, d), _F32),
        scratch_shapes=[pltpu.VMEM((SUBLANES, w), _F32)],
        compiler_params=_params(("arbitrary",)),
        name="conv_layer",
    )(x2, g.reshape(1, d), w_in.astype(_BF16), w_out.astype(_BF16), conv_w.T)


def _qkv_kernel(x_ref, g_ref, w_ref, qt_ref, k_ref, vt_ref, *, width):
    w = width
    h = _rmsnorm_rows(x_ref[...], g_ref[...]).astype(_BF16)
    q = _dot(h, w_ref[:, 0:w]) * (HEAD_DIM ** -0.5 * _LOG2E)
    k = _dot(h, w_ref[:, w:2 * w]).astype(_BF16)
    v = _dot(h, w_ref[:, 2 * w:3 * w])
    n_tiles, n_heads, vt_rows, tk = vt_ref.shape
    row = lax.broadcasted_iota(jnp.int32, (vt_rows - V_DIM, tk), 0)
    ones_rows = jnp.where(row == 0, 1.0, 0.0).astype(_BF16)
    for t in range(n_tiles):
        rows = slice(t * tk, (t + 1) * tk)
        for hd in range(n_heads):
            cols = slice(hd * V_DIM, (hd + 1) * V_DIM)
            qt_ref[t, hd] = q[rows, cols].T.astype(_BF16)
            k_ref[t, hd] = k[rows, cols]
            vt_ref[t, hd, 0:V_DIM, :] = v[rows, cols].T.astype(_BF16)
            vt_ref[t, hd, V_DIM:, :] = ones_rows


def _qkv_proj(x2, g, w_in):
    n, d = x2.shape
    w = w_in.shape[1] // 4
    tm, tk = ATTN_ROW_TILE, ATTN_K_TILE
    n_heads = w // V_DIM
    out = jax.ShapeDtypeStruct((n // tk, n_heads, tk, V_DIM), _BF16)
    head_tiles = pl.BlockSpec((tm // tk, n_heads, tk, V_DIM), lambda i: (i, 0, 0, 0))
    return pl.pallas_call(
        functools.partial(_qkv_kernel, width=w),
        grid=(n // tm,),
        in_specs=[
            pl.BlockSpec((tm, d), lambda i: (i, 0)),
            _const_spec((1, d)),
            _const_spec((d, 3 * w)),
        ],
        out_specs=[pl.BlockSpec((tm // tk, n_heads, V_DIM, tk), lambda i: (i, 0, 0, 0)), head_tiles,
                   pl.BlockSpec((tm // tk, n_heads, VT_ROWS, tk), lambda i: (i, 0, 0, 0))],
        out_shape=[jax.ShapeDtypeStruct((n // tk, n_heads, V_DIM, tk), _BF16), out,
                   jax.ShapeDtypeStruct((n // tk, n_heads, VT_ROWS, tk), _BF16)],
        compiler_params=_params(("arbitrary",)),
        name="qkv_proj",
    )(x2, g.reshape(1, d), w_in)


def _attn_kernel(qt_ref, k_ref, vt_ref, bias_ref, lam_ref, sg_ref, o_ref, *scratch, lambda_init):
    n_heads = qt_ref.shape[1]
    per_head = len(scratch) // n_heads
    heads = tuple(scratch[h * per_head:(h + 1) * per_head] for h in range(n_heads))
    tk = vt_ref.shape[-1]
    tq = qt_ref.shape[0] * tk
    qi = pl.program_id(2)

    for h in range(n_heads):
        qq_sc, m_sc, acc_sc = heads[h][:3]
        q_t = jnp.concatenate([qt_ref[t, h] for t in range(qt_ref.shape[0])], axis=1)
        row = lax.broadcasted_iota(jnp.int32, q_t.shape, 0)
        zero = jnp.zeros_like(q_t)
        qq_sc[:, 0:tq] = jnp.where(row < HEAD_DIM, q_t, zero)
        qq_sc[:, tq:2 * tq] = jnp.where(row >= HEAD_DIM, q_t, zero)
        m_sc[...] = jnp.full_like(m_sc, -jnp.inf)
        acc_sc[...] = jnp.zeros_like(acc_sc)

    prev, diag0, diag1 = 0, 1, 2

    def stage_a(u, biases):
        for h in range(n_heads):
            qq_sc, _, _, s_sc, t_sc, _, _ = heads[h]
            keys = k_ref[pl.ds(2 * u, 2), h].reshape(2 * tk, V_DIM)
            s = _dot(keys, qq_sc[...])
            col_max = None
            for half, bias in enumerate(biases):
                tile = s[half * tk:(half + 1) * tk]
                if bias is not None:
                    tile = tile + bias_ref[h, bias]
                s_sc[half * tk:(half + 1) * tk, :] = tile
                tile_max = jnp.max(tile, axis=0, keepdims=True)
                col_max = tile_max if col_max is None else jnp.maximum(col_max, tile_max)
            t_sc[...] = col_max

    hq = tq // 2

    def late_queries(x):
        return jnp.concatenate([x[:, hq:tq], x[:, tq + hq:]], axis=1)

    def stage_a_diag(u):
        for h in range(n_heads):
            qq_sc, _, _, s_sc, t_sc, _, _ = heads[h]
            s_lo = _dot(k_ref[2 * u, h], qq_sc[...]) + bias_ref[h, diag0]
            s_hi = _dot(k_ref[2 * u + 1, h], late_queries(qq_sc[...]))
            s_hi = s_hi + late_queries(bias_ref[h, diag1])
            masked = jnp.full((tk, hq), -jnp.inf, _F32)
            s_sc[0:tk, :] = s_lo
            s_sc[tk:, :] = jnp.concatenate([masked, s_hi[:, :hq], masked, s_hi[:, hq:]], axis=1)
            max_lo = jnp.max(s_lo, axis=0, keepdims=True)
            max_hi = jnp.max(s_hi, axis=0, keepdims=True)
            t_sc[...] = jnp.concatenate(
                [max_lo[:, :hq], jnp.maximum(max_lo[:, hq:tq], max_hi[:, :hq]),
                 max_lo[:, tq:tq + hq], jnp.maximum(max_lo[:, tq + hq:], max_hi[:, hq:])], axis=1)

    def stage_b():
        for h in range(n_heads):
            _, m_sc, _, s_sc, t_sc, p_sc, a_sc = heads[h]
            m_prev = m_sc[...]
            m_new = jnp.maximum(m_prev, t_sc[...])
            a_sc[...] = jnp.exp2(m_prev - m_new)
            p_sc[...] = jnp.exp2(s_sc[...] - m_new).astype(_BF16)
            m_sc[...] = m_new

    def stage_c(u):
        for h in range(n_heads):
            _, _, acc_sc, _, _, p_sc, a_sc = heads[h]
            vt = jnp.concatenate([vt_ref[2 * u, h], vt_ref[2 * u + 1, h]], axis=1)
            acc_sc[...] = acc_sc[...] * a_sc[...] + _dot(vt, p_sc[...])

    def stage_c_diag(u):
        for h in range(n_heads):
            _, _, acc_sc, _, _, p_sc, a_sc = heads[h]
            pv = _dot(vt_ref[2 * u, h], p_sc[0:tk, :])
            pv_late = _dot(vt_ref[2 * u + 1, h], late_queries(p_sc[tk:, :]))
            pv = jnp.concatenate([pv[:, :hq], pv[:, hq:tq] + pv_late[:, :hq],
                                  pv[:, tq:tq + hq], pv[:, tq + hq:] + pv_late[:, hq:]], axis=1)
            acc_sc[...] = acc_sc[...] * a_sc[...] + pv

    def trip(u, biases=(None, None), issue_a=True, issue_c=True):
        if issue_c:
            stage_c(u - 1)
        stage_b()
        if issue_a and biases is None:
            stage_a_diag(u + 1)
        elif issue_a:
            stage_a(u + 1, biases)

    before_diag = (None, prev)
    short = ([], [before_diag], [(None, None), before_diag])

    for n_q, step_biases in enumerate(short):
        @pl.when(qi == n_q)
        def _(step_biases=step_biases):
            n_before = len(step_biases)
            if n_before:
                stage_a(0, step_biases[0])
            else:
                stage_a_diag(0)
            for u in range(n_before + 1):
                next_biases = step_biases[u + 1] if u + 1 < n_before else None
                trip(u, next_biases, issue_a=u < n_before, issue_c=u > 0)
            stage_c_diag(n_before)

    @pl.when(qi >= len(short))
    def _():
        stage_a(0, (None, None))
        trip(0, issue_c=False)

        def plain(u, carry):
            trip(u)
            return carry

        lax.fori_loop(1, qi - 2, plain, 0)
        trip(qi - 2, before_diag)
        trip(qi - 1, None)
        trip(qi, issue_a=False)
        stage_c_diag(qi)

    lam = (jnp.exp(jnp.sum(lam_ref[0:1, :] * lam_ref[1:2, :], axis=-1, keepdims=True))
           - jnp.exp(jnp.sum(lam_ref[2:3, :] * lam_ref[3:4, :], axis=-1, keepdims=True))
           + lambda_init)
    for h in range(n_heads):
        acc_sc = heads[h][2]
        o_maps = acc_sc[0:V_DIM, :] * (1.0 / acc_sc[V_DIM:V_DIM + 1, :])
        o_t = o_maps[:, :tq] - lam * o_maps[:, tq:]
        o_t = o_t * lax.rsqrt(jnp.mean(o_t * o_t, axis=0, keepdims=True) + RMS_EPS)
        o_ref[:, h * V_DIM:(h + 1) * V_DIM] = (
            o_t.T * sg_ref[...] * (1.0 - lambda_init)).astype(o_ref.dtype)


def _attention(q, k, vt, bias_tiles, lam_params, subln_g, lambda_init, bsz):
    tq, tk, hps = ATTN_Q_TILE, ATTN_K_TILE, ATTN_HEADS_PER_STEP
    n_heads = k.shape[1]
    s = k.shape[0] * tk // bsz
    w = n_heads * V_DIM
    q_tiles = s // tq
    col = pltpu.VMEM((1, 2 * tq), _F32)
    per_head_scratch = [
        pltpu.VMEM((V_DIM, 2 * tq), _BF16),
        col,
        pltpu.VMEM((VT_ROWS, 2 * tq), _F32),
        pltpu.VMEM((2 * tk, 2 * tq), _F32),
        col,
        pltpu.VMEM((2 * tk, 2 * tq), _BF16),
        col,
    ]
    return pl.pallas_call(
        functools.partial(_attn_kernel, lambda_init=lambda_init),
        grid=(bsz, n_heads // hps, q_tiles),
        in_specs=[
            pl.BlockSpec((tq // tk, hps, V_DIM, tk), lambda b, h, i: (b * q_tiles + i, h, 0, 0)),
            pl.BlockSpec((s // tk, hps, tk, V_DIM), lambda b, h, i: (b, h, 0, 0),
                         pipeline_mode=pl.Buffered(1)),
            pl.BlockSpec((s // tk, hps, VT_ROWS, tk), lambda b, h, i: (b, h, 0, 0),
                         pipeline_mode=pl.Buffered(1)),
            pl.BlockSpec((hps, 3, tk, 2 * tq), lambda b, h, i: (h, 0, 0, 0),
                         pipeline_mode=pl.Buffered(1)),
            _const_spec((4, HEAD_DIM)),
            _const_spec((1, V_DIM)),
        ],
        out_specs=pl.BlockSpec((None, tq, hps * V_DIM), lambda b, h, i: (b, i, h)),
        out_shape=jax.ShapeDtypeStruct((bsz, s, w), _BF16),
        scratch_shapes=per_head_scratch * hps,
        compiler_params=_params(("arbitrary", "arbitrary", "arbitrary")),
        name="diff_attention",
    )(q, k, vt, bias_tiles, lam_params, subln_g.reshape(1, V_DIM))


def _attn_out_kernel(x_ref, o_ref, g_ref, wz_ref, wout_ref, fg_ref, y_ref, *, final_norm):
    x = x_ref[...]
    h = _rmsnorm_rows(x, g_ref[...]).astype(_BF16)
    z = _dot(h, wz_ref[...])
    y = (o_ref[...] * jax.nn.silu(z)).astype(_BF16)
    x_new = x + _dot(y, wout_ref[...])
    if final_norm:
        x_new = _rmsnorm_rows(x_new, fg_ref[...])
    y_ref[...] = x_new


def _attn_out(x2, o2, g, w_in, w_out, final_g, final_norm):
    n, d = x2.shape
    w = w_out.shape[0]
    tm = ATTN_ROW_TILE
    return pl.pallas_call(
        functools.partial(_attn_out_kernel, final_norm=final_norm),
        grid=(n // tm,),
        in_specs=[
            pl.BlockSpec((tm, d), lambda i: (i, 0)),
            pl.BlockSpec((tm, w), lambda i: (i, 0)),
            _const_spec((1, d)),
            pl.BlockSpec((d, w), lambda i: (0, 3), pipeline_mode=pl.Buffered(1)),
            _const_spec((w, d)),
            _const_spec((1, d)),
        ],
        out_specs=pl.BlockSpec((tm, d), lambda i: (i, 0)),
        out_shape=jax.ShapeDtypeStruct((n, d), _F32),
        compiler_params=_params(("arbitrary",)),
        name="attn_out",
    )(x2, o2, g.reshape(1, d), w_in, w_out, final_g.reshape(1, d))


def _t5_causal_bucket(dist):
    d_safe = jnp.maximum(dist, 1).astype(_F32)
    large = MAX_EXACT + (jnp.log(d_safe / MAX_EXACT) / math.log(REL_MAX_DIST / MAX_EXACT)
                         * (NUM_BUCKETS - MAX_EXACT)).astype(jnp.int32)
    large = jnp.minimum(large, NUM_BUCKETS - 1)
    return jnp.where(dist < MAX_EXACT, dist, large)


def _toeplitz_pair(g_vec, rows, cols):
    period = g_vec.shape[-1]
    base = jnp.stack([jnp.roll(g_vec, r, axis=-1) for r in range(SUBLANES)], axis=-2)
    groups = []
    for a in range(rows // SUBLANES):
        shift = a * SUBLANES
        pieces = []
        for m in range(2):
            bm = base[..., m, :, :]
            pieces += [bm[..., period - shift:], bm[..., :cols - shift]]
        groups.append(jnp.concatenate(pieces, axis=-1))
    out = jnp.stack(groups, axis=-3)
    return out.reshape(out.shape[:-3] + (rows, 2 * cols))


def _bias_tiles(rel_bias, tq, tk):
    period = 2 * (tq + tk)
    delta = jnp.arange(period, dtype=jnp.int32)
    delta = jnp.where(delta >= period // 2, delta - period, delta)
    n_dist = tq + tk
    rel = ((rel_bias[_t5_causal_bucket(jnp.arange(n_dist, dtype=jnp.int32))]
            - rel_bias[NUM_BUCKETS - 1]) * _LOG2E).T

    def vec(offset):
        dist = delta + offset
        vals = rel[:, jnp.clip(dist, 0, n_dist - 1)]
        return jnp.where(dist >= 0, vals, -jnp.inf)

    g_vec = jnp.stack([vec(tk), vec(0), vec(-tk)], axis=1)
    g_vec = g_vec.reshape(N_DIFF_HEADS, 2, 3, period).transpose(0, 2, 1, 3)
    return _toeplitz_pair(g_vec, tk, tq)


def _lambda_init_for(layer_idx):
    return 0.8 - 0.6 * math.exp(-0.3 * layer_idx)


def kernel(x, norm_g, w_in, w_out, conv_w, lambda_q1, lambda_k1, lambda_q2, lambda_k2, subln_g,
           rel_bias, final_g):
    bsz, seq, d = x.shape
    depth = norm_g.shape[0]
    w = w_out.shape[1]
    tq, tk = ATTN_Q_TILE, ATTN_K_TILE
    assert seq % tq == 0 and seq % ROW_TILE == 0 and w == N_DIFF_HEADS * V_DIM
    assert (bsz * seq) % ATTN_ROW_TILE == 0
    assert N_DIFF_HEADS % ATTN_HEADS_PER_STEP == 0
    assert depth % N_MIXERS == 0

    bias_tiles = _bias_tiles(rel_bias, tq, tk)
    x2 = x.reshape(bsz * seq, d)
    for i in range(depth):
        j = i // N_MIXERS
        if i % N_MIXERS == 0:
            x2 = _conv_layer(x2, norm_g[i], w_in[i], w_out[i], conv_w[j], seq)
        else:
            w_in_bf = w_in[i].astype(_BF16)
            q, k, vt = _qkv_proj(x2, norm_g[i], w_in_bf)
            lam_params = jnp.stack([lambda_q1[j], lambda_k1[j], lambda_q2[j], lambda_k2[j]])
            o = _attention(q, k, vt, bias_tiles, lam_params, subln_g[j], _lambda_init_for(i),
                           bsz)
            x2 = _attn_out(x2, o.reshape(bsz * seq, w), norm_g[i], w_in_bf,
                           w_out[i].astype(_BF16), final_g, final_norm=(i == depth - 1))
    return x2.reshape(bsz, seq, d)
```

```python
import functools
import math

import jax
import jax.numpy as jnp
from jax import lax
from jax.experimental import pallas as pl
from jax.experimental.pallas import tpu as pltpu

N_MIXERS = 2
CONV_K = 3
N_DIFF_HEADS = 8
HEAD_DIM = 64
V_DIM = 2 * HEAD_DIM
N_MAPS = 2 * N_DIFF_HEADS
NUM_BUCKETS = 32
MAX_EXACT = NUM_BUCKETS // 2
REL_MAX_DIST = 128
RMS_EPS = 1e-6

SUBLANES = 8
BF16_SUBLANES = 16
ROW_TILE = 1024
CONV_SUB_ROWS = 256
ATTN_ROW_TILE = 1024
ATTN_Q_TILE = 512
ATTN_K_TILE = 256
ATTN_HEADS_PER_STEP = 4
VT_ROWS = V_DIM + BF16_SUBLANES
VMEM_LIMIT_BYTES = 56 * 1024 * 1024

assert ATTN_Q_TILE == 2 * ATTN_K_TILE and ATTN_K_TILE >= REL_MAX_DIST
assert ATTN_ROW_TILE % ATTN_K_TILE == 0

_F32 = jnp.float32
_BF16 = jnp.bfloat16
_LOG2E = math.log2(math.e)


def _dot(a, b):
    return jnp.dot(a, b, preferred_element_type=_F32)


def _rmsnorm_rows(x, g):
    return x * lax.rsqrt(jnp.mean(x * x, axis=-1, keepdims=True) + RMS_EPS) * g


def _const_spec(shape):
    return pl.BlockSpec(shape, lambda *_: (0,) * len(shape), pipeline_mode=pl.Buffered(1))


def _params(semantics):
    return pltpu.CompilerParams(dimension_semantics=semantics,
                                vmem_limit_bytes=VMEM_LIMIT_BYTES)


def _shift_rows(v, prev, s):
    r = pltpu.roll(v, s, 0)
    p = pltpu.roll(prev, s, 0)
    row = lax.broadcasted_iota(jnp.int32, prev.shape, 0)
    head = jnp.where(row < s, p, r[:SUBLANES])
    return jnp.concatenate([head, r[SUBLANES:]], axis=0)


def _conv_layer_kernel(x_ref, g_ref, win_ref, wout_ref, cw_ref, o_ref, carry_ref, *,
                       tiles_per_seq, width, sub_rows):
    w = width

    @pl.when(pl.program_id(0) % tiles_per_seq == 0)
    def _():
        carry_ref[...] = jnp.zeros_like(carry_ref)

    prev = carry_ref[...]
    for r in range(x_ref.shape[0] // sub_rows):
        rows = slice(r * sub_rows, (r + 1) * sub_rows)
        x = x_ref[rows, :]
        h = _rmsnorm_rows(x, g_ref[...]).astype(_BF16)
        b_gate = _dot(h, win_ref[:, 0:w])
        c_gate = _dot(h, win_ref[:, w:2 * w])
        u = _dot(h, win_ref[:, 2 * w:3 * w])
        z = _dot(h, win_ref[:, 3 * w:4 * w])
        v = c_gate * u
        conv = (_shift_rows(v, prev, 2) * cw_ref[0:1, :]
                + _shift_rows(v, prev, 1) * cw_ref[1:2, :] + v * cw_ref[2:3, :])
        prev = v[sub_rows - SUBLANES:]
        y = (b_gate * conv * jax.nn.silu(z)).astype(_BF16)
        o_ref[rows, :] = x + _dot(y, wout_ref[...])
    carry_ref[...] = prev


def _conv_layer(x2, g, w_in, w_out, conv_w, seq):
    n, d = x2.shape
    w = w_out.shape[0]
    tm = ROW_TILE
    return pl.pallas_call(
        functools.partial(_conv_layer_kernel, tiles_per_seq=seq // tm, width=w,
                          sub_rows=CONV_SUB_ROWS),
        grid=(n // tm,),
        in_specs=[
            pl.BlockSpec((tm, d), lambda i: (i, 0)),
            _const_spec((1, d)),
            _const_spec((d, 4 * w)),
            _const_spec((w, d)),
            _const_spec((CONV_K, w)),
        ],
        out_specs=pl.BlockSpec((tm, d), lambda i: (i, 0)),
        out_shape=jax.ShapeDtypeStruct((n, d), _F32),
        scratch_shapes=[pltpu.VMEM((SUBLANES, w), _F32)],
        compiler_params=_params(("arbitrary",)),
        name="conv_layer",
    )(x2, g.reshape(1, d), w_in.astype(_BF16), w_out.astype(_BF16), conv_w.T)


def _qkv_kernel(x_ref, g_ref, w_ref, qt_ref, k_ref, vt_ref, *, width):
    w = width
    h = _rmsnorm_rows(x_ref[...], g_ref[...]).astype(_BF16)
    q = _dot(h, w_ref[:, 0:w]) * (HEAD_DIM ** -0.5 * _LOG2E)
    k = _dot(h, w_ref[:, w:2 * w]).astype(_BF16)
    v = _dot(h, w_ref[:, 2 * w:3 * w])
    n_tiles, n_heads, vt_rows, tk = vt_ref.shape
    row = lax.broadcasted_iota(jnp.int32, (vt_rows - V_DIM, tk), 0)
    ones_rows = jnp.where(row == 0, 1.0, 0.0).astype(_BF16)
    for t in range(n_tiles):
        rows = slice(t * tk, (t + 1) * tk)
        for hd in range(n_heads):
            cols = slice(hd * V_DIM, (hd + 1) * V_DIM)
            qt_ref[t, hd] = q[rows, cols].T.astype(_BF16)
            k_ref[t, hd] = k[rows, cols]
            vt_ref[t, hd, 0:V_DIM, :] = v[rows, cols].T.astype(_BF16)
            vt_ref[t, hd, V_DIM:, :] = ones_rows


def _qkv_proj(x2, g, w_in):
    n, d = x2.shape
    w = w_in.shape[1] // 4
    tm, tk = ATTN_ROW_TILE, ATTN_K_TILE
    n_heads = w // V_DIM
    out = jax.ShapeDtypeStruct((n // tk, n_heads, tk, V_DIM), _BF16)
    head_tiles = pl.BlockSpec((tm // tk, n_heads, tk, V_DIM), lambda i: (i, 0, 0, 0))
    return pl.pallas_call(
        functools.partial(_qkv_kernel, width=w),
        grid=(n // tm,),
        in_specs=[
            pl.BlockSpec((tm, d), lambda i: (i, 0)),
            _const_spec((1, d)),
            _const_spec((d, 3 * w)),
        ],
        out_specs=[pl.BlockSpec((tm // tk, n_heads, V_DIM, tk), lambda i: (i, 0, 0, 0)), head_tiles,
                   pl.BlockSpec((tm // tk, n_heads, VT_ROWS, tk), lambda i: (i, 0, 0, 0))],
        out_shape=[jax.ShapeDtypeStruct((n // tk, n_heads, V_DIM, tk), _BF16), out,
                   jax.ShapeDtypeStruct((n // tk, n_heads, VT_ROWS, tk), _BF16)],
        compiler_params=_params(("arbitrary",)),
        name="qkv_proj",
    )(x2, g.reshape(1, d), w_in)


def _attn_kernel(qt_ref, k_ref, vt_ref, bias_ref, lam_ref, sg_ref, o_ref, *scratch, lambda_init):
    n_heads = qt_ref.shape[1]
    per_head = len(scratch) // n_heads
    heads = tuple(scratch[h * per_head:(h + 1) * per_head] for h in range(n_heads))
    tk = vt_ref.shape[-1]
    tq = qt_ref.shape[0] * tk
    qi = pl.program_id(2)

    for h in range(n_heads):
        qq_sc, m_sc, acc_sc = heads[h][:3]
        q_t = jnp.concatenate([qt_ref[t, h] for t in range(qt_ref.shape[0])], axis=1)
        row = lax.broadcasted_iota(jnp.int32, q_t.shape, 0)
        zero = jnp.zeros_like(q_t)
        qq_sc[:, 0:tq] = jnp.where(row < HEAD_DIM, q_t, zero)
        qq_sc[:, tq:2 * tq] = jnp.where(row >= HEAD_DIM, q_t, zero)
        m_sc[...] = jnp.full_like(m_sc, -jnp.inf)
        acc_sc[...] = jnp.zeros_like(acc_sc)

    prev, diag0, diag1 = 0, 1, 2

    def stage_a(u, biases):
        for h in range(n_heads):
            qq_sc, _, _, s_sc, t_sc, _, _ = heads[h]
            keys = k_ref[pl.ds(2 * u, 2), h].reshape(2 * tk, V_DIM)
            s = _dot(keys, qq_sc[...])
            col_max = None
            for half, bias in enumerate(biases):
                tile = s[half * tk:(half + 1) * tk]
                if bias is not None:
                    tile = tile + bias_ref[h, bias]
                s_sc[half * tk:(half + 1) * tk, :] = tile
                tile_max = jnp.max(tile, axis=0, keepdims=True)
                col_max = tile_max if col_max is None else jnp.maximum(col_max, tile_max)
            t_sc[...] = col_max

    hq = tq // 2

    def late_queries(x):
        return jnp.concatenate([x[:, hq:tq], x[:, tq + hq:]], axis=1)

    def stage_a_diag(u):
        for h in range(n_heads):
            qq_sc, _, _, s_sc, t_sc, _, _ = heads[h]
            s_lo = _dot(k_ref[2 * u, h], qq_sc[...]) + bias_ref[h, diag0]
            s_hi = _dot(k_ref[2 * u + 1, h], late_queries(qq_sc[...]))
            s_hi = s_hi + late_queries(bias_ref[h, diag1])
            masked = jnp.full((tk, hq), -jnp.inf, _F32)
            s_sc[0:tk, :] = s_lo
            s_sc[tk:, :] = jnp.concatenate([masked, s_hi[:, :hq], masked, s_hi[:, hq:]], axis=1)
            max_lo = jnp.max(s_lo, axis=0, keepdims=True)
            max_hi = jnp.max(s_hi, axis=0, keepdims=True)
            t_sc[...] = jnp.concatenate(
                [max_lo[:, :hq], jnp.maximum(max_lo[:, hq:tq], max_hi[:, :hq]),
                 max_lo[:, tq:tq + hq], jnp.maximum(max_lo[:, tq + hq:], max_hi[:, hq:])], axis=1)

    def stage_b():
        for h in range(n_heads):
            _, m_sc, _, s_sc, t_sc, p_sc, a_sc = heads[h]
            m_prev = m_sc[...]
            m_new = jnp.maximum(m_prev, t_sc[...])
            a_sc[...] = jnp.exp2(m_prev - m_new)
            p_sc[...] = jnp.exp2(s_sc[...] - m_new).astype(_BF16)
            m_sc[...] = m_new

    def stage_c(u):
        for h in range(n_heads):
            _, _, acc_sc, _, _, p_sc, a_sc = heads[h]
            vt = jnp.concatenate([vt_ref[2 * u, h], vt_ref[2 * u + 1, h]], axis=1)
            acc_sc[...] = acc_sc[...] * a_sc[...] + _dot(vt, p_sc[...])

    def stage_c_diag(u):
        for h in range(n_heads):
            _, _, acc_sc, _, _, p_sc, a_sc = heads[h]
            pv = _dot(vt_ref[2 * u, h], p_sc[0:tk, :])
            pv_late = _dot(vt_ref[2 * u + 1, h], late_queries(p_sc[tk:, :]))
            pv = jnp.concatenate([pv[:, :hq], pv[:, hq:tq] + pv_late[:, :hq],
                                  pv[:, tq:tq + hq], pv[:, tq + hq:] + pv_late[:, hq:]], axis=1)
            acc_sc[...] = acc_sc[...] * a_sc[...] + pv

    def trip(u, biases=(None, None), issue_a=True, issue_c=True):
        if issue_c:
            stage_c(u - 1)
        stage_b()
        if issue_a and biases is None:
            stage_a_diag(u + 1)
        elif issue_a:
            stage_a(u + 1, biases)

    before_diag = (None, prev)
    short = ([], [before_diag], [(None, None), before_diag])

    for n_q, step_biases in enumerate(short):
        @pl.when(qi == n_q)
        def _(step_biases=step_biases):
            n_before = len(step_biases)
            if n_before:
                stage_a(0, step_biases[0])
            else:
                stage_a_diag(0)
            for u in range(n_before + 1):
                next_biases = step_biases[u + 1] if u + 1 < n_before else None
                trip(u, next_biases, issue_a=u < n_before, issue_c=u > 0)
            stage_c_diag(n_before)

    @pl.when(qi >= len(short))
    def _():
        stage_a(0, (None, None))
        trip(0, issue_c=False)

        def plain(u, carry):
            trip(u)
            return carry

        lax.fori_loop(1, qi - 2, plain, 0)
        trip(qi - 2, before_diag)
        trip(qi - 1, None)
        trip(qi, issue_a=False)
        stage_c_diag(qi)

    lam = (jnp.exp(jnp.sum(lam_ref[0:1, :] * lam_ref[1:2, :], axis=-1, keepdims=True))
           - jnp.exp(jnp.sum(lam_ref[2:3, :] * lam_ref[3:4, :], axis=-1, keepdims=True))
           + lambda_init)
    for h in range(n_heads):
        acc_sc = heads[h][2]
        o_maps = acc_sc[0:V_DIM, :] * (1.0 / acc_sc[V_DIM:V_DIM + 1, :])
        o_t = o_maps[:, :tq] - lam * o_maps[:, tq:]
        o_t = o_t * lax.rsqrt(jnp.mean(o_t * o_t, axis=0, keepdims=True) + RMS_EPS)
        o_ref[:, h * V_DIM:(h + 1) * V_DIM] = (
            o_t.T * sg_ref[...] * (1.0 - lambda_init)).astype(o_ref.dtype)


def _attention(q, k, vt, bias_tiles, lam_params, subln_g, lambda_init, bsz):
    tq, tk, hps = ATTN_Q_TILE, ATTN_K_TILE, ATTN_HEADS_PER_STEP
    n_heads = k.shape[1]
    s = k.shape[0] * tk // bsz
    w = n_heads * V_DIM
    q_tiles = s // tq
    col = pltpu.VMEM((1, 2 * tq), _F32)
    per_head_scratch = [
        pltpu.VMEM((V_DIM, 2 * tq), _BF16),
        col,
        pltpu.VMEM((VT_ROWS, 2 * tq), _F32),
        pltpu.VMEM((2 * tk, 2 * tq), _F32),
        col,
        pltpu.VMEM((2 * tk, 2 * tq), _BF16),
        col,
    ]
    return pl.pallas_call(
        functools.partial(_attn_kernel, lambda_init=lambda_init),
        grid=(n_heads // hps, bsz, q_tiles),
        in_specs=[
            pl.BlockSpec((tq // tk, hps, V_DIM, tk), lambda h, b, i: (b * q_tiles + i, h, 0, 0)),
            pl.BlockSpec((s // tk, hps, tk, V_DIM), lambda h, b, i: (b, h, 0, 0),
                         pipeline_mode=pl.Buffered(1)),
            pl.BlockSpec((s // tk, hps, VT_ROWS, tk), lambda h, b, i: (b, h, 0, 0),
                         pipeline_mode=pl.Buffered(1)),
            pl.BlockSpec((hps, 3, tk, 2 * tq), lambda h, b, i: (h, 0, 0, 0),
                         pipeline_mode=pl.Buffered(1)),
            _const_spec((4, HEAD_DIM)),
            _const_spec((1, V_DIM)),
        ],
        out_specs=pl.BlockSpec((None, tq, hps * V_DIM), lambda h, b, i: (b, i, h)),
        out_shape=jax.ShapeDtypeStruct((bsz, s, w), _BF16),
        scratch_shapes=per_head_scratch * hps,
        compiler_params=_params(("arbitrary", "arbitrary", "arbitrary")),
        name="diff_attention",
    )(q, k, vt, bias_tiles, lam_params, subln_g.reshape(1, V_DIM))


def _attn_out_kernel(x_ref, o_ref, g_ref, wz_ref, wout_ref, fg_ref, y_ref, *, final_norm):
    x = x_ref[...]
    h = _rmsnorm_rows(x, g_ref[...]).astype(_BF16)
    z = _dot(h, wz_ref[...])
    y = (o_ref[...] * jax.nn.silu(z)).astype(_BF16)
    x_new = x + _dot(y, wout_ref[...])
    if final_norm:
        x_new = _rmsnorm_rows(x_new, fg_ref[...])
    y_ref[...] = x_new


def _attn_out(x2, o2, g, w_in, w_out, final_g, final_norm):
    n, d = x2.shape
    w = w_out.shape[0]
    tm = ATTN_ROW_TILE
    return pl.pallas_call(
        functools.partial(_attn_out_kernel, final_norm=final_norm),
        grid=(n // tm,),
        in_specs=[
            pl.BlockSpec((tm, d), lambda i: (i, 0)),
            pl.BlockSpec((tm, w), lambda i: (i, 0)),
            _const_spec((1, d)),
            pl.BlockSpec((d, w), lambda i: (0, 3), pipeline_mode=pl.Buffered(1)),
            _const_spec((w, d)),
            _const_spec((1, d)),
        ],
        out_specs=pl.BlockSpec((tm, d), lambda i: (i, 0)),
        out_shape=jax.ShapeDtypeStruct((n, d), _F32),
        compiler_params=_params(("arbitrary",)),
        name="attn_out",
    )(x2, o2, g.reshape(1, d), w_in, w_out, final_g.reshape(1, d))


def _t5_causal_bucket(dist):
    d_safe = jnp.maximum(dist, 1).astype(_F32)
    large = MAX_EXACT + (jnp.log(d_safe / MAX_EXACT) / math.log(REL_MAX_DIST / MAX_EXACT)
                         * (NUM_BUCKETS - MAX_EXACT)).astype(jnp.int32)
    large = jnp.minimum(large, NUM_BUCKETS - 1)
    return jnp.where(dist < MAX_EXACT, dist, large)


def _toeplitz_pair(g_vec, rows, cols):
    period = g_vec.shape[-1]
    base = jnp.stack([jnp.roll(g_vec, r, axis=-1) for r in range(SUBLANES)], axis=-2)
    groups = []
    for a in range(rows // SUBLANES):
        shift = a * SUBLANES
        pieces = []
        for m in range(2):
            bm = base[..., m, :, :]
            pieces += [bm[..., period - shift:], bm[..., :cols - shift]]
        groups.append(jnp.concatenate(pieces, axis=-1))
    out = jnp.stack(groups, axis=-3)
    return out.reshape(out.shape[:-3] + (rows, 2 * cols))


def _bias_tiles(rel_bias, tq, tk):
    period = 2 * (tq + tk)
    delta = jnp.arange(period, dtype=jnp.int32)
    delta = jnp.where(delta >= period // 2, delta - period, delta)
    n_dist = tq + tk
    rel = ((rel_bias[_t5_causal_bucket(jnp.arange(n_dist, dtype=jnp.int32))]
            - rel_bias[NUM_BUCKETS - 1]) * _LOG2E).T

    def vec(offset):
        dist = delta + offset
        vals = rel[:, jnp.clip(dist, 0, n_dist - 1)]
        return jnp.where(dist >= 0, vals, -jnp.inf)

    g_vec = jnp.stack([vec(tk), vec(0), vec(-tk)], axis=1)
    g_vec = g_vec.reshape(N_DIFF_HEADS, 2, 3, period).transpose(0, 2, 1, 3)
    return _toeplitz_pair(g_vec, tk, tq)


def _lambda_init_for(layer_idx):
    return 0.8 - 0.6 * math.exp(-0.3 * layer_idx)


def kernel(x, norm_g, w_in, w_out, conv_w, lambda_q1, lambda_k1, lambda_q2, lambda_k2, subln_g,
           rel_bias, final_g):
    bsz, seq, d = x.shape
    depth = norm_g.shape[0]
    w = w_out.shape[1]
    tq, tk = ATTN_Q_TILE, ATTN_K_TILE
    assert seq % tq == 0 and seq % ROW_TILE == 0 and w == N_DIFF_HEADS * V_DIM
    assert (bsz * seq) % ATTN_ROW_TILE == 0
    assert N_DIFF_HEADS % ATTN_HEADS_PER_STEP == 0
    assert depth % N_MIXERS == 0

    bias_tiles = _bias_tiles(rel_bias, tq, tk)
    x2 = x.reshape(bsz * seq, d)
    for i in range(depth):
        j = i // N_MIXERS
        if i % N_MIXERS == 0:
            x2 = _conv_layer(x2, norm_g[i], w_in[i], w_out[i], conv_w[j], seq)
        else:
            w_in_bf = w_in[i].astype(_BF16)
            q, k, vt = _qkv_proj(x2, norm_g[i], w_in_bf)
            lam_params = jnp.stack([lambda_q1[j], lambda_k1[j], lambda_q2[j], lambda_k2[j]])
            o = _attention(q, k, vt, bias_tiles, lam_params, subln_g[j], _lambda_init_for(i),
                           bsz)
            x2 = _attn_out(x2, o.reshape(bsz * seq, w), norm_g[i], w_in_bf,
                           w_out[i].astype(_BF16), final_g, final_norm=(i == depth - 1))
    return x2.reshape(bsz, seq, d)
```

```python
import functools
import math

import jax
import jax.numpy as jnp
from jax import lax
from jax.experimental import pallas as pl
from jax.experimental.pallas import tpu as pltpu

N_MIXERS = 2
CONV_K = 3
N_DIFF_HEADS = 8
HEAD_DIM = 64
V_DIM = 2 * HEAD_DIM
N_MAPS = 2 * N_DIFF_HEADS
NUM_BUCKETS = 32
MAX_EXACT = NUM_BUCKETS // 2
REL_MAX_DIST = 128
RMS_EPS = 1e-6

SUBLANES = 8
BF16_SUBLANES = 16
ROW_TILE = 1024
CONV_SUB_ROWS = 256
ATTN_ROW_TILE = 1024
ATTN_SUB_ROWS = 512
ATTN_Q_TILE = 512
ATTN_K_TILE = 256
ATTN_HEADS_PER_STEP = 4
VT_ROWS = V_DIM + BF16_SUBLANES
VMEM_LIMIT_BYTES = 56 * 1024 * 1024

assert ATTN_Q_TILE == 2 * ATTN_K_TILE and ATTN_K_TILE >= REL_MAX_DIST
assert ATTN_ROW_TILE % ATTN_K_TILE == 0

_F32 = jnp.float32
_BF16 = jnp.bfloat16
_LOG2E = math.log2(math.e)


def _dot(a, b):
    return jnp.dot(a, b, preferred_element_type=_F32)


def _rmsnorm_rows(x, g):
    return x * lax.rsqrt(jnp.mean(x * x, axis=-1, keepdims=True) + RMS_EPS) * g


def _const_spec(shape):
    return pl.BlockSpec(shape, lambda *_: (0,) * len(shape), pipeline_mode=pl.Buffered(1))


def _params(semantics):
    return pltpu.CompilerParams(dimension_semantics=semantics,
                                vmem_limit_bytes=VMEM_LIMIT_BYTES)


def _shift_rows(v, prev, s):
    r = pltpu.roll(v, s, 0)
    p = pltpu.roll(prev, s, 0)
    row = lax.broadcasted_iota(jnp.int32, prev.shape, 0)
    head = jnp.where(row < s, p, r[:SUBLANES])
    return jnp.concatenate([head, r[SUBLANES:]], axis=0)


def _conv_layer_kernel(x_ref, g_ref, win_ref, wout_ref, cw_ref, o_ref, carry_ref, *,
                       tiles_per_seq, width, sub_rows):
    w = width

    @pl.when(pl.program_id(0) % tiles_per_seq == 0)
    def _():
        carry_ref[...] = jnp.zeros_like(carry_ref)

    prev = carry_ref[...]
    for r in range(x_ref.shape[0] // sub_rows):
        rows = slice(r * sub_rows, (r + 1) * sub_rows)
        x = x_ref[rows, :]
        h = _rmsnorm_rows(x, g_ref[...]).astype(_BF16)
        b_gate = _dot(h, win_ref[:, 0:w])
        c_gate = _dot(h, win_ref[:, w:2 * w])
        u = _dot(h, win_ref[:, 2 * w:3 * w])
        z = _dot(h, win_ref[:, 3 * w:4 * w])
        v = c_gate * u
        conv = (_shift_rows(v, prev, 2) * cw_ref[0:1, :]
                + _shift_rows(v, prev, 1) * cw_ref[1:2, :] + v * cw_ref[2:3, :])
        prev = v[sub_rows - SUBLANES:]
        y = (b_gate * conv * jax.nn.silu(z)).astype(_BF16)
        o_ref[rows, :] = x + _dot(y, wout_ref[...])
    carry_ref[...] = prev


def _conv_layer(x2, g, w_in, w_out, conv_w, seq):
    n, d = x2.shape
    w = w_out.shape[0]
    tm = ROW_TILE
    return pl.pallas_call(
        functools.partial(_conv_layer_kernel, tiles_per_seq=seq // tm, width=w,
                          sub_rows=CONV_SUB_ROWS),
        grid=(n // tm,),
        in_specs=[
            pl.BlockSpec((tm, d), lambda i: (i, 0)),
            _const_spec((1, d)),
            _const_spec((d, 4 * w)),
            _const_spec((w, d)),
            _const_spec((CONV_K, w)),
        ],
        out_specs=pl.BlockSpec((tm, d), lambda i: (i, 0)),
        out_shape=jax.ShapeDtypeStruct((n, d), _F32),
        scratch_shapes=[pltpu.VMEM((SUBLANES, w), _F32)],
        compiler_params=_params(("arbitrary",)),
        name="conv_layer",
    )(x2, g.reshape(1, d), w_in.astype(_BF16), w_out.astype(_BF16), conv_w.T)


def _qkv_kernel(x_ref, g_ref, w_ref, qt_ref, k_ref, vt_ref, *, width, sub_rows):
    w = width
    n_tiles, n_heads, vt_rows, tk = vt_ref.shape
    row = lax.broadcasted_iota(jnp.int32, (vt_rows - V_DIM, tk), 0)
    ones_rows = jnp.where(row == 0, 1.0, 0.0).astype(_BF16)
    tiles_per_sub = sub_rows // tk
    for r in range(x_ref.shape[0] // sub_rows):
        h = _rmsnorm_rows(x_ref[r * sub_rows:(r + 1) * sub_rows, :], g_ref[...]).astype(_BF16)
        q = _dot(h, w_ref[:, 0:w]) * (HEAD_DIM ** -0.5 * _LOG2E)
        k = _dot(h, w_ref[:, w:2 * w]).astype(_BF16)
        v = _dot(h, w_ref[:, 2 * w:3 * w])
        for t in range(tiles_per_sub):
            rows = slice(t * tk, (t + 1) * tk)
            tile = r * tiles_per_sub + t
            for hd in range(n_heads):
                cols = slice(hd * V_DIM, (hd + 1) * V_DIM)
                qt_ref[tile, hd] = q[rows, cols].T.astype(_BF16)
                k_ref[tile, hd] = k[rows, cols]
                vt_ref[tile, hd, 0:V_DIM, :] = v[rows, cols].T.astype(_BF16)
                vt_ref[tile, hd, V_DIM:, :] = ones_rows


def _qkv_proj(x2, g, w_in):
    n, d = x2.shape
    w = w_in.shape[1] // 4
    tm, tk = ATTN_ROW_TILE, ATTN_K_TILE
    n_heads = w // V_DIM
    out = jax.ShapeDtypeStruct((n // tk, n_heads, tk, V_DIM), _BF16)
    head_tiles = pl.BlockSpec((tm // tk, n_heads, tk, V_DIM), lambda i: (i, 0, 0, 0))
    return pl.pallas_call(
        functools.partial(_qkv_kernel, width=w, sub_rows=ATTN_SUB_ROWS),
        grid=(n // tm,),
        in_specs=[
            pl.BlockSpec((tm, d), lambda i: (i, 0)),
            _const_spec((1, d)),
            _const_spec((d, 3 * w)),
        ],
        out_specs=[pl.BlockSpec((tm // tk, n_heads, V_DIM, tk), lambda i: (i, 0, 0, 0)), head_tiles,
                   pl.BlockSpec((tm // tk, n_heads, VT_ROWS, tk), lambda i: (i, 0, 0, 0))],
        out_shape=[jax.ShapeDtypeStruct((n // tk, n_heads, V_DIM, tk), _BF16), out,
                   jax.ShapeDtypeStruct((n // tk, n_heads, VT_ROWS, tk), _BF16)],
        compiler_params=_params(("arbitrary",)),
        name="qkv_proj",
    )(x2, g.reshape(1, d), w_in)


def _attn_kernel(qt_ref, k_ref, vt_ref, bias_ref, lam_ref, sg_ref, o_ref, *scratch, lambda_init):
    n_heads = qt_ref.shape[1]
    per_head = len(scratch) // n_heads
    heads = tuple(scratch[h * per_head:(h + 1) * per_head] for h in range(n_heads))
    tk = vt_ref.shape[-1]
    tq = qt_ref.shape[0] * tk
    qi = pl.program_id(2)

    for h in range(n_heads):
        qq_sc, m_sc, acc_sc = heads[h][:3]
        q_t = jnp.concatenate([qt_ref[t, h] for t in range(qt_ref.shape[0])], axis=1)
        row = lax.broadcasted_iota(jnp.int32, q_t.shape, 0)
        zero = jnp.zeros_like(q_t)
        qq_sc[:, 0:tq] = jnp.where(row < HEAD_DIM, q_t, zero)
        qq_sc[:, tq:2 * tq] = jnp.where(row >= HEAD_DIM, q_t, zero)
        m_sc[...] = jnp.full_like(m_sc, -jnp.inf)
        acc_sc[...] = jnp.zeros_like(acc_sc)

    prev, diag0, diag1 = 0, 1, 2

    def stage_a(u, biases):
        for h in range(n_heads):
            qq_sc, _, _, s_sc, t_sc, _, _ = heads[h]
            keys = k_ref[pl.ds(2 * u, 2), h].reshape(2 * tk, V_DIM)
            s = _dot(keys, qq_sc[...])
            col_max = None
            for half, bias in enumerate(biases):
                tile = s[half * tk:(half + 1) * tk]
                if bias is not None:
                    tile = tile + bias_ref[h, bias]
                s_sc[half * tk:(half + 1) * tk, :] = tile
                tile_max = jnp.max(tile, axis=0, keepdims=True)
                col_max = tile_max if col_max is None else jnp.maximum(col_max, tile_max)
            t_sc[...] = col_max

    hq = tq // 2

    def late_queries(x):
        return jnp.concatenate([x[:, hq:tq], x[:, tq + hq:]], axis=1)

    def stage_a_diag(u):
        for h in range(n_heads):
            qq_sc, _, _, s_sc, t_sc, _, _ = heads[h]
            s_lo = _dot(k_ref[2 * u, h], qq_sc[...]) + bias_ref[h, diag0]
            s_hi = _dot(k_ref[2 * u + 1, h], late_queries(qq_sc[...]))
            s_hi = s_hi + late_queries(bias_ref[h, diag1])
            masked = jnp.full((tk, hq), -jnp.inf, _F32)
            s_sc[0:tk, :] = s_lo
            s_sc[tk:, :] = jnp.concatenate([masked, s_hi[:, :hq], masked, s_hi[:, hq:]], axis=1)
            max_lo = jnp.max(s_lo, axis=0, keepdims=True)
            max_hi = jnp.max(s_hi, axis=0, keepdims=True)
            t_sc[...] = jnp.concatenate(
                [max_lo[:, :hq], jnp.maximum(max_lo[:, hq:tq], max_hi[:, :hq]),
                 max_lo[:, tq:tq + hq], jnp.maximum(max_lo[:, tq + hq:], max_hi[:, hq:])], axis=1)

    def stage_b():
        for h in range(n_heads):
            _, m_sc, _, s_sc, t_sc, p_sc, a_sc = heads[h]
            m_prev = m_sc[...]
            m_new = jnp.maximum(m_prev, t_sc[...])
            a_sc[...] = jnp.exp2(m_prev - m_new)
            p_sc[...] = jnp.exp2(s_sc[...] - m_new).astype(_BF16)
            m_sc[...] = m_new

    def stage_c(u):
        for h in range(n_heads):
            _, _, acc_sc, _, _, p_sc, a_sc = heads[h]
            vt = jnp.concatenate([vt_ref[2 * u, h], vt_ref[2 * u + 1, h]], axis=1)
            acc_sc[...] = acc_sc[...] * a_sc[...] + _dot(vt, p_sc[...])

    def stage_c_diag(u):
        for h in range(n_heads):
            _, _, acc_sc, _, _, p_sc, a_sc = heads[h]
            pv = _dot(vt_ref[2 * u, h], p_sc[0:tk, :])
            pv_late = _dot(vt_ref[2 * u + 1, h], late_queries(p_sc[tk:, :]))
            pv = jnp.concatenate([pv[:, :hq], pv[:, hq:tq] + pv_late[:, :hq],
                                  pv[:, tq:tq + hq], pv[:, tq + hq:] + pv_late[:, hq:]], axis=1)
            acc_sc[...] = acc_sc[...] * a_sc[...] + pv

    def trip(u, biases=(None, None), issue_a=True, issue_c=True):
        if issue_c:
            stage_c(u - 1)
        stage_b()
        if issue_a and biases is None:
            stage_a_diag(u + 1)
        elif issue_a:
            stage_a(u + 1, biases)

    before_diag = (None, prev)
    short = ([], [before_diag], [(None, None), before_diag])

    for n_q, step_biases in enumerate(short):
        @pl.when(qi == n_q)
        def _(step_biases=step_biases):
            n_before = len(step_biases)
            if n_before:
                stage_a(0, step_biases[0])
            else:
                stage_a_diag(0)
            for u in range(n_before + 1):
                next_biases = step_biases[u + 1] if u + 1 < n_before else None
                trip(u, next_biases, issue_a=u < n_before, issue_c=u > 0)
            stage_c_diag(n_before)

    @pl.when(qi >= len(short))
    def _():
        stage_a(0, (None, None))
        trip(0, issue_c=False)

        def plain(u, carry):
            trip(u)
            return carry

        lax.fori_loop(1, qi - 2, plain, 0)
        trip(qi - 2, before_diag)
        trip(qi - 1, None)
        trip(qi, issue_a=False)
        stage_c_diag(qi)

    lam = (jnp.exp(jnp.sum(lam_ref[0:1, :] * lam_ref[1:2, :], axis=-1, keepdims=True))
           - jnp.exp(jnp.sum(lam_ref[2:3, :] * lam_ref[3:4, :], axis=-1, keepdims=True))
           + lambda_init)
    for h in range(n_heads):
        acc_sc = heads[h][2]
        o_maps = acc_sc[0:V_DIM, :] * (1.0 / acc_sc[V_DIM:V_DIM + 1, :])
        o_t = o_maps[:, :tq] - lam * o_maps[:, tq:]
        o_t = o_t * lax.rsqrt(jnp.mean(o_t * o_t, axis=0, keepdims=True) + RMS_EPS)
        o_ref[:, h * V_DIM:(h + 1) * V_DIM] = (
            o_t.T * sg_ref[...] * (1.0 - lambda_init)).astype(o_ref.dtype)


def _attention(q, k, vt, bias_tiles, lam_params, subln_g, lambda_init, bsz):
    tq, tk, hps = ATTN_Q_TILE, ATTN_K_TILE, ATTN_HEADS_PER_STEP
    n_heads = k.shape[1]
    s = k.shape[0] * tk // bsz
    w = n_heads * V_DIM
    q_tiles = s // tq
    col = pltpu.VMEM((1, 2 * tq), _F32)
    per_head_scratch = [
        pltpu.VMEM((V_DIM, 2 * tq), _BF16),
        col,
        pltpu.VMEM((VT_ROWS, 2 * tq), _F32),
        pltpu.VMEM((2 * tk, 2 * tq), _F32),
        col,
        pltpu.VMEM((2 * tk, 2 * tq), _BF16),
        col,
    ]
    return pl.pallas_call(
        functools.partial(_attn_kernel, lambda_init=lambda_init),
        grid=(n_heads // hps, bsz, q_tiles),
        in_specs=[
            pl.BlockSpec((tq // tk, hps, V_DIM, tk), lambda h, b, i: (b * q_tiles + i, h, 0, 0)),
            pl.BlockSpec((s // tk, hps, tk, V_DIM), lambda h, b, i: (b, h, 0, 0),
                         pipeline_mode=pl.Buffered(1)),
            pl.BlockSpec((s // tk, hps, VT_ROWS, tk), lambda h, b, i: (b, h, 0, 0),
                         pipeline_mode=pl.Buffered(1)),
            pl.BlockSpec((hps, 3, tk, 2 * tq), lambda h, b, i: (h, 0, 0, 0),
                         pipeline_mode=pl.Buffered(1)),
            _const_spec((4, HEAD_DIM)),
            _const_spec((1, V_DIM)),
        ],
        out_specs=pl.BlockSpec((None, tq, hps * V_DIM), lambda h, b, i: (b, i, h)),
        out_shape=jax.ShapeDtypeStruct((bsz, s, w), _BF16),
        scratch_shapes=per_head_scratch * hps,
        compiler_params=_params(("arbitrary", "arbitrary", "arbitrary")),
        name="diff_attention",
    )(q, k, vt, bias_tiles, lam_params, subln_g.reshape(1, V_DIM))


def _attn_out_kernel(x_ref, o_ref, g_ref, wz_ref, wout_ref, fg_ref, y_ref, *, final_norm,
                     sub_rows):
    for r in range(x_ref.shape[0] // sub_rows):
        rows = slice(r * sub_rows, (r + 1) * sub_rows)
        x = x_ref[rows, :]
        h = _rmsnorm_rows(x, g_ref[...]).astype(_BF16)
        z = _dot(h, wz_ref[...])
        y = (o_ref[rows, :] * jax.nn.silu(z)).astype(_BF16)
        x_new = x + _dot(y, wout_ref[...])
        if final_norm:
            x_new = _rmsnorm_rows(x_new, fg_ref[...])
        y_ref[rows, :] = x_new


def _attn_out(x2, o2, g, w_in, w_out, final_g, final_norm):
    n, d = x2.shape
    w = w_out.shape[0]
    tm = ATTN_ROW_TILE
    return pl.pallas_call(
        functools.partial(_attn_out_kernel, final_norm=final_norm, sub_rows=ATTN_SUB_ROWS),
        grid=(n // tm,),
        in_specs=[
            pl.BlockSpec((tm, d), lambda i: (i, 0)),
            pl.BlockSpec((tm, w), lambda i: (i, 0)),
            _const_spec((1, d)),
            pl.BlockSpec((d, w), lambda i: (0, 3), pipeline_mode=pl.Buffered(1)),
            _const_spec((w, d)),
            _const_spec((1, d)),
        ],
        out_specs=pl.BlockSpec((tm, d), lambda i: (i, 0)),
        out_shape=jax.ShapeDtypeStruct((n, d), _F32),
        compiler_params=_params(("arbitrary",)),
        name="attn_out",
    )(x2, o2, g.reshape(1, d), w_in, w_out, final_g.reshape(1, d))


def _t5_causal_bucket(dist):
    d_safe = jnp.maximum(dist, 1).astype(_F32)
    large = MAX_EXACT + (jnp.log(d_safe / MAX_EXACT) / math.log(REL_MAX_DIST / MAX_EXACT)
                         * (NUM_BUCKETS - MAX_EXACT)).astype(jnp.int32)
    large = jnp.minimum(large, NUM_BUCKETS - 1)
    return jnp.where(dist < MAX_EXACT, dist, large)


def _toeplitz_pair(g_vec, rows, cols):
    period = g_vec.shape[-1]
    base = jnp.stack([jnp.roll(g_vec, r, axis=-1) for r in range(SUBLANES)], axis=-2)
    groups = []
    for a in range(rows // SUBLANES):
        shift = a * SUBLANES
        pieces = []
        for m in range(2):
            bm = base[..., m, :, :]
            pieces += [bm[..., period - shift:], bm[..., :cols - shift]]
        groups.append(jnp.concatenate(pieces, axis=-1))
    out = jnp.stack(groups, axis=-3)
    return out.reshape(out.shape[:-3] + (rows, 2 * cols))


def _bias_tiles(rel_bias, tq, tk):
    period = 2 * (tq + tk)
    delta = jnp.arange(period, dtype=jnp.int32)
    delta = jnp.where(delta >= period // 2, delta - period, delta)
    n_dist = tq + tk
    rel = ((rel_bias[_t5_causal_bucket(jnp.arange(n_dist, dtype=jnp.int32))]
            - rel_bias[NUM_BUCKETS - 1]) * _LOG2E).T

    def vec(offset):
        dist = delta + offset
        vals = rel[:, jnp.clip(dist, 0, n_dist - 1)]
        return jnp.where(dist >= 0, vals, -jnp.inf)

    g_vec = jnp.stack([vec(tk), vec(0), vec(-tk)], axis=1)
    g_vec = g_vec.reshape(N_DIFF_HEADS, 2, 3, period).transpose(0, 2, 1, 3)
    return _toeplitz_pair(g_vec, tk, tq)


def _lambda_init_for(layer_idx):
    return 0.8 - 0.6 * math.exp(-0.3 * layer_idx)


def kernel(x, norm_g, w_in, w_out, conv_w, lambda_q1, lambda_k1, lambda_q2, lambda_k2, subln_g,
           rel_bias, final_g):
    bsz, seq, d = x.shape
    depth = norm_g.shape[0]
    w = w_out.shape[1]
    tq, tk = ATTN_Q_TILE, ATTN_K_TILE
    assert seq % tq == 0 and seq % ROW_TILE == 0 and w == N_DIFF_HEADS * V_DIM
    assert (bsz * seq) % ATTN_ROW_TILE == 0
    assert N_DIFF_HEADS % ATTN_HEADS_PER_STEP == 0
    assert depth % N_MIXERS == 0

    bias_tiles = _bias_tiles(rel_bias, tq, tk)
    x2 = x.reshape(bsz * seq, d)
    for i in range(depth):
        j = i // N_MIXERS
        if i % N_MIXERS == 0:
            x2 = _conv_layer(x2, norm_g[i], w_in[i], w_out[i], conv_w[j], seq)
        else:
            w_in_bf = w_in[i].astype(_BF16)
            q, k, vt = _qkv_proj(x2, norm_g[i], w_in_bf)
            lam_params = jnp.stack([lambda_q1[j], lambda_k1[j], lambda_q2[j], lambda_k2[j]])
            o = _attention(q, k, vt, bias_tiles, lam_params, subln_g[j], _lambda_init_for(i),
                           bsz)
            x2 = _attn_out(x2, o.reshape(bsz * seq, w), norm_g[i], w_in_bf,
                           w_out[i].astype(_BF16), final_g, final_norm=(i == depth - 1))
    return x2.reshape(bsz, seq, d)
```

```python
import functools
import math

import jax
import jax.numpy as jnp
from jax import lax
from jax.experimental import pallas as pl
from jax.experimental.pallas import tpu as pltpu

N_MIXERS = 2
CONV_K = 3
N_DIFF_HEADS = 8
HEAD_DIM = 64
V_DIM = 2 * HEAD_DIM
N_MAPS = 2 * N_DIFF_HEADS
NUM_BUCKETS = 32
MAX_EXACT = NUM_BUCKETS // 2
REL_MAX_DIST = 128
RMS_EPS = 1e-6

SUBLANES = 8
BF16_SUBLANES = 16
ROW_TILE = 1024
CONV_SUB_ROWS = 256
ATTN_ROW_TILE = 1024
ATTN_Q_TILE = 512
ATTN_K_TILE = 256
ATTN_HEADS_PER_STEP = 4
VT_ROWS = V_DIM + BF16_SUBLANES
VMEM_LIMIT_BYTES = 56 * 1024 * 1024

assert ATTN_Q_TILE == 2 * ATTN_K_TILE and ATTN_K_TILE >= REL_MAX_DIST
assert ATTN_ROW_TILE % ATTN_K_TILE == 0

_F32 = jnp.float32
_BF16 = jnp.bfloat16
_LOG2E = math.log2(math.e)


def _dot(a, b):
    return jnp.dot(a, b, preferred_element_type=_F32)


def _rmsnorm_rows(x, g):
    return x * lax.rsqrt(jnp.mean(x * x, axis=-1, keepdims=True) + RMS_EPS) * g


def _const_spec(shape):
    return pl.BlockSpec(shape, lambda *_: (0,) * len(shape), pipeline_mode=pl.Buffered(1))


def _params(semantics):
    return pltpu.CompilerParams(dimension_semantics=semantics,
                                vmem_limit_bytes=VMEM_LIMIT_BYTES)


def _shift_rows(v, prev, s):
    r = pltpu.roll(v, s, 0)
    p = pltpu.roll(prev, s, 0)
    row = lax.broadcasted_iota(jnp.int32, prev.shape, 0)
    head = jnp.where(row < s, p, r[:SUBLANES])
    return jnp.concatenate([head, r[SUBLANES:]], axis=0)


def _conv_layer_kernel(x_ref, g_ref, win_ref, wout_ref, cw_ref, o_ref, carry_ref, *,
                       tiles_per_seq, width, sub_rows):
    w = width

    @pl.when(pl.program_id(0) % tiles_per_seq == 0)
    def _():
        carry_ref[...] = jnp.zeros_like(carry_ref)

    prev = carry_ref[...]
    for r in range(x_ref.shape[0] // sub_rows):
        rows = slice(r * sub_rows, (r + 1) * sub_rows)
        x = x_ref[rows, :]
        h = _rmsnorm_rows(x, g_ref[...]).astype(_BF16)
        b_gate = _dot(h, win_ref[:, 0:w])
        c_gate = _dot(h, win_ref[:, w:2 * w])
        u = _dot(h, win_ref[:, 2 * w:3 * w])
        z = _dot(h, win_ref[:, 3 * w:4 * w])
        v = c_gate * u
        conv = (_shift_rows(v, prev, 2) * cw_ref[0:1, :]
                + _shift_rows(v, prev, 1) * cw_ref[1:2, :] + v * cw_ref[2:3, :])
        prev = v[sub_rows - SUBLANES:]
        y = (b_gate * conv * jax.nn.silu(z)).astype(_BF16)
        o_ref[rows, :] = x + _dot(y, wout_ref[...])
    carry_ref[...] = prev


def _conv_layer(x2, g, w_in, w_out, conv_w, seq):
    n, d = x2.shape
    w = w_out.shape[0]
    tm = ROW_TILE
    return pl.pallas_call(
        functools.partial(_conv_layer_kernel, tiles_per_seq=seq // tm, width=w,
                          sub_rows=CONV_SUB_ROWS),
        grid=(n // tm,),
        in_specs=[
            pl.BlockSpec((tm, d), lambda i: (i, 0)),
            _const_spec((1, d)),
            _const_spec((d, 4 * w)),
            _const_spec((w, d)),
            _const_spec((CONV_K, w)),
        ],
        out_specs=pl.BlockSpec((tm, d), lambda i: (i, 0)),
        out_shape=jax.ShapeDtypeStruct((n, d), _F32),
        scratch_shapes=[pltpu.VMEM((SUBLANES, w), _F32)],
        compiler_params=_params(("arbitrary",)),
        name="conv_layer",
    )(x2, g.reshape(1, d), w_in.astype(_BF16), w_out.astype(_BF16), conv_w.T)


def _qkv_kernel(x_ref, g_ref, w_ref, qt_ref, k_ref, vt_ref, *, width):
    w = width
    h = _rmsnorm_rows(x_ref[...], g_ref[...]).astype(_BF16)
    q = _dot(h, w_ref[:, 0:w]) * (HEAD_DIM ** -0.5 * _LOG2E)
    k = _dot(h, w_ref[:, w:2 * w]).astype(_BF16)
    v = _dot(h, w_ref[:, 2 * w:3 * w])
    n_tiles, n_heads, vt_rows, tk = vt_ref.shape
    row = lax.broadcasted_iota(jnp.int32, (vt_rows - V_DIM, tk), 0)
    ones_rows = jnp.where(row == 0, 1.0, 0.0).astype(_BF16)
    for t in range(n_tiles):
        rows = slice(t * tk, (t + 1) * tk)
        for hd in range(n_heads):
            cols = slice(hd * V_DIM, (hd + 1) * V_DIM)
            qt_ref[t, hd] = q[rows, cols].T.astype(_BF16)
            k_ref[t, hd] = k[rows, cols]
            vt_ref[t, hd, 0:V_DIM, :] = v[rows, cols].T.astype(_BF16)
            vt_ref[t, hd, V_DIM:, :] = ones_rows


def _qkv_proj(x2, g, w_in):
    n, d = x2.shape
    w = w_in.shape[1] // 4
    tm, tk = ATTN_ROW_TILE, ATTN_K_TILE
    n_heads = w // V_DIM
    out = jax.ShapeDtypeStruct((n // tk, n_heads, tk, V_DIM), _BF16)
    head_tiles = pl.BlockSpec((tm // tk, n_heads, tk, V_DIM), lambda i: (i, 0, 0, 0))
    return pl.pallas_call(
        functools.partial(_qkv_kernel, width=w),
        grid=(n // tm,),
        in_specs=[
            pl.BlockSpec((tm, d), lambda i: (i, 0)),
            _const_spec((1, d)),
            _const_spec((d, 3 * w)),
        ],
        out_specs=[pl.BlockSpec((tm // tk, n_heads, V_DIM, tk), lambda i: (i, 0, 0, 0)), head_tiles,
                   pl.BlockSpec((tm // tk, n_heads, VT_ROWS, tk), lambda i: (i, 0, 0, 0))],
        out_shape=[jax.ShapeDtypeStruct((n // tk, n_heads, V_DIM, tk), _BF16), out,
                   jax.ShapeDtypeStruct((n // tk, n_heads, VT_ROWS, tk), _BF16)],
        compiler_params=_params(("arbitrary",)),
        name="qkv_proj",
    )(x2, g.reshape(1, d), w_in)


def _attn_kernel(qt_ref, k_ref, vt_ref, bias_ref, lam_ref, sg_ref, o_ref, *scratch, lambda_init):
    n_heads = qt_ref.shape[1]
    per_head = len(scratch) // n_heads
    heads = tuple(scratch[h * per_head:(h + 1) * per_head] for h in range(n_heads))
    tk = vt_ref.shape[-1]
    tq = qt_ref.shape[0] * tk
    qi = pl.program_id(2)

    for h in range(n_heads):
        qq_sc, m_sc, acc_sc = heads[h][:3]
        q_t = jnp.concatenate([qt_ref[t, h] for t in range(qt_ref.shape[0])], axis=1)
        row = lax.broadcasted_iota(jnp.int32, q_t.shape, 0)
        zero = jnp.zeros_like(q_t)
        qq_sc[:, 0:tq] = jnp.where(row < HEAD_DIM, q_t, zero)
        qq_sc[:, tq:2 * tq] = jnp.where(row >= HEAD_DIM, q_t, zero)
        m_sc[...] = jnp.full_like(m_sc, -jnp.inf)
        acc_sc[...] = jnp.zeros_like(acc_sc)

    hq = tq // 2
    near = tk // 2
    assert near >= REL_MAX_DIST and near <= hq

    def near_bias(h):
        zeros = jnp.zeros((near, tq - near), _F32)
        return jnp.concatenate([bias_ref[h, 0:near, near:2 * near], zeros,
                                bias_ref[h, 0:near, tq + near:tq + 2 * near], zeros], axis=1)

    def stage_a(u, before_diag=False):
        for h in range(n_heads):
            qq_sc, _, _, s_sc, t_sc, _, _ = heads[h]
            keys = k_ref[pl.ds(2 * u, 2), h].reshape(2 * tk, V_DIM)
            s = _dot(keys, qq_sc[...])
            if before_diag:
                s = jnp.concatenate([s[:2 * tk - near], s[2 * tk - near:] + near_bias(h)], axis=0)
            s_sc[...] = s
            t_sc[...] = jnp.max(s, axis=0, keepdims=True)

    def late_queries(x):
        return jnp.concatenate([x[:, hq:tq], x[:, tq + hq:]], axis=1)

    def stage_a_diag(u):
        for h in range(n_heads):
            qq_sc, _, _, s_sc, t_sc, _, _ = heads[h]
            s_lo = _dot(k_ref[2 * u, h], qq_sc[...]) + bias_ref[h]
            s_hi = _dot(k_ref[2 * u + 1, h], late_queries(qq_sc[...]))
            s_hi = s_hi + jnp.concatenate([bias_ref[h, :, 0:hq], bias_ref[h, :, tq:tq + hq]],
                                          axis=1)
            masked = jnp.full((tk, hq), -jnp.inf, _F32)
            s_sc[0:tk, :] = s_lo
            s_sc[tk:, :] = jnp.concatenate([masked, s_hi[:, :hq], masked, s_hi[:, hq:]], axis=1)
            max_lo = jnp.max(s_lo, axis=0, keepdims=True)
            max_hi = jnp.max(s_hi, axis=0, keepdims=True)
            t_sc[...] = jnp.concatenate(
                [max_lo[:, :hq], jnp.maximum(max_lo[:, hq:tq], max_hi[:, :hq]),
                 max_lo[:, tq:tq + hq], jnp.maximum(max_lo[:, tq + hq:], max_hi[:, hq:])], axis=1)

    def stage_b():
        for h in range(n_heads):
            _, m_sc, _, s_sc, t_sc, p_sc, a_sc = heads[h]
            m_prev = m_sc[...]
            m_new = jnp.maximum(m_prev, t_sc[...])
            a_sc[...] = jnp.exp2(m_prev - m_new)
            p_sc[...] = jnp.exp2(s_sc[...] - m_new).astype(_BF16)
            m_sc[...] = m_new

    def stage_c(u):
        for h in range(n_heads):
            _, _, acc_sc, _, _, p_sc, a_sc = heads[h]
            vt = jnp.concatenate([vt_ref[2 * u, h], vt_ref[2 * u + 1, h]], axis=1)
            acc_sc[...] = acc_sc[...] * a_sc[...] + _dot(vt, p_sc[...])

    def stage_c_diag(u):
        for h in range(n_heads):
            _, _, acc_sc, _, _, p_sc, a_sc = heads[h]
            pv = _dot(vt_ref[2 * u, h], p_sc[0:tk, :])
            pv_late = _dot(vt_ref[2 * u + 1, h], late_queries(p_sc[tk:, :]))
            pv = jnp.concatenate([pv[:, :hq], pv[:, hq:tq] + pv_late[:, :hq],
                                  pv[:, tq:tq + hq], pv[:, tq + hq:] + pv_late[:, hq:]], axis=1)
            acc_sc[...] = acc_sc[...] * a_sc[...] + pv

    plain_step, before_diag, diag_step = 0, 1, 2

    def trip(u, next_step=plain_step, issue_a=True, issue_c=True):
        if issue_c:
            stage_c(u - 1)
        stage_b()
        if issue_a and next_step == diag_step:
            stage_a_diag(u + 1)
        elif issue_a:
            stage_a(u + 1, before_diag=next_step == before_diag)

    short = ([], [before_diag], [plain_step, before_diag])

    for n_q, step_biases in enumerate(short):
        @pl.when(qi == n_q)
        def _(step_biases=step_biases):
            n_before = len(step_biases)
            if n_before:
                stage_a(0, before_diag=step_biases[0] == before_diag)
            else:
                stage_a_diag(0)
            for u in range(n_before + 1):
                next_step = step_biases[u + 1] if u + 1 < n_before else diag_step
                trip(u, next_step, issue_a=u < n_before, issue_c=u > 0)
            stage_c_diag(n_before)

    @pl.when(qi >= len(short))
    def _():
        stage_a(0)
        trip(0, issue_c=False)

        def plain(u, carry):
            trip(u)
            return carry

        lax.fori_loop(1, qi - 2, plain, 0)
        trip(qi - 2, before_diag)
        trip(qi - 1, diag_step)
        trip(qi, issue_a=False)
        stage_c_diag(qi)

    lam = (jnp.exp(jnp.sum(lam_ref[0:1, :] * lam_ref[1:2, :], axis=-1, keepdims=True))
           - jnp.exp(jnp.sum(lam_ref[2:3, :] * lam_ref[3:4, :], axis=-1, keepdims=True))
           + lambda_init)
    for h in range(n_heads):
        acc_sc = heads[h][2]
        o_maps = acc_sc[0:V_DIM, :] * (1.0 / acc_sc[V_DIM:V_DIM + 1, :])
        o_t = o_maps[:, :tq] - lam * o_maps[:, tq:]
        o_t = o_t * lax.rsqrt(jnp.mean(o_t * o_t, axis=0, keepdims=True) + RMS_EPS)
        o_ref[:, h * V_DIM:(h + 1) * V_DIM] = (
            o_t.T * sg_ref[...] * (1.0 - lambda_init)).astype(o_ref.dtype)


def _attention(q, k, vt, bias_tiles, lam_params, subln_g, lambda_init, bsz):
    tq, tk, hps = ATTN_Q_TILE, ATTN_K_TILE, ATTN_HEADS_PER_STEP
    n_heads = k.shape[1]
    s = k.shape[0] * tk // bsz
    w = n_heads * V_DIM
    q_tiles = s // tq
    col = pltpu.VMEM((1, 2 * tq), _F32)
    per_head_scratch = [
        pltpu.VMEM((V_DIM, 2 * tq), _BF16),
        col,
        pltpu.VMEM((VT_ROWS, 2 * tq), _F32),
        pltpu.VMEM((2 * tk, 2 * tq), _F32),
        col,
        pltpu.VMEM((2 * tk, 2 * tq), _BF16),
        col,
    ]
    return pl.pallas_call(
        functools.partial(_attn_kernel, lambda_init=lambda_init),
        grid=(n_heads // hps, bsz, q_tiles),
        in_specs=[
            pl.BlockSpec((tq // tk, hps, V_DIM, tk), lambda h, b, i: (b * q_tiles + i, h, 0, 0)),
            pl.BlockSpec((s // tk, hps, tk, V_DIM), lambda h, b, i: (b, h, 0, 0)),
            pl.BlockSpec((s // tk, hps, VT_ROWS, tk), lambda h, b, i: (b, h, 0, 0),
                         pipeline_mode=pl.Buffered(1)),
            pl.BlockSpec((hps, tk, 2 * tq), lambda h, b, i: (h, 0, 0),
                         pipeline_mode=pl.Buffered(1)),
            _const_spec((4, HEAD_DIM)),
            _const_spec((1, V_DIM)),
        ],
        out_specs=pl.BlockSpec((None, tq, hps * V_DIM), lambda h, b, i: (b, i, h)),
        out_shape=jax.ShapeDtypeStruct((bsz, s, w), _BF16),
        scratch_shapes=per_head_scratch * hps,
        compiler_params=_params(("arbitrary", "arbitrary", "arbitrary")),
        name="diff_attention",
    )(q, k, vt, bias_tiles, lam_params, subln_g.reshape(1, V_DIM))


def _attn_out_kernel(x_ref, o_ref, g_ref, wz_ref, wout_ref, fg_ref, y_ref, *, final_norm):
    x = x_ref[...]
    h = _rmsnorm_rows(x, g_ref[...]).astype(_BF16)
    z = _dot(h, wz_ref[...])
    y = (o_ref[...] * jax.nn.silu(z)).astype(_BF16)
    x_new = x + _dot(y, wout_ref[...])
    if final_norm:
        x_new = _rmsnorm_rows(x_new, fg_ref[...])
    y_ref[...] = x_new


def _attn_out(x2, o2, g, w_in, w_out, final_g, final_norm):
    n, d = x2.shape
    w = w_out.shape[0]
    tm = ATTN_ROW_TILE
    return pl.pallas_call(
        functools.partial(_attn_out_kernel, final_norm=final_norm),
        grid=(n // tm,),
        in_specs=[
            pl.BlockSpec((tm, d), lambda i: (i, 0)),
            pl.BlockSpec((tm, w), lambda i: (i, 0)),
            _const_spec((1, d)),
            pl.BlockSpec((d, w), lambda i: (0, 3), pipeline_mode=pl.Buffered(1)),
            _const_spec((w, d)),
            _const_spec((1, d)),
        ],
        out_specs=pl.BlockSpec((tm, d), lambda i: (i, 0)),
        out_shape=jax.ShapeDtypeStruct((n, d), _F32),
        compiler_params=_params(("arbitrary",)),
        name="attn_out",
    )(x2, o2, g.reshape(1, d), w_in, w_out, final_g.reshape(1, d))


def _t5_causal_bucket(dist):
    d_safe = jnp.maximum(dist, 1).astype(_F32)
    large = MAX_EXACT + (jnp.log(d_safe / MAX_EXACT) / math.log(REL_MAX_DIST / MAX_EXACT)
                         * (NUM_BUCKETS - MAX_EXACT)).astype(jnp.int32)
    large = jnp.minimum(large, NUM_BUCKETS - 1)
    return jnp.where(dist < MAX_EXACT, dist, large)


def _toeplitz_pair(g_vec, rows, cols):
    period = g_vec.shape[-1]
    base = jnp.stack([jnp.roll(g_vec, r, axis=-1) for r in range(SUBLANES)], axis=-2)
    groups = []
    for a in range(rows // SUBLANES):
        shift = a * SUBLANES
        pieces = []
        for m in range(2):
            bm = base[..., m, :, :]
            pieces += [bm[..., period - shift:], bm[..., :cols - shift]]
        groups.append(jnp.concatenate(pieces, axis=-1))
    out = jnp.stack(groups, axis=-3)
    return out.reshape(out.shape[:-3] + (rows, 2 * cols))


def _bias_tiles(rel_bias, tq, tk):
    period = 2 * (tq + tk)
    delta = jnp.arange(period, dtype=jnp.int32)
    delta = jnp.where(delta >= period // 2, delta - period, delta)
    n_dist = tq + tk
    rel = ((rel_bias[_t5_causal_bucket(jnp.arange(n_dist, dtype=jnp.int32))]
            - rel_bias[NUM_BUCKETS - 1]) * _LOG2E).T

    def vec(offset):
        dist = delta + offset
        vals = rel[:, jnp.clip(dist, 0, n_dist - 1)]
        return jnp.where(dist >= 0, vals, -jnp.inf)

    g_vec = vec(0).reshape(N_DIFF_HEADS, 2, period)
    return _toeplitz_pair(g_vec, tk, tq)


def _lambda_init_for(layer_idx):
    return 0.8 - 0.6 * math.exp(-0.3 * layer_idx)


def kernel(x, norm_g, w_in, w_out, conv_w, lambda_q1, lambda_k1, lambda_q2, lambda_k2, subln_g,
           rel_bias, final_g):
    bsz, seq, d = x.shape
    depth = norm_g.shape[0]
    w = w_out.shape[1]
    tq, tk = ATTN_Q_TILE, ATTN_K_TILE
    assert seq % tq == 0 and seq % ROW_TILE == 0 and w == N_DIFF_HEADS * V_DIM
    assert (bsz * seq) % ATTN_ROW_TILE == 0
    assert N_DIFF_HEADS % ATTN_HEADS_PER_STEP == 0
    assert depth % N_MIXERS == 0

    bias_tiles = _bias_tiles(rel_bias, tq, tk)
    x2 = x.reshape(bsz * seq, d)
    for i in range(depth):
        j = i // N_MIXERS
        if i % N_MIXERS == 0:
            x2 = _conv_layer(x2, norm_g[i], w_in[i], w_out[i], conv_w[j], seq)
        else:
            w_in_bf = w_in[i].astype(_BF16)
            q, k, vt = _qkv_proj(x2, norm_g[i], w_in_bf)
            lam_params = jnp.stack([lambda_q1[j], lambda_k1[j], lambda_q2[j], lambda_k2[j]])
            o = _attention(q, k, vt, bias_tiles, lam_params, subln_g[j], _lambda_init_for(i),
                           bsz)
            x2 = _attn_out(x2, o.reshape(bsz * seq, w), norm_g[i], w_in_bf,
                           w_out[i].astype(_BF16), final_g, final_norm=(i == depth - 1))
    return x2.reshape(bsz, seq, d)
```

```python
import functools
import math

import jax
import jax.numpy as jnp
from jax import lax
from jax.experimental import pallas as pl
from jax.experimental.pallas import tpu as pltpu

N_MIXERS = 2
CONV_K = 3
N_DIFF_HEADS = 8
HEAD_DIM = 64
V_DIM = 2 * HEAD_DIM
N_MAPS = 2 * N_DIFF_HEADS
NUM_BUCKETS = 32
MAX_EXACT = NUM_BUCKETS // 2
REL_MAX_DIST = 128
RMS_EPS = 1e-6

SUBLANES = 8
BF16_SUBLANES = 16
ROW_TILE = 1024
CONV_SUB_ROWS = 256
ATTN_ROW_TILE = 1024
QKV_SUB_ROWS = 512
ATTN_Q_TILE = 512
ATTN_K_TILE = 256
ATTN_HEADS_PER_STEP = 4
VT_ROWS = V_DIM + BF16_SUBLANES
VMEM_LIMIT_BYTES = 56 * 1024 * 1024

assert ATTN_Q_TILE == 2 * ATTN_K_TILE and ATTN_K_TILE >= REL_MAX_DIST
assert ATTN_ROW_TILE % ATTN_K_TILE == 0

_F32 = jnp.float32
_BF16 = jnp.bfloat16
_LOG2E = math.log2(math.e)


def _dot(a, b):
    return jnp.dot(a, b, preferred_element_type=_F32)


def _rmsnorm_rows(x, g):
    return x * lax.rsqrt(jnp.mean(x * x, axis=-1, keepdims=True) + RMS_EPS) * g


def _const_spec(shape):
    return pl.BlockSpec(shape, lambda *_: (0,) * len(shape), pipeline_mode=pl.Buffered(1))


def _params(semantics):
    return pltpu.CompilerParams(dimension_semantics=semantics,
                                vmem_limit_bytes=VMEM_LIMIT_BYTES)


def _shift_rows(v, prev, s):
    r = pltpu.roll(v, s, 0)
    p = pltpu.roll(prev, s, 0)
    row = lax.broadcasted_iota(jnp.int32, prev.shape, 0)
    head = jnp.where(row < s, p, r[:SUBLANES])
    return jnp.concatenate([head, r[SUBLANES:]], axis=0)


def _conv_layer_kernel(x_ref, g_ref, win_ref, wout_ref, cw_ref, o_ref, carry_ref, *,
                       tiles_per_seq, width, sub_rows):
    w = width

    @pl.when(pl.program_id(0) % tiles_per_seq == 0)
    def _():
        carry_ref[...] = jnp.zeros_like(carry_ref)

    prev = carry_ref[...]
    for r in range(x_ref.shape[0] // sub_rows):
        rows = slice(r * sub_rows, (r + 1) * sub_rows)
        x = x_ref[rows, :]
        h = _rmsnorm_rows(x, g_ref[...]).astype(_BF16)
        b_gate = _dot(h, win_ref[:, 0:w])
        c_gate = _dot(h, win_ref[:, w:2 * w])
        u = _dot(h, win_ref[:, 2 * w:3 * w])
        z = _dot(h, win_ref[:, 3 * w:4 * w])
        v = c_gate * u
        conv = (_shift_rows(v, prev, 2) * cw_ref[0:1, :]
                + _shift_rows(v, prev, 1) * cw_ref[1:2, :] + v * cw_ref[2:3, :])
        prev = v[sub_rows - SUBLANES:]
        y = (b_gate * conv * jax.nn.silu(z)).astype(_BF16)
        o_ref[rows, :] = x + _dot(y, wout_ref[...])
    carry_ref[...] = prev


def _conv_layer(x2, g, w_in, w_out, conv_w, seq):
    n, d = x2.shape
    w = w_out.shape[0]
    tm = ROW_TILE
    return pl.pallas_call(
        functools.partial(_conv_layer_kernel, tiles_per_seq=seq // tm, width=w,
                          sub_rows=CONV_SUB_ROWS),
        grid=(n // tm,),
        in_specs=[
            pl.BlockSpec((tm, d), lambda i: (i, 0)),
            _const_spec((1, d)),
            _const_spec((d, 4 * w)),
            _const_spec((w, d)),
            _const_spec((CONV_K, w)),
        ],
        out_specs=pl.BlockSpec((tm, d), lambda i: (i, 0)),
        out_shape=jax.ShapeDtypeStruct((n, d), _F32),
        scratch_shapes=[pltpu.VMEM((SUBLANES, w), _F32)],
        compiler_params=_params(("arbitrary",)),
        name="conv_layer",
    )(x2, g.reshape(1, d), w_in.astype(_BF16), w_out.astype(_BF16), conv_w.T)


def _qkv_kernel(x_ref, g_ref, w_ref, qt_ref, k_ref, vt_ref, *, width, sub_rows):
    w = width
    n_tiles, n_heads, vt_rows, tk = vt_ref.shape
    row = lax.broadcasted_iota(jnp.int32, (vt_rows - V_DIM, tk), 0)
    ones_rows = jnp.where(row == 0, 1.0, 0.0).astype(_BF16)
    tiles_per_sub = sub_rows // tk
    for r in range(x_ref.shape[0] // sub_rows):
        h = _rmsnorm_rows(x_ref[r * sub_rows:(r + 1) * sub_rows, :], g_ref[...]).astype(_BF16)
        q = _dot(h, w_ref[:, 0:w]) * (HEAD_DIM ** -0.5 * _LOG2E)
        k = _dot(h, w_ref[:, w:2 * w]).astype(_BF16)
        v = _dot(h, w_ref[:, 2 * w:3 * w])
        for t in range(tiles_per_sub):
            rows = slice(t * tk, (t + 1) * tk)
            tile = r * tiles_per_sub + t
            for hd in range(n_heads):
                cols = slice(hd * V_DIM, (hd + 1) * V_DIM)
                qt_ref[tile, hd] = q[rows, cols].T.astype(_BF16)
                k_ref[tile, hd] = k[rows, cols]
                vt_ref[tile, hd, 0:V_DIM, :] = v[rows, cols].T.astype(_BF16)
                vt_ref[tile, hd, V_DIM:, :] = ones_rows


def _qkv_proj(x2, g, w_in):
    n, d = x2.shape
    w = w_in.shape[1] // 4
    tm, tk = ATTN_ROW_TILE, ATTN_K_TILE
    n_heads = w // V_DIM
    out = jax.ShapeDtypeStruct((n // tk, n_heads, tk, V_DIM), _BF16)
    head_tiles = pl.BlockSpec((tm // tk, n_heads, tk, V_DIM), lambda i: (i, 0, 0, 0))
    return pl.pallas_call(
        functools.partial(_qkv_kernel, width=w, sub_rows=QKV_SUB_ROWS),
        grid=(n // tm,),
        in_specs=[
            pl.BlockSpec((tm, d), lambda i: (i, 0)),
            _const_spec((1, d)),
            _const_spec((d, 3 * w)),
        ],
        out_specs=[pl.BlockSpec((tm // tk, n_heads, V_DIM, tk), lambda i: (i, 0, 0, 0)), head_tiles,
                   pl.BlockSpec((tm // tk, n_heads, VT_ROWS, tk), lambda i: (i, 0, 0, 0))],
        out_shape=[jax.ShapeDtypeStruct((n // tk, n_heads, V_DIM, tk), _BF16), out,
                   jax.ShapeDtypeStruct((n // tk, n_heads, VT_ROWS, tk), _BF16)],
        compiler_params=_params(("arbitrary",)),
        name="qkv_proj",
    )(x2, g.reshape(1, d), w_in)


def _attn_kernel(qt_ref, k_ref, vt_ref, bias_ref, lam_ref, sg_ref, o_ref, *scratch, lambda_init):
    n_heads = qt_ref.shape[1]
    per_head = len(scratch) // n_heads
    heads = tuple(scratch[h * per_head:(h + 1) * per_head] for h in range(n_heads))
    tk = vt_ref.shape[-1]
    tq = qt_ref.shape[0] * tk
    qi = pl.program_id(2)

    for h in range(n_heads):
        qq_sc, m_sc, acc_sc = heads[h][:3]
        q_t = jnp.concatenate([qt_ref[t, h] for t in range(qt_ref.shape[0])], axis=1)
        row = lax.broadcasted_iota(jnp.int32, q_t.shape, 0)
        zero = jnp.zeros_like(q_t)
        qq_sc[:, 0:tq] = jnp.where(row < HEAD_DIM, q_t, zero)
        qq_sc[:, tq:2 * tq] = jnp.where(row >= HEAD_DIM, q_t, zero)
        m_sc[...] = jnp.full_like(m_sc, -jnp.inf)
        acc_sc[...] = jnp.zeros_like(acc_sc)

    hq = tq // 2
    near = tk // 2
    assert near >= REL_MAX_DIST and near <= hq

    def near_bias(h):
        zeros = jnp.zeros((near, tq - near), _F32)
        return jnp.concatenate([bias_ref[h, 0:near, near:2 * near], zeros,
                                bias_ref[h, 0:near, tq + near:tq + 2 * near], zeros], axis=1)

    def stage_a(u, before_diag=False):
        for h in range(n_heads):
            qq_sc, _, _, s_sc, t_sc, _, _ = heads[h]
            keys = k_ref[pl.ds(2 * u, 2), h].reshape(2 * tk, V_DIM)
            s = _dot(keys, qq_sc[...])
            if before_diag:
                s = jnp.concatenate([s[:2 * tk - near], s[2 * tk - near:] + near_bias(h)], axis=0)
            s_sc[...] = s
            t_sc[...] = jnp.max(s, axis=0, keepdims=True)

    def late_queries(x):
        return jnp.concatenate([x[:, hq:tq], x[:, tq + hq:]], axis=1)

    def stage_a_diag(u):
        for h in range(n_heads):
            qq_sc, _, _, s_sc, t_sc, _, _ = heads[h]
            s_lo = _dot(k_ref[2 * u, h], qq_sc[...]) + bias_ref[h]
            s_hi = _dot(k_ref[2 * u + 1, h], late_queries(qq_sc[...]))
            s_hi = s_hi + jnp.concatenate([bias_ref[h, :, 0:hq], bias_ref[h, :, tq:tq + hq]],
                                          axis=1)
            masked = jnp.full((tk, hq), -jnp.inf, _F32)
            s_sc[0:tk, :] = s_lo
            s_sc[tk:, :] = jnp.concatenate([masked, s_hi[:, :hq], masked, s_hi[:, hq:]], axis=1)
            max_lo = jnp.max(s_lo, axis=0, keepdims=True)
            max_hi = jnp.max(s_hi, axis=0, keepdims=True)
            t_sc[...] = jnp.concatenate(
                [max_lo[:, :hq], jnp.maximum(max_lo[:, hq:tq], max_hi[:, :hq]),
                 max_lo[:, tq:tq + hq], jnp.maximum(max_lo[:, tq + hq:], max_hi[:, hq:])], axis=1)

    def stage_b():
        for h in range(n_heads):
            _, m_sc, _, s_sc, t_sc, p_sc, a_sc = heads[h]
            m_prev = m_sc[...]
            m_new = jnp.maximum(m_prev, t_sc[...])
            a_sc[...] = jnp.exp2(m_prev - m_new)
            p_sc[...] = jnp.exp2(s_sc[...] - m_new).astype(_BF16)
            m_sc[...] = m_new

    def stage_c(u):
        for h in range(n_heads):
            _, _, acc_sc, _, _, p_sc, a_sc = heads[h]
            vt = jnp.concatenate([vt_ref[2 * u, h], vt_ref[2 * u + 1, h]], axis=1)
            acc_sc[...] = acc_sc[...] * a_sc[...] + _dot(vt, p_sc[...])

    def stage_c_diag(u):
        for h in range(n_heads):
            _, _, acc_sc, _, _, p_sc, a_sc = heads[h]
            pv = _dot(vt_ref[2 * u, h], p_sc[0:tk, :])
            pv_late = _dot(vt_ref[2 * u + 1, h], late_queries(p_sc[tk:, :]))
            pv = jnp.concatenate([pv[:, :hq], pv[:, hq:tq] + pv_late[:, :hq],
                                  pv[:, tq:tq + hq], pv[:, tq + hq:] + pv_late[:, hq:]], axis=1)
            acc_sc[...] = acc_sc[...] * a_sc[...] + pv

    plain_step, before_diag, diag_step = 0, 1, 2

    def trip(u, next_step=plain_step, issue_a=True, issue_c=True):
        if issue_c:
            stage_c(u - 1)
        stage_b()
        if issue_a and next_step == diag_step:
            stage_a_diag(u + 1)
        elif issue_a:
            stage_a(u + 1, before_diag=next_step == before_diag)

    short = ([], [before_diag], [plain_step, before_diag])

    for n_q, step_biases in enumerate(short):
        @pl.when(qi == n_q)
        def _(step_biases=step_biases):
            n_before = len(step_biases)
            if n_before:
                stage_a(0, before_diag=step_biases[0] == before_diag)
            else:
                stage_a_diag(0)
            for u in range(n_before + 1):
                next_step = step_biases[u + 1] if u + 1 < n_before else diag_step
                trip(u, next_step, issue_a=u < n_before, issue_c=u > 0)
            stage_c_diag(n_before)

    @pl.when(qi >= len(short))
    def _():
        stage_a(0)
        trip(0, issue_c=False)

        def plain(u, carry):
            trip(u)
            return carry

        lax.fori_loop(1, qi - 2, plain, 0)
        trip(qi - 2, before_diag)
        trip(qi - 1, diag_step)
        trip(qi, issue_a=False)
        stage_c_diag(qi)

    lam = (jnp.exp(jnp.sum(lam_ref[0:1, :] * lam_ref[1:2, :], axis=-1, keepdims=True))
           - jnp.exp(jnp.sum(lam_ref[2:3, :] * lam_ref[3:4, :], axis=-1, keepdims=True))
           + lambda_init)
    for h in range(n_heads):
        acc_sc = heads[h][2]
        o_maps = acc_sc[0:V_DIM, :] * (1.0 / acc_sc[V_DIM:V_DIM + 1, :])
        o_t = o_maps[:, :tq] - lam * o_maps[:, tq:]
        o_t = o_t * lax.rsqrt(jnp.mean(o_t * o_t, axis=0, keepdims=True) + RMS_EPS)
        o_ref[:, h * V_DIM:(h + 1) * V_DIM] = (
            o_t.T * sg_ref[...] * (1.0 - lambda_init)).astype(o_ref.dtype)


def _attention(q, k, vt, bias_tiles, lam_params, subln_g, lambda_init, bsz):
    tq, tk, hps = ATTN_Q_TILE, ATTN_K_TILE, ATTN_HEADS_PER_STEP
    n_heads = k.shape[1]
    s = k.shape[0] * tk // bsz
    w = n_heads * V_DIM
    q_tiles = s // tq
    col = pltpu.VMEM((1, 2 * tq), _F32)
    per_head_scratch = [
        pltpu.VMEM((V_DIM, 2 * tq), _BF16),
        col,
        pltpu.VMEM((VT_ROWS, 2 * tq), _F32),
        pltpu.VMEM((2 * tk, 2 * tq), _F32),
        col,
        pltpu.VMEM((2 * tk, 2 * tq), _BF16),
        col,
    ]
    return pl.pallas_call(
        functools.partial(_attn_kernel, lambda_init=lambda_init),
        grid=(n_heads // hps, bsz, q_tiles),
        in_specs=[
            pl.BlockSpec((tq // tk, hps, V_DIM, tk), lambda h, b, i: (b * q_tiles + i, h, 0, 0)),
            pl.BlockSpec((s // tk, hps, tk, V_DIM), lambda h, b, i: (b, h, 0, 0)),
            pl.BlockSpec((s // tk, hps, VT_ROWS, tk), lambda h, b, i: (b, h, 0, 0),
                         pipeline_mode=pl.Buffered(1)),
            pl.BlockSpec((hps, tk, 2 * tq), lambda h, b, i: (h, 0, 0),
                         pipeline_mode=pl.Buffered(1)),
            _const_spec((4, HEAD_DIM)),
            _const_spec((1, V_DIM)),
        ],
        out_specs=pl.BlockSpec((None, tq, hps * V_DIM), lambda h, b, i: (b, i, h)),
        out_shape=jax.ShapeDtypeStruct((bsz, s, w), _BF16),
        scratch_shapes=per_head_scratch * hps,
        compiler_params=_params(("arbitrary", "arbitrary", "arbitrary")),
        name="diff_attention",
    )(q, k, vt, bias_tiles, lam_params, subln_g.reshape(1, V_DIM))


def _attn_out_kernel(x_ref, o_ref, g_ref, wz_ref, wout_ref, fg_ref, y_ref, *, final_norm):
    x = x_ref[...]
    h = _rmsnorm_rows(x, g_ref[...]).astype(_BF16)
    z = _dot(h, wz_ref[...])
    y = (o_ref[...] * jax.nn.silu(z)).astype(_BF16)
    x_new = x + _dot(y, wout_ref[...])
    if final_norm:
        x_new = _rmsnorm_rows(x_new, fg_ref[...])
    y_ref[...] = x_new


def _attn_out(x2, o2, g, w_in, w_out, final_g, final_norm):
    n, d = x2.shape
    w = w_out.shape[0]
    tm = ATTN_ROW_TILE
    return pl.pallas_call(
        functools.partial(_attn_out_kernel, final_norm=final_norm),
        grid=(n // tm,),
        in_specs=[
            pl.BlockSpec((tm, d), lambda i: (i, 0)),
            pl.BlockSpec((tm, w), lambda i: (i, 0)),
            _const_spec((1, d)),
            pl.BlockSpec((d, w), lambda i: (0, 3), pipeline_mode=pl.Buffered(1)),
            _const_spec((w, d)),
            _const_spec((1, d)),
        ],
        out_specs=pl.BlockSpec((tm, d), lambda i: (i, 0)),
        out_shape=jax.ShapeDtypeStruct((n, d), _F32),
        compiler_params=_params(("arbitrary",)),
        name="attn_out",
    )(x2, o2, g.reshape(1, d), w_in, w_out, final_g.reshape(1, d))


def _t5_causal_bucket(dist):
    d_safe = jnp.maximum(dist, 1).astype(_F32)
    large = MAX_EXACT + (jnp.log(d_safe / MAX_EXACT) / math.log(REL_MAX_DIST / MAX_EXACT)
                         * (NUM_BUCKETS - MAX_EXACT)).astype(jnp.int32)
    large = jnp.minimum(large, NUM_BUCKETS - 1)
    return jnp.where(dist < MAX_EXACT, dist, large)


def _toeplitz_pair(g_vec, rows, cols):
    period = g_vec.shape[-1]
    base = jnp.stack([jnp.roll(g_vec, r, axis=-1) for r in range(SUBLANES)], axis=-2)
    groups = []
    for a in range(rows // SUBLANES):
        shift = a * SUBLANES
        pieces = []
        for m in range(2):
            bm = base[..., m, :, :]
            pieces += [bm[..., period - shift:], bm[..., :cols - shift]]
        groups.append(jnp.concatenate(pieces, axis=-1))
    out = jnp.stack(groups, axis=-3)
    return out.reshape(out.shape[:-3] + (rows, 2 * cols))


def _bias_tiles(rel_bias, tq, tk):
    period = 2 * (tq + tk)
    delta = jnp.arange(period, dtype=jnp.int32)
    delta = jnp.where(delta >= period // 2, delta - period, delta)
    n_dist = tq + tk
    rel = ((rel_bias[_t5_causal_bucket(jnp.arange(n_dist, dtype=jnp.int32))]
            - rel_bias[NUM_BUCKETS - 1]) * _LOG2E).T

    def vec(offset):
        dist = delta + offset
        vals = rel[:, jnp.clip(dist, 0, n_dist - 1)]
        return jnp.where(dist >= 0, vals, -jnp.inf)

    g_vec = vec(0).reshape(N_DIFF_HEADS, 2, period)
    return _toeplitz_pair(g_vec, tk, tq)


def _lambda_init_for(layer_idx):
    return 0.8 - 0.6 * math.exp(-0.3 * layer_idx)


def kernel(x, norm_g, w_in, w_out, conv_w, lambda_q1, lambda_k1, lambda_q2, lambda_k2, subln_g,
           rel_bias, final_g):
    bsz, seq, d = x.shape
    depth = norm_g.shape[0]
    w = w_out.shape[1]
    tq, tk = ATTN_Q_TILE, ATTN_K_TILE
    assert seq % tq == 0 and seq % ROW_TILE == 0 and w == N_DIFF_HEADS * V_DIM
    assert (bsz * seq) % ATTN_ROW_TILE == 0
    assert N_DIFF_HEADS % ATTN_HEADS_PER_STEP == 0
    assert depth % N_MIXERS == 0

    bias_tiles = _bias_tiles(rel_bias, tq, tk)
    x2 = x.reshape(bsz * seq, d)
    for i in range(depth):
        j = i // N_MIXERS
        if i % N_MIXERS == 0:
            x2 = _conv_layer(x2, norm_g[i], w_in[i], w_out[i], conv_w[j], seq)
        else:
            w_in_bf = w_in[i].astype(_BF16)
            q, k, vt = _qkv_proj(x2, norm_g[i], w_in_bf)
            lam_params = jnp.stack([lambda_q1[j], lambda_k1[j], lambda_q2[j], lambda_k2[j]])
            o = _attention(q, k, vt, bias_tiles, lam_params, subln_g[j], _lambda_init_for(i),
                           bsz)
            x2 = _attn_out(x2, o.reshape(bsz * seq, w), norm_g[i], w_in_bf,
                           w_out[i].astype(_BF16), final_g, final_norm=(i == depth - 1))
    return x2.reshape(bsz, seq, d)
```

```python
import functools
import math

import jax
import jax.numpy as jnp
from jax import lax
from jax.experimental import pallas as pl
from jax.experimental.pallas import tpu as pltpu

N_MIXERS = 2
CONV_K = 3
N_DIFF_HEADS = 8
HEAD_DIM = 64
V_DIM = 2 * HEAD_DIM
N_MAPS = 2 * N_DIFF_HEADS
NUM_BUCKETS = 32
MAX_EXACT = NUM_BUCKETS // 2
REL_MAX_DIST = 128
RMS_EPS = 1e-6

SUBLANES = 8
BF16_SUBLANES = 16
ROW_TILE = 1024
CONV_SUB_ROWS = 256
ATTN_ROW_TILE = 1024
QKV_SUB_ROWS = 512
ATTN_Q_TILE = 512
ATTN_K_TILE = 256
ATTN_HEADS_PER_STEP = 4
VT_ROWS = V_DIM + BF16_SUBLANES
VMEM_LIMIT_BYTES = 56 * 1024 * 1024

assert ATTN_Q_TILE == 2 * ATTN_K_TILE and ATTN_K_TILE >= REL_MAX_DIST
assert ATTN_ROW_TILE % ATTN_K_TILE == 0

_F32 = jnp.float32
_BF16 = jnp.bfloat16
_LOG2E = math.log2(math.e)


def _dot(a, b):
    return jnp.dot(a, b, preferred_element_type=_F32)


def _rmsnorm_rows(x, g):
    return x * lax.rsqrt(jnp.mean(x * x, axis=-1, keepdims=True) + RMS_EPS) * g


def _const_spec(shape):
    return pl.BlockSpec(shape, lambda *_: (0,) * len(shape), pipeline_mode=pl.Buffered(1))


def _params(semantics):
    return pltpu.CompilerParams(dimension_semantics=semantics,
                                vmem_limit_bytes=VMEM_LIMIT_BYTES)


def _shift_rows(v, prev, s):
    r = pltpu.roll(v, s, 0)
    p = pltpu.roll(prev, s, 0)
    row = lax.broadcasted_iota(jnp.int32, prev.shape, 0)
    head = jnp.where(row < s, p, r[:SUBLANES])
    return jnp.concatenate([head, r[SUBLANES:]], axis=0)


def _conv_rows(x, prev, g_ref, win_ref, wout_ref, cw_ref, w):
    h = _rmsnorm_rows(x, g_ref[...]).astype(_BF16)
    b_gate = _dot(h, win_ref[:, 0:w])
    c_gate = _dot(h, win_ref[:, w:2 * w])
    u = _dot(h, win_ref[:, 2 * w:3 * w])
    z = _dot(h, win_ref[:, 3 * w:4 * w])
    v = c_gate * u
    conv = (_shift_rows(v, prev, 2) * cw_ref[0:1, :]
            + _shift_rows(v, prev, 1) * cw_ref[1:2, :] + v * cw_ref[2:3, :])
    y = (b_gate * conv * jax.nn.silu(z)).astype(_BF16)
    return x + _dot(y, wout_ref[...]), v[v.shape[0] - SUBLANES:]


def _conv_layer_kernel(x_ref, g_ref, win_ref, wout_ref, cw_ref, o_ref, carry_ref, *,
                       tiles_per_seq, width, sub_rows):
    w = width

    @pl.when(pl.program_id(0) % tiles_per_seq == 0)
    def _():
        carry_ref[...] = jnp.zeros_like(carry_ref)

    prev = carry_ref[...]
    for r in range(x_ref.shape[0] // sub_rows):
        rows = slice(r * sub_rows, (r + 1) * sub_rows)
        o_ref[rows, :], prev = _conv_rows(x_ref[rows, :], prev, g_ref, win_ref, wout_ref,
                                          cw_ref, w)
    carry_ref[...] = prev


def _conv_layer(x2, g, w_in, w_out, conv_w, seq):
    n, d = x2.shape
    w = w_out.shape[0]
    tm = ROW_TILE
    return pl.pallas_call(
        functools.partial(_conv_layer_kernel, tiles_per_seq=seq // tm, width=w,
                          sub_rows=CONV_SUB_ROWS),
        grid=(n // tm,),
        in_specs=[
            pl.BlockSpec((tm, d), lambda i: (i, 0)),
            _const_spec((1, d)),
            _const_spec((d, 4 * w)),
            _const_spec((w, d)),
            _const_spec((CONV_K, w)),
        ],
        out_specs=pl.BlockSpec((tm, d), lambda i: (i, 0)),
        out_shape=jax.ShapeDtypeStruct((n, d), _F32),
        scratch_shapes=[pltpu.VMEM((SUBLANES, w), _F32)],
        compiler_params=_params(("arbitrary",)),
        name="conv_layer",
    )(x2, g.reshape(1, d), w_in.astype(_BF16), w_out.astype(_BF16), conv_w.T)


def _qkv_kernel(x_ref, g_ref, w_ref, qt_ref, k_ref, vt_ref, *, width, sub_rows):
    w = width
    n_tiles, n_heads, vt_rows, tk = vt_ref.shape
    row = lax.broadcasted_iota(jnp.int32, (vt_rows - V_DIM, tk), 0)
    ones_rows = jnp.where(row == 0, 1.0, 0.0).astype(_BF16)
    tiles_per_sub = sub_rows // tk
    for r in range(x_ref.shape[0] // sub_rows):
        h = _rmsnorm_rows(x_ref[r * sub_rows:(r + 1) * sub_rows, :], g_ref[...]).astype(_BF16)
        q = _dot(h, w_ref[:, 0:w]) * (HEAD_DIM ** -0.5 * _LOG2E)
        k = _dot(h, w_ref[:, w:2 * w]).astype(_BF16)
        v = _dot(h, w_ref[:, 2 * w:3 * w])
        for t in range(tiles_per_sub):
            rows = slice(t * tk, (t + 1) * tk)
            tile = r * tiles_per_sub + t
            for hd in range(n_heads):
                cols = slice(hd * V_DIM, (hd + 1) * V_DIM)
                qt_ref[tile, hd] = q[rows, cols].T.astype(_BF16)
                k_ref[tile, hd] = k[rows, cols]
                vt_ref[tile, hd, 0:V_DIM, :] = v[rows, cols].T.astype(_BF16)
                vt_ref[tile, hd, V_DIM:, :] = ones_rows


def _qkv_proj(x2, g, w_in):
    n, d = x2.shape
    w = w_in.shape[1] // 4
    tm, tk = ATTN_ROW_TILE, ATTN_K_TILE
    n_heads = w // V_DIM
    out = jax.ShapeDtypeStruct((n // tk, n_heads, tk, V_DIM), _BF16)
    head_tiles = pl.BlockSpec((tm // tk, n_heads, tk, V_DIM), lambda i: (i, 0, 0, 0))
    return pl.pallas_call(
        functools.partial(_qkv_kernel, width=w, sub_rows=QKV_SUB_ROWS),
        grid=(n // tm,),
        in_specs=[
            pl.BlockSpec((tm, d), lambda i: (i, 0)),
            _const_spec((1, d)),
            _const_spec((d, 3 * w)),
        ],
        out_specs=[pl.BlockSpec((tm // tk, n_heads, V_DIM, tk), lambda i: (i, 0, 0, 0)), head_tiles,
                   pl.BlockSpec((tm // tk, n_heads, VT_ROWS, tk), lambda i: (i, 0, 0, 0))],
        out_shape=[jax.ShapeDtypeStruct((n // tk, n_heads, V_DIM, tk), _BF16), out,
                   jax.ShapeDtypeStruct((n // tk, n_heads, VT_ROWS, tk), _BF16)],
        compiler_params=_params(("arbitrary",)),
        name="qkv_proj",
    )(x2, g.reshape(1, d), w_in)


def _attn_kernel(qt_ref, k_ref, vt_ref, bias_ref, lam_ref, sg_ref, o_ref, *scratch, lambda_init):
    n_heads = qt_ref.shape[1]
    per_head = len(scratch) // n_heads
    heads = tuple(scratch[h * per_head:(h + 1) * per_head] for h in range(n_heads))
    tk = vt_ref.shape[-1]
    tq = qt_ref.shape[0] * tk
    qi = pl.program_id(2)

    for h in range(n_heads):
        qq_sc, m_sc, acc_sc = heads[h][:3]
        q_t = jnp.concatenate([qt_ref[t, h] for t in range(qt_ref.shape[0])], axis=1)
        row = lax.broadcasted_iota(jnp.int32, q_t.shape, 0)
        zero = jnp.zeros_like(q_t)
        qq_sc[:, 0:tq] = jnp.where(row < HEAD_DIM, q_t, zero)
        qq_sc[:, tq:2 * tq] = jnp.where(row >= HEAD_DIM, q_t, zero)
        m_sc[...] = jnp.full_like(m_sc, -jnp.inf)
        acc_sc[...] = jnp.zeros_like(acc_sc)

    hq = tq // 2
    near = tk // 2
    assert near >= REL_MAX_DIST and near <= hq

    def near_bias(h):
        zeros = jnp.zeros((near, tq - near), _F32)
        return jnp.concatenate([bias_ref[h, 0:near, near:2 * near], zeros,
                                bias_ref[h, 0:near, tq + near:tq + 2 * near], zeros], axis=1)

    def stage_a(u, before_diag=False):
        for h in range(n_heads):
            qq_sc, _, _, s_sc, t_sc, _, _ = heads[h]
            keys = k_ref[pl.ds(2 * u, 2), h].reshape(2 * tk, V_DIM)
            s = _dot(keys, qq_sc[...])
            if before_diag:
                s = jnp.concatenate([s[:2 * tk - near], s[2 * tk - near:] + near_bias(h)], axis=0)
            s_sc[...] = s
            t_sc[...] = jnp.max(s, axis=0, keepdims=True)

    def late_queries(x):
        return jnp.concatenate([x[:, hq:tq], x[:, tq + hq:]], axis=1)

    def stage_a_diag(u):
        for h in range(n_heads):
            qq_sc, _, _, s_sc, t_sc, _, _ = heads[h]
            s_lo = _dot(k_ref[2 * u, h], qq_sc[...]) + bias_ref[h]
            s_hi = _dot(k_ref[2 * u + 1, h], late_queries(qq_sc[...]))
            s_hi = s_hi + jnp.concatenate([bias_ref[h, :, 0:hq], bias_ref[h, :, tq:tq + hq]],
                                          axis=1)
            masked = jnp.full((tk, hq), -jnp.inf, _F32)
            s_sc[0:tk, :] = s_lo
            s_sc[tk:, :] = jnp.concatenate([masked, s_hi[:, :hq], masked, s_hi[:, hq:]], axis=1)
            max_lo = jnp.max(s_lo, axis=0, keepdims=True)
            max_hi = jnp.max(s_hi, axis=0, keepdims=True)
            t_sc[...] = jnp.concatenate(
                [max_lo[:, :hq], jnp.maximum(max_lo[:, hq:tq], max_hi[:, :hq]),
                 max_lo[:, tq:tq + hq], jnp.maximum(max_lo[:, tq + hq:], max_hi[:, hq:])], axis=1)

    def stage_b():
        for h in range(n_heads):
            _, m_sc, _, s_sc, t_sc, p_sc, a_sc = heads[h]
            m_prev = m_sc[...]
            m_new = jnp.maximum(m_prev, t_sc[...])
            a_sc[...] = jnp.exp2(m_prev - m_new)
            p_sc[...] = jnp.exp2(s_sc[...] - m_new).astype(_BF16)
            m_sc[...] = m_new

    def stage_c(u):
        for h in range(n_heads):
            _, _, acc_sc, _, _, p_sc, a_sc = heads[h]
            vt = jnp.concatenate([vt_ref[2 * u, h], vt_ref[2 * u + 1, h]], axis=1)
            acc_sc[...] = acc_sc[...] * a_sc[...] + _dot(vt, p_sc[...])

    def stage_c_diag(u):
        for h in range(n_heads):
            _, _, acc_sc, _, _, p_sc, a_sc = heads[h]
            pv = _dot(vt_ref[2 * u, h], p_sc[0:tk, :])
            pv_late = _dot(vt_ref[2 * u + 1, h], late_queries(p_sc[tk:, :]))
            pv = jnp.concatenate([pv[:, :hq], pv[:, hq:tq] + pv_late[:, :hq],
                                  pv[:, tq:tq + hq], pv[:, tq + hq:] + pv_late[:, hq:]], axis=1)
            acc_sc[...] = acc_sc[...] * a_sc[...] + pv

    plain_step, before_diag, diag_step = 0, 1, 2

    def trip(u, next_step=plain_step, issue_a=True, issue_c=True):
        if issue_c:
            stage_c(u - 1)
        stage_b()
        if issue_a and next_step == diag_step:
            stage_a_diag(u + 1)
        elif issue_a:
            stage_a(u + 1, before_diag=next_step == before_diag)

    short = ([], [before_diag], [plain_step, before_diag])

    for n_q, step_biases in enumerate(short):
        @pl.when(qi == n_q)
        def _(step_biases=step_biases):
            n_before = len(step_biases)
            if n_before:
                stage_a(0, before_diag=step_biases[0] == before_diag)
            else:
                stage_a_diag(0)
            for u in range(n_before + 1):
                next_step = step_biases[u + 1] if u + 1 < n_before else diag_step
                trip(u, next_step, issue_a=u < n_before, issue_c=u > 0)
            stage_c_diag(n_before)

    @pl.when(qi >= len(short))
    def _():
        stage_a(0)
        trip(0, issue_c=False)

        def plain(u, carry):
            trip(u)
            return carry

        lax.fori_loop(1, qi - 2, plain, 0)
        trip(qi - 2, before_diag)
        trip(qi - 1, diag_step)
        trip(qi, issue_a=False)
        stage_c_diag(qi)

    lam = (jnp.exp(jnp.sum(lam_ref[0:1, :] * lam_ref[1:2, :], axis=-1, keepdims=True))
           - jnp.exp(jnp.sum(lam_ref[2:3, :] * lam_ref[3:4, :], axis=-1, keepdims=True))
           + lambda_init)
    for h in range(n_heads):
        acc_sc = heads[h][2]
        o_maps = acc_sc[0:V_DIM, :] * (1.0 / acc_sc[V_DIM:V_DIM + 1, :])
        o_t = o_maps[:, :tq] - lam * o_maps[:, tq:]
        o_t = o_t * lax.rsqrt(jnp.mean(o_t * o_t, axis=0, keepdims=True) + RMS_EPS)
        o_ref[:, h * V_DIM:(h + 1) * V_DIM] = (
            o_t.T * sg_ref[...] * (1.0 - lambda_init)).astype(o_ref.dtype)


def _attention(q, k, vt, bias_tiles, lam_params, subln_g, lambda_init, bsz):
    tq, tk, hps = ATTN_Q_TILE, ATTN_K_TILE, ATTN_HEADS_PER_STEP
    n_heads = k.shape[1]
    s = k.shape[0] * tk // bsz
    w = n_heads * V_DIM
    q_tiles = s // tq
    col = pltpu.VMEM((1, 2 * tq), _F32)
    per_head_scratch = [
        pltpu.VMEM((V_DIM, 2 * tq), _BF16),
        col,
        pltpu.VMEM((VT_ROWS, 2 * tq), _F32),
        pltpu.VMEM((2 * tk, 2 * tq), _F32),
        col,
        pltpu.VMEM((2 * tk, 2 * tq), _BF16),
        col,
    ]
    return pl.pallas_call(
        functools.partial(_attn_kernel, lambda_init=lambda_init),
        grid=(n_heads // hps, bsz, q_tiles),
        in_specs=[
            pl.BlockSpec((tq // tk, hps, V_DIM, tk), lambda h, b, i: (b * q_tiles + i, h, 0, 0)),
            pl.BlockSpec((s // tk, hps, tk, V_DIM), lambda h, b, i: (b, h, 0, 0)),
            pl.BlockSpec((s // tk, hps, VT_ROWS, tk), lambda h, b, i: (b, h, 0, 0),
                         pipeline_mode=pl.Buffered(1)),
            pl.BlockSpec((hps, tk, 2 * tq), lambda h, b, i: (h, 0, 0),
                         pipeline_mode=pl.Buffered(1)),
            _const_spec((4, HEAD_DIM)),
            _const_spec((1, V_DIM)),
        ],
        out_specs=pl.BlockSpec((None, tq, hps * V_DIM), lambda h, b, i: (b, i, h)),
        out_shape=jax.ShapeDtypeStruct((bsz, s, w), _BF16),
        scratch_shapes=per_head_scratch * hps,
        compiler_params=_params(("arbitrary", "arbitrary", "arbitrary")),
        name="diff_attention",
    )(q, k, vt, bias_tiles, lam_params, subln_g.reshape(1, V_DIM))


def _attn_out_kernel(x_ref, o_ref, g_ref, wz_ref, wout_ref, fg_ref, y_ref, *, final_norm):
    x = x_ref[...]
    h = _rmsnorm_rows(x, g_ref[...]).astype(_BF16)
    z = _dot(h, wz_ref[...])
    y = (o_ref[...] * jax.nn.silu(z)).astype(_BF16)
    x_new = x + _dot(y, wout_ref[...])
    if final_norm:
        x_new = _rmsnorm_rows(x_new, fg_ref[...])
    y_ref[...] = x_new


def _attn_out(x2, o2, g, w_in, w_out, final_g, final_norm):
    n, d = x2.shape
    w = w_out.shape[0]
    tm = ATTN_ROW_TILE
    return pl.pallas_call(
        functools.partial(_attn_out_kernel, final_norm=final_norm),
        grid=(n // tm,),
        in_specs=[
            pl.BlockSpec((tm, d), lambda i: (i, 0)),
            pl.BlockSpec((tm, w), lambda i: (i, 0)),
            _const_spec((1, d)),
            pl.BlockSpec((d, w), lambda i: (0, 3), pipeline_mode=pl.Buffered(1)),
            _const_spec((w, d)),
            _const_spec((1, d)),
        ],
        out_specs=pl.BlockSpec((tm, d), lambda i: (i, 0)),
        out_shape=jax.ShapeDtypeStruct((n, d), _F32),
        compiler_params=_params(("arbitrary",)),
        name="attn_out",
    )(x2, o2, g.reshape(1, d), w_in, w_out, final_g.reshape(1, d))


def _attn_out_conv_kernel(x_ref, o_ref, ga_ref, wz_ref, wouta_ref, gc_ref, win_ref, woutc_ref,
                          cw_ref, y_ref, carry_ref, *, tiles_per_seq, width, sub_rows):
    @pl.when(pl.program_id(0) % tiles_per_seq == 0)
    def _():
        carry_ref[...] = jnp.zeros_like(carry_ref)

    prev = carry_ref[...]
    for r in range(x_ref.shape[0] // sub_rows):
        rows = slice(r * sub_rows, (r + 1) * sub_rows)
        x = x_ref[rows, :]
        h = _rmsnorm_rows(x, ga_ref[...]).astype(_BF16)
        gated = (o_ref[rows, :] * jax.nn.silu(_dot(h, wz_ref[...]))).astype(_BF16)
        x = x + _dot(gated, wouta_ref[...])
        y_ref[rows, :], prev = _conv_rows(x, prev, gc_ref, win_ref, woutc_ref, cw_ref, width)
    carry_ref[...] = prev


def _attn_out_conv(x2, o2, g_attn, w_in_attn, w_out_attn, g_conv, w_in_conv, w_out_conv, conv_w,
                   seq):
    n, d = x2.shape
    w = w_out_attn.shape[0]
    tm = ROW_TILE
    return pl.pallas_call(
        functools.partial(_attn_out_conv_kernel, tiles_per_seq=seq // tm, width=w,
                          sub_rows=CONV_SUB_ROWS),
        grid=(n // tm,),
        in_specs=[
            pl.BlockSpec((tm, d), lambda i: (i, 0)),
            pl.BlockSpec((tm, w), lambda i: (i, 0)),
            _const_spec((1, d)),
            pl.BlockSpec((d, w), lambda i: (0, 3), pipeline_mode=pl.Buffered(1)),
            _const_spec((w, d)),
            _const_spec((1, d)),
            _const_spec((d, 4 * w)),
            _const_spec((w, d)),
            _const_spec((CONV_K, w)),
        ],
        out_specs=pl.BlockSpec((tm, d), lambda i: (i, 0)),
        out_shape=jax.ShapeDtypeStruct((n, d), _F32),
        scratch_shapes=[pltpu.VMEM((SUBLANES, w), _F32)],
        compiler_params=_params(("arbitrary",)),
        name="attn_out_conv",
    )(x2, o2, g_attn.reshape(1, d), w_in_attn, w_out_attn, g_conv.reshape(1, d),
      w_in_conv.astype(_BF16), w_out_conv.astype(_BF16), conv_w.T)


def _t5_causal_bucket(dist):
    d_safe = jnp.maximum(dist, 1).astype(_F32)
    large = MAX_EXACT + (jnp.log(d_safe / MAX_EXACT) / math.log(REL_MAX_DIST / MAX_EXACT)
                         * (NUM_BUCKETS - MAX_EXACT)).astype(jnp.int32)
    large = jnp.minimum(large, NUM_BUCKETS - 1)
    return jnp.where(dist < MAX_EXACT, dist, large)


def _toeplitz_pair(g_vec, rows, cols):
    period = g_vec.shape[-1]
    base = jnp.stack([jnp.roll(g_vec, r, axis=-1) for r in range(SUBLANES)], axis=-2)
    groups = []
    for a in range(rows // SUBLANES):
        shift = a * SUBLANES
        pieces = []
        for m in range(2):
            bm = base[..., m, :, :]
            pieces += [bm[..., period - shift:], bm[..., :cols - shift]]
        groups.append(jnp.concatenate(pieces, axis=-1))
    out = jnp.stack(groups, axis=-3)
    return out.reshape(out.shape[:-3] + (rows, 2 * cols))


def _bias_tiles(rel_bias, tq, tk):
    period = 2 * (tq + tk)
    delta = jnp.arange(period, dtype=jnp.int32)
    delta = jnp.where(delta >= period // 2, delta - period, delta)
    n_dist = tq + tk
    rel = ((rel_bias[_t5_causal_bucket(jnp.arange(n_dist, dtype=jnp.int32))]
            - rel_bias[NUM_BUCKETS - 1]) * _LOG2E).T

    def vec(offset):
        dist = delta + offset
        vals = rel[:, jnp.clip(dist, 0, n_dist - 1)]
        return jnp.where(dist >= 0, vals, -jnp.inf)

    g_vec = vec(0).reshape(N_DIFF_HEADS, 2, period)
    return _toeplitz_pair(g_vec, tk, tq)


def _lambda_init_for(layer_idx):
    return 0.8 - 0.6 * math.exp(-0.3 * layer_idx)


def kernel(x, norm_g, w_in, w_out, conv_w, lambda_q1, lambda_k1, lambda_q2, lambda_k2, subln_g,
           rel_bias, final_g):
    bsz, seq, d = x.shape
    depth = norm_g.shape[0]
    w = w_out.shape[1]
    tq, tk = ATTN_Q_TILE, ATTN_K_TILE
    assert seq % tq == 0 and seq % ROW_TILE == 0 and w == N_DIFF_HEADS * V_DIM
    assert (bsz * seq) % ATTN_ROW_TILE == 0
    assert N_DIFF_HEADS % ATTN_HEADS_PER_STEP == 0
    assert depth % N_MIXERS == 0

    bias_tiles = _bias_tiles(rel_bias, tq, tk)
    x2 = x.reshape(bsz * seq, d)
    for i in range(depth):
        j = i // N_MIXERS
        if i % N_MIXERS == 0:
            if i == 0:
                x2 = _conv_layer(x2, norm_g[i], w_in[i], w_out[i], conv_w[j], seq)
        else:
            w_in_bf = w_in[i].astype(_BF16)
            q, k, vt = _qkv_proj(x2, norm_g[i], w_in_bf)
            lam_params = jnp.stack([lambda_q1[j], lambda_k1[j], lambda_q2[j], lambda_k2[j]])
            o = _attention(q, k, vt, bias_tiles, lam_params, subln_g[j], _lambda_init_for(i),
                           bsz)
            o2 = o.reshape(bsz * seq, w)
            if i == depth - 1:
                x2 = _attn_out(x2, o2, norm_g[i], w_in_bf, w_out[i].astype(_BF16), final_g,
                               final_norm=True)
            else:
                x2 = _attn_out_conv(x2, o2, norm_g[i], w_in_bf, w_out[i].astype(_BF16),
                                    norm_g[i + 1], w_in[i + 1], w_out[i + 1], conv_w[j + 1], seq)
    return x2.reshape(bsz, seq, d)
```
